```python
import jax, jax.numpy as jnp
from jax import lax
import numpy as np

D_MODEL = 1024
BATCH = 8
SEQ = 8192
DEPTH = 4

N_HEADS = 16
N_KV_HEADS = 2
HEAD_DIM = 64
GROUP = N_HEADS // N_KV_HEADS
ROT_DIM = HEAD_DIM // 4
ROPE_THETA = 500000.0
WINDOW = 128
BLOCK = 128
CONV_CH = D_MODEL // 2
CONV_WIDTH = 31
D_FF = -(-(8 * D_MODEL) // (3 * 256)) * 256
EPS = 1e-6
Q_W = N_HEADS * HEAD_DIM
KV_W = N_KV_HEADS * HEAD_DIM
IN_W = Q_W + 2 * KV_W + 2 * CONV_CH + 2 * D_MODEL
SPLITS = tuple(int(s) for s in np.cumsum([Q_W, KV_W, KV_W, CONV_CH, CONV_CH, D_MODEL])[:])

kernel_name = "hybrid_swa_sink_conformer_gated"


def rmsnorm(x, g):
    xf = x.astype(jnp.float32)
    y = xf * lax.rsqrt(jnp.mean(xf * xf, axis=-1, keepdims=True) + EPS)
    return (y * g.astype(jnp.float32)).astype(x.dtype)


def layernorm(x, g, b):
    xf = x.astype(jnp.float32)
    mu = jnp.mean(xf, axis=-1, keepdims=True)
    var = jnp.mean(jnp.square(xf - mu), axis=-1, keepdims=True)
    y = (xf - mu) * lax.rsqrt(var + EPS)
    return (y * g.astype(jnp.float32) + b.astype(jnp.float32)).astype(x.dtype)


def rope_tables(seq):
    inv_freq = ROPE_THETA ** (-jnp.arange(0, ROT_DIM, 2, dtype=jnp.float32) / ROT_DIM)
    ang = jnp.arange(seq, dtype=jnp.float32)[:, None] * inv_freq[None, :]
    return jnp.cos(ang), jnp.sin(ang)


def partial_rope(x, cos, sin):
    half = ROT_DIM // 2
    c = cos[None, :, None, :].astype(x.dtype)
    s = sin[None, :, None, :].astype(x.dtype)
    x1, x2, xp = x[..., :half], x[..., half:ROT_DIM], x[..., ROT_DIM:]
    return jnp.concatenate([x1 * c - x2 * s, x2 * c + x1 * s, xp], axis=-1)


def sliding_window_attention(q, k, v, sinks):
    B, T = q.shape[0], q.shape[1]
    nb = T // BLOCK
    qb = q.reshape(B, nb, BLOCK, N_KV_HEADS, GROUP, HEAD_DIM)

    def band(t):
        tp = jnp.pad(t, ((0, 0), (BLOCK, 0), (0, 0), (0, 0)))
        tp = tp.reshape(B, nb + 1, BLOCK, N_KV_HEADS, HEAD_DIM)
        return jnp.concatenate([tp[:, :-1], tp[:, 1:]], axis=2)

    kb, vb = band(k), band(v)
    scale = HEAD_DIM ** -0.5
    s = jnp.einsum('bnqkgd,bnskd->bnkgqs', qb, kb,
                   preferred_element_type=jnp.float32) * scale
    qi = jnp.arange(BLOCK)[:, None]
    sj = jnp.arange(2 * BLOCK)[None, :]
    rel = qi + BLOCK - sj
    kpos = jnp.arange(nb)[:, None, None] * BLOCK - BLOCK + sj
    mask = (rel >= 0) & (rel < WINDOW) & (kpos >= 0)
    s = jnp.where(mask[None, :, None, None], s, -jnp.inf)
    sink = sinks.astype(jnp.float32).reshape(N_KV_HEADS, GROUP)[None, None, :, :, None, None]
    m = jnp.maximum(jnp.max(s, axis=-1, keepdims=True), sink)
    p = jnp.exp(s - m)
    p = p / (jnp.sum(p, axis=-1, keepdims=True) + jnp.exp(sink - m))
    o = jnp.einsum('bnkgqs,bnskd->bnqkgd', p.astype(v.dtype), vb)
    return o.reshape(B, T, N_HEADS * HEAD_DIM)


def conformer_conv(u, ug, w_dw, b_dw, ln_g, ln_b, w_pw):
    a = u * jax.nn.sigmoid(ug)
    y = lax.conv_general_dilated(
        a, w_dw[:, None, :].astype(a.dtype), window_strides=(1,),
        padding=[(CONV_WIDTH - 1, 0)],
        dimension_numbers=('NWC', 'WIO', 'NWC'),
        feature_group_count=CONV_CH) + b_dw.astype(a.dtype)
    y = layernorm(y, ln_g, ln_b)
    y = jax.nn.silu(y)
    return jnp.einsum('btc,cd->btd', y, w_pw)


def _fwd_setup_inputs(seed: int = 0) -> dict:
    key = jax.random.key(seed)
    ks = jax.random.split(key, 16)
    f32 = jnp.float32

    def nrm(k, shape, scale):
        return jax.random.normal(k, shape, f32) * scale

    return {
        "x": nrm(ks[0], (BATCH, SEQ, D_MODEL), 1.0),
        "norm_mix": 1.0 + nrm(ks[1], (DEPTH, D_MODEL), 0.02),
        "w_in": nrm(ks[2], (DEPTH, D_MODEL, IN_W), D_MODEL ** -0.5),
        "q_norm": 1.0 + nrm(ks[3], (DEPTH, HEAD_DIM), 0.02),
        "k_norm": 1.0 + nrm(ks[4], (DEPTH, HEAD_DIM), 0.02),
        "sinks": nrm(ks[5], (DEPTH, N_HEADS), 0.5),
        "conv_w": nrm(ks[6], (DEPTH, CONV_WIDTH, CONV_CH), CONV_WIDTH ** -0.5),
        "conv_b": nrm(ks[7], (DEPTH, CONV_CH), 0.02),
        "conv_ln_g": 1.0 + nrm(ks[8], (DEPTH, CONV_CH), 0.02),
        "conv_ln_b": nrm(ks[9], (DEPTH, CONV_CH), 0.02),
        "w_conv_out": nrm(ks[10], (DEPTH, CONV_CH, D_MODEL), CONV_CH ** -0.5),
        "w_out": nrm(ks[11], (DEPTH, D_MODEL, D_MODEL), D_MODEL ** -0.5),
        "norm_ffn": 1.0 + nrm(ks[12], (DEPTH, D_MODEL), 0.02),
        "w_gate_up": nrm(ks[13], (DEPTH, D_MODEL, 2 * D_FF), D_MODEL ** -0.5),
        "w_down": nrm(ks[14], (DEPTH, D_FF, D_MODEL), D_FF ** -0.5),
    }


def _fwd_reference(x, norm_mix, w_in, q_norm, k_norm, sinks, conv_w, conv_b, conv_ln_g,
              conv_ln_b, w_conv_out, w_out, norm_ffn, w_gate_up, w_down):
    B, T = x.shape[0], x.shape[1]
    cos, sin = rope_tables(T)
    for l in range(DEPTH):
        h = rmsnorm(x, norm_mix[l])
        proj = jnp.einsum('btd,de->bte', h, w_in[l])
        q, k, v, u, ug, ga, gb = jnp.split(proj, SPLITS, axis=-1)
        q = q.reshape(B, T, N_HEADS, HEAD_DIM)
        k = k.reshape(B, T, N_KV_HEADS, HEAD_DIM)
        v = v.reshape(B, T, N_KV_HEADS, HEAD_DIM)
        q = partial_rope(rmsnorm(q, q_norm[l]), cos, sin)
        k = partial_rope(rmsnorm(k, k_norm[l]), cos, sin)
        a_out = sliding_window_attention(q, k, v, sinks[l])
        c_out = conformer_conv(u, ug, conv_w[l], conv_b[l], conv_ln_g[l],
                               conv_ln_b[l], w_conv_out[l])
        merged = jax.nn.sigmoid(ga) * a_out + jax.nn.sigmoid(gb) * c_out
        x = x + jnp.einsum('btd,de->bte', merged, w_out[l])
        h2 = rmsnorm(x, norm_ffn[l])
        gu = jnp.einsum('btd,df->btf', h2, w_gate_up[l])
        g, up = jnp.split(gu, 2, axis=-1)
        x = x + jnp.einsum('btf,fd->btd', jax.nn.silu(g) * up, w_down[l])
    return x


import jax as _jax
import jax.numpy as _jnp

TWIN_FORMAT = 'train_step'
FWD_PARAMS = ['x', 'norm_mix', 'w_in', 'q_norm', 'k_norm', 'sinks', 'conv_w', 'conv_b', 'conv_ln_g', 'conv_ln_b', 'w_conv_out', 'w_out', 'norm_ffn', 'w_gate_up', 'w_down']
TWIN_WEIGHTS = ['norm_mix', 'w_in', 'q_norm', 'k_norm', 'sinks', 'conv_w', 'conv_b', 'conv_ln_g', 'conv_ln_b', 'w_conv_out', 'w_out', 'norm_ffn', 'w_gate_up', 'w_down']
TWIN_DIFF_INPUT = 'x'
TWIN_INPUTS = ['x', 'norm_mix', 'w_in', 'q_norm', 'k_norm', 'sinks', 'conv_w', 'conv_b', 'conv_ln_g', 'conv_ln_b', 'w_conv_out', 'w_out', 'norm_ffn', 'w_gate_up', 'w_down', 'loss_target', 'm_norm_mix', 'm_w_in', 'm_q_norm', 'm_k_norm', 'm_sinks', 'm_conv_w', 'm_conv_b', 'm_conv_ln_g', 'm_conv_ln_b', 'm_w_conv_out', 'm_w_out', 'm_norm_ffn', 'm_w_gate_up', 'm_w_down', 'v_norm_mix', 'v_w_in', 'v_q_norm', 'v_k_norm', 'v_sinks', 'v_conv_w', 'v_conv_b', 'v_conv_ln_g', 'v_conv_ln_b', 'v_w_conv_out', 'v_w_out', 'v_norm_ffn', 'v_w_gate_up', 'v_w_down']
TWIN_OUTPUTS = ['loss', 'grad_x', 'grad_norm_mix', 'grad_w_in', 'grad_q_norm', 'grad_k_norm', 'grad_sinks', 'grad_conv_w', 'grad_conv_b', 'grad_conv_ln_g', 'grad_conv_ln_b', 'grad_w_conv_out', 'grad_w_out', 'grad_norm_ffn', 'grad_w_gate_up', 'grad_w_down', 'delta_norm_mix', 'delta_w_in', 'delta_q_norm', 'delta_k_norm', 'delta_sinks', 'delta_conv_w', 'delta_conv_b', 'delta_conv_ln_g', 'delta_conv_ln_b', 'delta_w_conv_out', 'delta_w_out', 'delta_norm_ffn', 'delta_w_gate_up', 'delta_w_down', 'new_m_norm_mix', 'new_m_w_in', 'new_m_q_norm', 'new_m_k_norm', 'new_m_sinks', 'new_m_conv_w', 'new_m_conv_b', 'new_m_conv_ln_g', 'new_m_conv_ln_b', 'new_m_w_conv_out', 'new_m_w_out', 'new_m_norm_ffn', 'new_m_w_gate_up', 'new_m_w_down', 'new_v_norm_mix', 'new_v_w_in', 'new_v_q_norm', 'new_v_k_norm', 'new_v_sinks', 'new_v_conv_w', 'new_v_conv_b', 'new_v_conv_ln_g', 'new_v_conv_ln_b', 'new_v_w_conv_out', 'new_v_w_out', 'new_v_norm_ffn', 'new_v_w_gate_up', 'new_v_w_down']
TWIN_LEAF_KINDS = {'loss': 'loss', 'grad_x': 'grad_x', 'grad_norm_mix': 'grad_w', 'grad_w_in': 'grad_w', 'grad_q_norm': 'grad_w', 'grad_k_norm': 'grad_w', 'grad_sinks': 'grad_w', 'grad_conv_w': 'grad_w', 'grad_conv_b': 'grad_w', 'grad_conv_ln_g': 'grad_w', 'grad_conv_ln_b': 'grad_w', 'grad_w_conv_out': 'grad_w', 'grad_w_out': 'grad_w', 'grad_norm_ffn': 'grad_w', 'grad_w_gate_up': 'grad_w', 'grad_w_down': 'grad_w', 'delta_norm_mix': 'delta_w', 'delta_w_in': 'delta_w', 'delta_q_norm': 'delta_w', 'delta_k_norm': 'delta_w', 'delta_sinks': 'delta_w', 'delta_conv_w': 'delta_w', 'delta_conv_b': 'delta_w', 'delta_conv_ln_g': 'delta_w', 'delta_conv_ln_b': 'delta_w', 'delta_w_conv_out': 'delta_w', 'delta_w_out': 'delta_w', 'delta_norm_ffn': 'delta_w', 'delta_w_gate_up': 'delta_w', 'delta_w_down': 'delta_w', 'new_m_norm_mix': 'new_m', 'new_m_w_in': 'new_m', 'new_m_q_norm': 'new_m', 'new_m_k_norm': 'new_m', 'new_m_sinks': 'new_m', 'new_m_conv_w': 'new_m', 'new_m_conv_b': 'new_m', 'new_m_conv_ln_g': 'new_m', 'new_m_conv_ln_b': 'new_m', 'new_m_w_conv_out': 'new_m', 'new_m_w_out': 'new_m', 'new_m_norm_ffn': 'new_m', 'new_m_w_gate_up': 'new_m', 'new_m_w_down': 'new_m', 'new_v_norm_mix': 'new_v', 'new_v_w_in': 'new_v', 'new_v_q_norm': 'new_v', 'new_v_k_norm': 'new_v', 'new_v_sinks': 'new_v', 'new_v_conv_w': 'new_v', 'new_v_conv_b': 'new_v', 'new_v_conv_ln_g': 'new_v', 'new_v_conv_ln_b': 'new_v', 'new_v_w_conv_out': 'new_v', 'new_v_w_out': 'new_v', 'new_v_norm_ffn': 'new_v', 'new_v_w_gate_up': 'new_v', 'new_v_w_down': 'new_v'}


def _forward(args):
    return _fwd_reference(*[args[k] for k in FWD_PARAMS])


def _output_shape():
    def fwd():
        inp = _fwd_setup_inputs(0)
        return _fwd_reference(*[inp[k] for k in FWD_PARAMS])
    out = _jax.eval_shape(fwd)
    return out.shape, out.dtype

N_MICROBATCH = 1
ADAM_LR = 0.001
ADAM_B1 = 0.9
ADAM_B2 = 0.999
ADAM_EPS = 1e-08
ADAM_WD = 0.01
ADAM_STEP = 10
PER_EXAMPLE_BATCH_AXIS = {'x': 0, 'loss_target': 0}
SHARED_INPUTS = []
_WEIGHT_DTYPES = {'norm_mix': _jnp.float32, 'w_in': _jnp.float32, 'q_norm': _jnp.float32, 'k_norm': _jnp.float32, 'sinks': _jnp.float32, 'conv_w': _jnp.float32, 'conv_b': _jnp.float32, 'conv_ln_g': _jnp.float32, 'conv_ln_b': _jnp.float32, 'w_conv_out': _jnp.float32, 'w_out': _jnp.float32, 'norm_ffn': _jnp.float32, 'w_gate_up': _jnp.float32, 'w_down': _jnp.float32}
MOMENT_SCALE = {'norm_mix': 1.557360e+00, 'w_in': 5.190961e-01, 'q_norm': 3.410109e+00, 'k_norm': 3.411571e+00, 'sinks': 5.811743e-01, 'conv_w': 1.156951e+00, 'conv_b': 1.834741e+01, 'conv_ln_g': 1.892863e+01, 'conv_ln_b': 1.549706e+01, 'w_conv_out': 2.945808e+00, 'w_out': 2.688813e+00, 'norm_ffn': 4.923307e+01, 'w_gate_up': 6.025722e-01, 'w_down': 8.017073e-01}


def _to_microbatches(a, axis):
    t = _jnp.moveaxis(a, axis, 0)
    t = t.reshape((N_MICROBATCH, t.shape[0] // N_MICROBATCH) + t.shape[1:])
    return _jnp.moveaxis(t, 1, axis + 1)


def setup_inputs(seed: int = 0) -> dict:
    inp = _fwd_setup_inputs(seed)
    key = _jax.random.fold_in(_jax.random.key(seed), 7919)
    shape, _ = _output_shape()
    out = dict(inp)
    out["loss_target"] = _jax.random.normal(_jax.random.fold_in(key, 0), shape, _jnp.float32)
    for i, name in enumerate(TWIN_WEIGHTS):
        w = inp[name].astype(_jnp.float32)
        if MOMENT_SCALE is None:
            s = _jnp.sqrt(_jnp.mean(_jnp.square(w)) + 1e-30)
        else:
            s = MOMENT_SCALE[name]
        km, kv = _jax.random.split(_jax.random.fold_in(key, i + 1))
        out[name] = w
        out["m_" + name] = s * _jax.random.normal(km, w.shape, _jnp.float32)
        out["v_" + name] = (s * s) * _jax.random.uniform(kv, w.shape, _jnp.float32, 0.5, 1.5)
    if N_MICROBATCH > 1:
        for name, axis in PER_EXAMPLE_BATCH_AXIS.items():
            out[name] = _to_microbatches(out[name], axis)
    return {'x': out['x'], 'norm_mix': out['norm_mix'], 'w_in': out['w_in'], 'q_norm': out['q_norm'], 'k_norm': out['k_norm'], 'sinks': out['sinks'], 'conv_w': out['conv_w'], 'conv_b': out['conv_b'], 'conv_ln_g': out['conv_ln_g'], 'conv_ln_b': out['conv_ln_b'], 'w_conv_out': out['w_conv_out'], 'w_out': out['w_out'], 'norm_ffn': out['norm_ffn'], 'w_gate_up': out['w_gate_up'], 'w_down': out['w_down'], 'loss_target': out['loss_target'], 'm_norm_mix': out['m_norm_mix'], 'm_w_in': out['m_w_in'], 'm_q_norm': out['m_q_norm'], 'm_k_norm': out['m_k_norm'], 'm_sinks': out['m_sinks'], 'm_conv_w': out['m_conv_w'], 'm_conv_b': out['m_conv_b'], 'm_conv_ln_g': out['m_conv_ln_g'], 'm_conv_ln_b': out['m_conv_ln_b'], 'm_w_conv_out': out['m_w_conv_out'], 'm_w_out': out['m_w_out'], 'm_norm_ffn': out['m_norm_ffn'], 'm_w_gate_up': out['m_w_gate_up'], 'm_w_down': out['m_w_down'], 'v_norm_mix': out['v_norm_mix'], 'v_w_in': out['v_w_in'], 'v_q_norm': out['v_q_norm'], 'v_k_norm': out['v_k_norm'], 'v_sinks': out['v_sinks'], 'v_conv_w': out['v_conv_w'], 'v_conv_b': out['v_conv_b'], 'v_conv_ln_g': out['v_conv_ln_g'], 'v_conv_ln_b': out['v_conv_ln_b'], 'v_w_conv_out': out['v_w_conv_out'], 'v_w_out': out['v_w_out'], 'v_norm_ffn': out['v_norm_ffn'], 'v_w_gate_up': out['v_w_gate_up'], 'v_w_down': out['v_w_down']}


def _loss(weights, diff, rest, loss_target):
    with _jax.named_scope("forward"):
        args = {**rest, TWIN_DIFF_INPUT: diff, **{k: w.astype(_WEIGHT_DTYPES[k]) for k, w in weights.items()}}
        y = _forward(args)
    with _jax.named_scope("loss_head"):
        err = _jnp.square(y.astype(_jnp.float32) - loss_target)
        return 0.5 * _jnp.sum(_jnp.mean(err, axis=-1)) if err.ndim else 0.5 * err


def _adamw(w, g, m, v):
    m = ADAM_B1 * m + (1.0 - ADAM_B1) * g
    v = ADAM_B2 * v + (1.0 - ADAM_B2) * _jnp.square(g)
    m_hat = m / (1.0 - ADAM_B1 ** ADAM_STEP)
    v_hat = v / (1.0 - ADAM_B2 ** ADAM_STEP)
    delta = -ADAM_LR * (m_hat / (_jnp.sqrt(v_hat) + ADAM_EPS) + ADAM_WD * w)
    return delta, m, v


def reference(x, norm_mix, w_in, q_norm, k_norm, sinks, conv_w, conv_b, conv_ln_g, conv_ln_b, w_conv_out, w_out, norm_ffn, w_gate_up, w_down, loss_target, m_norm_mix, m_w_in, m_q_norm, m_k_norm, m_sinks, m_conv_w, m_conv_b, m_conv_ln_g, m_conv_ln_b, m_w_conv_out, m_w_out, m_norm_ffn, m_w_gate_up, m_w_down, v_norm_mix, v_w_in, v_q_norm, v_k_norm, v_sinks, v_conv_w, v_conv_b, v_conv_ln_g, v_conv_ln_b, v_w_conv_out, v_w_out, v_norm_ffn, v_w_gate_up, v_w_down):
    given = dict(x=x, norm_mix=norm_mix, w_in=w_in, q_norm=q_norm, k_norm=k_norm, sinks=sinks, conv_w=conv_w, conv_b=conv_b, conv_ln_g=conv_ln_g, conv_ln_b=conv_ln_b, w_conv_out=w_conv_out, w_out=w_out, norm_ffn=norm_ffn, w_gate_up=w_gate_up, w_down=w_down, loss_target=loss_target, m_norm_mix=m_norm_mix, m_w_in=m_w_in, m_q_norm=m_q_norm, m_k_norm=m_k_norm, m_sinks=m_sinks, m_conv_w=m_conv_w, m_conv_b=m_conv_b, m_conv_ln_g=m_conv_ln_g, m_conv_ln_b=m_conv_ln_b, m_w_conv_out=m_w_conv_out, m_w_out=m_w_out, m_norm_ffn=m_norm_ffn, m_w_gate_up=m_w_gate_up, m_w_down=m_w_down, v_norm_mix=v_norm_mix, v_w_in=v_w_in, v_q_norm=v_q_norm, v_k_norm=v_k_norm, v_sinks=v_sinks, v_conv_w=v_conv_w, v_conv_b=v_conv_b, v_conv_ln_g=v_conv_ln_g, v_conv_ln_b=v_conv_ln_b, v_w_conv_out=v_w_conv_out, v_w_out=v_w_out, v_norm_ffn=v_norm_ffn, v_w_gate_up=v_w_gate_up, v_w_down=v_w_down)
    weights = {n: given[n] for n in TWIN_WEIGHTS}
    shared = {n: given[n] for n in SHARED_INPUTS}
    per_example = {n: given[n] for n in ['x']}
    grad_fn = _jax.value_and_grad(_loss, argnums=(0, 1))

    def one_microbatch(ex, loss_target):
        ex = dict(ex)
        diff = ex.pop(TWIN_DIFF_INPUT)
        return grad_fn(weights, diff, {**shared, **ex}, loss_target)

    if N_MICROBATCH == 1:
        loss, (grad_w, grad_x) = one_microbatch(per_example, given["loss_target"])
    else:
        def body(carry, xs):
            loss_sum, grad_sum = carry
            l_k, (gw_k, gx_k) = one_microbatch(xs[0], xs[1])
            with _jax.named_scope("update"):
                return (loss_sum + l_k, _jax.tree.map(_jnp.add, grad_sum, gw_k)), gx_k

        init = (_jnp.zeros((), _jnp.float32), _jax.tree.map(_jnp.zeros_like, weights))
        (loss, grad_w), grad_x = _jax.lax.scan(body, init, (per_example, given["loss_target"]))
    with _jax.named_scope("update"):
        delta_w, new_m, new_v = {}, {}, {}
        for n in TWIN_WEIGHTS:
            delta_w[n], new_m[n], new_v[n] = _adamw(weights[n], grad_w[n], given["m_" + n], given["v_" + n])
    return (loss, grad_x, *[grad_w[n] for n in TWIN_WEIGHTS], *[delta_w[n] for n in TWIN_WEIGHTS],
            *[new_m[n] for n in TWIN_WEIGHTS], *[new_v[n] for n in TWIN_WEIGHTS])
```

```python
import functools
import math

import jax
import jax.numpy as jnp
from jax import lax
from jax.experimental import pallas as pl
from jax.experimental.pallas import tpu as pltpu

F32 = jnp.float32
BF16 = jnp.bfloat16

D_MODEL = 1024
DEPTH = 4
N_HEADS = 16
N_KV_HEADS = 2
HEAD_DIM = 64
ROT_DIM = HEAD_DIM // 4
ROPE_THETA = 500000.0
BLOCK = 128
CONV_CH = D_MODEL // 2
CONV_WIDTH = 31
D_FF = 2816
EPS = 1e-6
Q_W = N_HEADS * HEAD_DIM
KV_W = N_KV_HEADS * HEAD_DIM
QKV_W = Q_W + 2 * KV_W
UUG_W = 2 * CONV_CH
GG_W = 2 * D_MODEL
IN_W = QKV_W + UUG_W + GG_W
N_DEV = 8

ADAM_LR = 0.001
ADAM_B1 = 0.9
ADAM_B2 = 0.999
ADAM_EPS = 1e-08
ADAM_WD = 0.01
ADAM_STEP = 10

LANES = 128
SUBLANES = 8
HALO = 32
VMEM_LIMIT = 56 * 1024 * 1024
NEG = -1e30


def _cp(sem=None):
    return pltpu.CompilerParams(dimension_semantics=sem, vmem_limit_bytes=VMEM_LIMIT)


def _sigmoid(z):
    return 1.0 / (1.0 + jnp.exp(-z))


def _rowgroup_sum(z):
    r, c = z.shape
    return jnp.sum(z.reshape(r // SUBLANES, SUBLANES, c), axis=0)


_DIMS = {"nn": (((1,), (0,)), ((), ())), "nt": (((1,), (1,)), ((), ())), "tn": (((0,), (0,)), ((), ()))}


def _mm(a, b, mode, *, out_dtype, name, tm, tn, tk, resid=None):
    if mode == "nn":
        (m, k), (k2, n) = a.shape, b.shape
        a_spec = pl.BlockSpec((tm, tk), lambda i, j, s: (i, s))
        b_spec = pl.BlockSpec((tk, tn), lambda i, j, s: (s, j))
    elif mode == "nt":
        (m, k), (n, k2) = a.shape, b.shape
        a_spec = pl.BlockSpec((tm, tk), lambda i, j, s: (i, s))
        b_spec = pl.BlockSpec((tn, tk), lambda i, j, s: (j, s))
    else:
        (k, m), (k2, n) = a.shape, b.shape
        a_spec = pl.BlockSpec((tk, tm), lambda i, j, s: (s, i))
        b_spec = pl.BlockSpec((tk, tn), lambda i, j, s: (s, j))
    assert k == k2 and m % tm == 0 and n % tn == 0 and k % tk == 0, (name, a.shape, b.shape, tm, tn, tk)
    nk = k // tk
    dims = _DIMS[mode]
    has_resid = resid is not None

    def body(*refs):
        if has_resid:
            a_ref, b_ref, r_ref, o_ref, acc_ref = refs
        else:
            a_ref, b_ref, o_ref, acc_ref = refs
        s = pl.program_id(2)

        @pl.when(s == 0)
        def _():
            acc_ref[...] = jnp.zeros_like(acc_ref)

        acc_ref[...] += lax.dot_general(a_ref[...].astype(BF16), b_ref[...].astype(BF16), dims,
                                        preferred_element_type=F32)

        @pl.when(s == nk - 1)
        def _():
            acc = acc_ref[...]
            if has_resid:
                acc = acc + r_ref[...]
            o_ref[...] = acc.astype(out_dtype)

    in_specs = [a_spec, b_spec]
    args = [a, b]
    if has_resid:
        in_specs.append(pl.BlockSpec((tm, tn), lambda i, j, s: (i, j)))
        args.append(resid)
    return pl.pallas_call(
        body, name=name, grid=(m // tm, n // tn, nk),
        in_specs=in_specs, out_specs=pl.BlockSpec((tm, tn), lambda i, j, s: (i, j)),
        out_shape=jax.ShapeDtypeStruct((m, n), out_dtype),
        scratch_shapes=[pltpu.VMEM((tm, tn), F32)],
        compiler_params=_cp(("parallel", "parallel", "arbitrary")),
    )(*args)


def _rmsnorm_fwd(x, g, name, tm=512):
    t, d = x.shape

    def body(x_ref, g_ref, h_ref):
        xv = x_ref[...]
        r = lax.rsqrt(jnp.mean(xv * xv, axis=-1, keepdims=True) + EPS)
        h_ref[...] = ((xv * r) * g_ref[...]).astype(BF16)

    return pl.pallas_call(
        body, name=name, grid=(t // tm,),
        in_specs=[pl.BlockSpec((tm, d), lambda i: (i, 0)), pl.BlockSpec((1, d), lambda i: (0, 0))],
        out_specs=pl.BlockSpec((tm, d), lambda i: (i, 0)),
        out_shape=jax.ShapeDtypeStruct((t, d), BF16),
        compiler_params=_cp(("parallel",)),
    )(x, g)


def _rmsnorm_bwd(dh, x, g, resid, name, tm=512):
    t, d = x.shape

    def body(dh_ref, x_ref, g_ref, r_ref, dx_ref, dg_ref):
        @pl.when(pl.program_id(0) == 0)
        def _():
            dg_ref[...] = jnp.zeros_like(dg_ref)

        xv = x_ref[...]
        dhv = dh_ref[...]
        r = lax.rsqrt(jnp.mean(xv * xv, axis=-1, keepdims=True) + EPS)
        y = xv * r
        dg_ref[...] += _rowgroup_sum(dhv * y)
        dy = dhv * g_ref[...]
        dx_ref[...] = r_ref[...] + r * (dy - y * jnp.mean(dy * y, axis=-1, keepdims=True))

    row = pl.BlockSpec((tm, d), lambda i: (i, 0))
    return pl.pallas_call(
        body, name=name, grid=(t // tm,),
        in_specs=[row, row, pl.BlockSpec((1, d), lambda i: (0, 0)), row],
        out_specs=[row, pl.BlockSpec((SUBLANES, d), lambda i: (0, 0))],
        out_shape=[jax.ShapeDtypeStruct((t, d), F32), jax.ShapeDtypeStruct((SUBLANES, d), F32)],
        compiler_params=_cp(("arbitrary",)),
    )(dh, x, g, resid)


def _seg_sum(z, bd):
    hi = z.astype(BF16)
    lo = (z - hi.astype(F32)).astype(BF16)
    return jnp.dot(hi, bd, preferred_element_type=F32) + jnp.dot(lo, bd, preferred_element_type=F32)


def _partner(z, lane64):
    return jnp.where(lane64 < ROT_DIM // 2, pltpu.roll(z, LANES - ROT_DIM // 2, 1), pltpu.roll(z, ROT_DIM // 2, 1))


def _rope_tables(t):
    inv_freq = ROPE_THETA ** (-jnp.arange(0, ROT_DIM, 2, dtype=F32) / ROT_DIM)
    ang = jnp.arange(t, dtype=F32)[:, None] * inv_freq[None, :]
    cos, sin = jnp.cos(ang), jnp.sin(ang)
    c64 = jnp.concatenate([cos, cos, jnp.ones((t, HEAD_DIM - ROT_DIM), F32)], axis=1)
    s64 = jnp.concatenate([-sin, sin, jnp.zeros((t, HEAD_DIM - ROT_DIM), F32)], axis=1)
    return jnp.tile(c64, (1, 2)), jnp.tile(s64, (1, 2))


def _block_diag_ones():
    r = lax.broadcasted_iota(jnp.int32, (LANES, LANES), 0) // HEAD_DIM
    c = lax.broadcasted_iota(jnp.int32, (LANES, LANES), 1) // HEAD_DIM
    return (r == c).astype(BF16)


def _qk_prep_fwd(qkv, gq, gk, ctab, stab, bd, name, tm=512):
    t = qkv.shape[0]
    scale = HEAD_DIM ** -0.5
    n_qg = Q_W // LANES

    def body(q_ref, kv_ref, gq_ref, gk_ref, c_ref, s_ref, bd_ref, qs_ref, kd_ref, vd_ref):
        lane = lax.broadcasted_iota(jnp.int32, (tm, LANES), 1)
        lane64 = lane % HEAD_DIM
        lo_half = lane < HEAD_DIM
        cv, sv, bdv = c_ref[...], s_ref[...], bd_ref[...]

        def norm_rope(xg, g):
            r = lax.rsqrt(_seg_sum(xg * xg, bdv) * (1.0 / HEAD_DIM) + EPS)
            yn = (xg * r) * g
            return yn * cv + _partner(yn, lane64) * sv

        for c in range(n_qg):
            xg = q_ref[:, c * LANES:(c + 1) * LANES]
            qs_ref[:, c * LANES:(c + 1) * LANES] = (norm_rope(xg, gq_ref[...]) * scale).astype(BF16)
        kk = norm_rope(kv_ref[:, 0:LANES], gk_ref[...])
        kr = pltpu.roll(kk, HEAD_DIM, 1)
        kd_ref[:, 0:LANES] = jnp.where(lo_half, kk, kr).astype(BF16)
        kd_ref[:, LANES:2 * LANES] = jnp.where(lo_half, kr, kk).astype(BF16)
        vv = kv_ref[:, LANES:2 * LANES]
        vr = pltpu.roll(vv, HEAD_DIM, 1)
        vd_ref[:, 0:LANES] = jnp.where(lo_half, vv, vr).astype(BF16)
        vd_ref[:, LANES:2 * LANES] = jnp.where(lo_half, vr, vv).astype(BF16)

    vec = pl.BlockSpec((1, LANES), lambda i: (0, 0))
    tab = pl.BlockSpec((tm, LANES), lambda i: (i, 0))
    return pl.pallas_call(
        body, name=name, grid=(t // tm,),
        in_specs=[pl.BlockSpec((tm, Q_W), lambda i: (i, 0)),
                  pl.BlockSpec((tm, 2 * KV_W), lambda i: (i, Q_W // (2 * KV_W))),
                  vec, vec, tab, tab, pl.BlockSpec((LANES, LANES), lambda i: (0, 0))],
        out_specs=[pl.BlockSpec((tm, Q_W), lambda i: (i, 0)),
                   pl.BlockSpec((tm, 2 * LANES), lambda i: (i, 0)),
                   pl.BlockSpec((tm, 2 * LANES), lambda i: (i, 0))],
        out_shape=[jax.ShapeDtypeStruct((t, Q_W), BF16), jax.ShapeDtypeStruct((t, 2 * LANES), BF16),
                   jax.ShapeDtypeStruct((t, 2 * LANES), BF16)],
        compiler_params=_cp(("parallel",)),
    )(qkv, qkv, gq, gk, ctab, stab, bd)


def _qk_prep_bwd(dqs, dkd, dvd, qkv, gq, gk, ctab, stab, bd, name, tm=512):
    t = qkv.shape[0]
    scale = HEAD_DIM ** -0.5
    n_qg = Q_W // LANES

    def body(dqs_ref, dkd_ref, dvd_ref, q_ref, kv_ref, gq_ref, gk_ref, c_ref, s_ref, bd_ref,
             dqkv_ref, dgq_ref, dgk_ref):
        @pl.when(pl.program_id(0) == 0)
        def _():
            dgq_ref[...] = jnp.zeros_like(dgq_ref)
            dgk_ref[...] = jnp.zeros_like(dgk_ref)

        lane = lax.broadcasted_iota(jnp.int32, (tm, LANES), 1)
        lane64 = lane % HEAD_DIM
        lo_half = lane < HEAD_DIM
        cv, sv, bdv = c_ref[...], s_ref[...], bd_ref[...]

        def bwd(xg, g, dout):
            r = lax.rsqrt(_seg_sum(xg * xg, bdv) * (1.0 / HEAD_DIM) + EPS)
            y = xg * r
            dyn = dout * cv + jnp.where(lane64 < ROT_DIM, _partner(dout * sv, lane64), 0.0)
            dy = dyn * g
            dx = r * (dy - y * (_seg_sum(dy * y, bdv) * (1.0 / HEAD_DIM)))
            return dx, _rowgroup_sum(dyn * y)

        dgq = jnp.zeros((SUBLANES, LANES), F32)
        for c in range(n_qg):
            sl = slice(c * LANES, (c + 1) * LANES)
            dx, dg = bwd(q_ref[:, sl], gq_ref[...], dqs_ref[:, sl] * scale)
            dqkv_ref[:, sl] = dx.astype(BF16)
            dgq = dgq + dg
        dgq_ref[...] += dgq
        dk = jnp.where(lo_half, dkd_ref[:, 0:LANES], dkd_ref[:, LANES:2 * LANES])
        dx, dg = bwd(kv_ref[:, 0:LANES], gk_ref[...], dk)
        dqkv_ref[:, Q_W:Q_W + LANES] = dx.astype(BF16)
        dgk_ref[...] += dg
        dv = jnp.where(lo_half, dvd_ref[:, 0:LANES], dvd_ref[:, LANES:2 * LANES])
        dqkv_ref[:, Q_W + LANES:Q_W + 2 * LANES] = dv.astype(BF16)

    vec = pl.BlockSpec((1, LANES), lambda i: (0, 0))
    tab = pl.BlockSpec((tm, LANES), lambda i: (i, 0))
    wide = pl.BlockSpec((tm, 2 * LANES), lambda i: (i, 0))
    acc = pl.BlockSpec((SUBLANES, LANES), lambda i: (0, 0))
    return pl.pallas_call(
        body, name=name, grid=(t // tm,),
        in_specs=[pl.BlockSpec((tm, Q_W), lambda i: (i, 0)), wide, wide,
                  pl.BlockSpec((tm, Q_W), lambda i: (i, 0)),
                  pl.BlockSpec((tm, 2 * KV_W), lambda i: (i, Q_W // (2 * KV_W))),
                  vec, vec, tab, tab, pl.BlockSpec((LANES, LANES), lambda i: (0, 0))],
        out_specs=[pl.BlockSpec((tm, QKV_W), lambda i: (i, 0)), acc, acc],
        out_shape=[jax.ShapeDtypeStruct((t, QKV_W), BF16), jax.ShapeDtypeStruct((SUBLANES, LANES), F32),
                   jax.ShapeDtypeStruct((SUBLANES, LANES), F32)],
        compiler_params=_cp(("arbitrary",)),
    )(dqs, dkd, dvd, qkv, qkv, gq, gk, ctab, stab, bd)


def _attn_masks(i):
    row = lax.broadcasted_iota(jnp.int32, (BLOCK, BLOCK), 0)
    col = lax.broadcasted_iota(jnp.int32, (BLOCK, BLOCK), 1)
    return col <= row, jnp.logical_and(col > row, i > 0), col < HEAD_DIM


def _softmax_sink(qh, kc, kp, mask_c, mask_p, sink):
    s_c = jnp.where(mask_c, lax.dot_general(qh, kc, _DIMS["nt"], preferred_element_type=F32), NEG)
    s_p = jnp.where(mask_p, lax.dot_general(qh, kp, _DIMS["nt"], preferred_element_type=F32), NEG)
    m = jnp.maximum(jnp.maximum(jnp.max(s_c, axis=-1, keepdims=True), jnp.max(s_p, axis=-1, keepdims=True)), sink)
    e_c = jnp.exp(s_c - m)
    e_p = jnp.exp(s_p - m)
    e_s = jnp.exp(sink - m)
    inv = 1.0 / (jnp.sum(e_c, axis=-1, keepdims=True) + jnp.sum(e_p, axis=-1, keepdims=True) + e_s)
    return e_c * inv, e_p * inv, e_s * inv


def _attn_fwd(qs, kd, vd, sinks, name):
    t = qs.shape[0]
    nb = t // BLOCK
    pairs = N_HEADS // 2

    def body(sink_ref, q_ref, kc_ref, kp_ref, vc_ref, vp_ref, o_ref):
        i = pl.program_id(0)
        mask_c, mask_p, lo_half = _attn_masks(i)
        for p in range(pairs):
            j = p // (pairs // N_KV_HEADS)
            ks = slice(j * LANES, (j + 1) * LANES)
            qp = q_ref[:, p * LANES:(p + 1) * LANES]
            kc, kp, vc, vp = kc_ref[:, ks], kp_ref[:, ks], vc_ref[:, ks], vp_ref[:, ks]
            outs = []
            for half in range(2):
                qh = jnp.where(lo_half if half == 0 else jnp.logical_not(lo_half), qp, jnp.zeros_like(qp))
                p_c, p_p, _ = _softmax_sink(qh, kc, kp, mask_c, mask_p, sink_ref[2 * p + half])
                outs.append(jnp.dot(p_c.astype(BF16), vc, preferred_element_type=F32)
                            + jnp.dot(p_p.astype(BF16), vp, preferred_element_type=F32))
            o_ref[:, p * LANES:(p + 1) * LANES] = jnp.where(lo_half, outs[0], outs[1])

    cur = lambda i: (i, 0)
    prev = lambda i: (jnp.maximum(i - 1, 0), 0)
    kvs = (BLOCK, 2 * LANES)
    return pl.pallas_call(
        body, name=name, grid=(nb,),
        in_specs=[pl.BlockSpec(memory_space=pltpu.SMEM),
                  pl.BlockSpec((BLOCK, Q_W), cur),
                  pl.BlockSpec(kvs, cur), pl.BlockSpec(kvs, prev), pl.BlockSpec(kvs, cur), pl.BlockSpec(kvs, prev)],
        out_specs=pl.BlockSpec((BLOCK, Q_W), cur),
        out_shape=jax.ShapeDtypeStruct((t, Q_W), F32),
        compiler_params=_cp(("parallel",)),
    )(sinks, qs, kd, kd, vd, vd)


def _attn_bwd(qs, kd, vd, sinks, do, name):
    t = qs.shape[0]
    nb = t // BLOCK
    pairs = N_HEADS // 2
    per_kv = pairs // N_KV_HEADS

    def body(sink_ref, q_ref, do_ref, kc_ref, kp_ref, vc_ref, vp_ref,
             dq_ref, dk_ref, dv_ref, dsink_ref, carry_k, carry_v, dsink_acc):
        i = pl.program_id(0)

        @pl.when(i == 0)
        def _():
            carry_k[...] = jnp.zeros_like(carry_k)
            carry_v[...] = jnp.zeros_like(carry_v)
            dsink_acc[...] = jnp.zeros_like(dsink_acc)

        @pl.when(i < nb)
        def _():
            mask_c, mask_p, lo_half = _attn_masks(i)
            hi_half = jnp.logical_not(lo_half)
            srow = lax.broadcasted_iota(jnp.int32, (SUBLANES, LANES), 0)
            scol = lax.broadcasted_iota(jnp.int32, (SUBLANES, LANES), 1)
            dsink = jnp.zeros((SUBLANES, LANES), F32)
            for j in range(N_KV_HEADS):
                ks = slice(j * LANES, (j + 1) * LANES)
                kc, kp, vc, vp = kc_ref[:, ks], kp_ref[:, ks], vc_ref[:, ks], vp_ref[:, ks]
                dk_c = jnp.zeros((BLOCK, LANES), F32)
                dk_p = jnp.zeros((BLOCK, LANES), F32)
                dv_c = jnp.zeros((BLOCK, LANES), F32)
                dv_p = jnp.zeros((BLOCK, LANES), F32)
                for pp in range(per_kv):
                    p = j * per_kv + pp
                    qp = q_ref[:, p * LANES:(p + 1) * LANES]
                    dop = do_ref[:, p * LANES:(p + 1) * LANES]
                    dqs = []
                    for half in range(2):
                        hm = lo_half if half == 0 else hi_half
                        h = 2 * p + half
                        qh = jnp.where(hm, qp, jnp.zeros_like(qp))
                        doh = jnp.where(hm, dop, 0.0).astype(BF16)
                        p_c, p_p, p_s = _softmax_sink(qh, kc, kp, mask_c, mask_p, sink_ref[h])
                        dp_c = lax.dot_general(doh, vc, _DIMS["nt"], preferred_element_type=F32)
                        dp_p = lax.dot_general(doh, vp, _DIMS["nt"], preferred_element_type=F32)
                        delta = (jnp.sum(p_c * dp_c, axis=-1, keepdims=True)
                                 + jnp.sum(p_p * dp_p, axis=-1, keepdims=True))
                        ds_c = (p_c * (dp_c - delta)).astype(BF16)
                        ds_p = (p_p * (dp_p - delta)).astype(BF16)
                        dsv = -jnp.sum(p_s * delta, axis=0, keepdims=True)
                        dsink = dsink + jnp.where(jnp.logical_and(srow == 0, scol == h), dsv, 0.0)
                        dqs.append(jnp.dot(ds_c, kc, preferred_element_type=F32)
                                   + jnp.dot(ds_p, kp, preferred_element_type=F32))
                        dk_c = dk_c + lax.dot_general(ds_c, qh, _DIMS["tn"], preferred_element_type=F32)
                        dk_p = dk_p + lax.dot_general(ds_p, qh, _DIMS["tn"], preferred_element_type=F32)
                        dv_c = dv_c + lax.dot_general(p_c.astype(BF16), doh, _DIMS["tn"], preferred_element_type=F32)
                        dv_p = dv_p + lax.dot_general(p_p.astype(BF16), doh, _DIMS["tn"], preferred_element_type=F32)
                    dq_ref[:, p * LANES:(p + 1) * LANES] = jnp.where(lo_half, dqs[0], dqs[1])
                dk_ref[:, ks] = carry_k[:, ks] + dk_p + pltpu.roll(dk_p, HEAD_DIM, 1)
                dv_ref[:, ks] = carry_v[:, ks] + dv_p + pltpu.roll(dv_p, HEAD_DIM, 1)
                carry_k[:, ks] = dk_c + pltpu.roll(dk_c, HEAD_DIM, 1)
                carry_v[:, ks] = dv_c + pltpu.roll(dv_c, HEAD_DIM, 1)
            dsink_acc[...] += dsink

        @pl.when(i == nb)
        def _():
            dk_ref[...] = carry_k[...]
            dv_ref[...] = carry_v[...]
            dsink_ref[...] = dsink_acc[...]

    last = nb - 1
    cur = lambda i: (jnp.minimum(i, last), 0)
    prev = lambda i: (jnp.clip(i - 1, 0, last), 0)
    kvs = (BLOCK, 2 * LANES)
    return pl.pallas_call(
        body, name=name, grid=(nb + 1,),
        in_specs=[pl.BlockSpec(memory_space=pltpu.SMEM),
                  pl.BlockSpec((BLOCK, Q_W), cur), pl.BlockSpec((BLOCK, Q_W), cur),
                  pl.BlockSpec(kvs, cur), pl.BlockSpec(kvs, prev), pl.BlockSpec(kvs, cur), pl.BlockSpec(kvs, prev)],
        out_specs=[pl.BlockSpec((BLOCK, Q_W), cur), pl.BlockSpec(kvs, prev), pl.BlockSpec(kvs, prev),
                   pl.BlockSpec((SUBLANES, LANES), lambda i: (0, 0))],
        out_shape=[jax.ShapeDtypeStruct((t, Q_W), F32), jax.ShapeDtypeStruct((t, 2 * LANES), F32),
                   jax.ShapeDtypeStruct((t, 2 * LANES), F32), jax.ShapeDtypeStruct((SUBLANES, LANES), F32)],
        scratch_shapes=[pltpu.VMEM(kvs, F32), pltpu.VMEM(kvs, F32), pltpu.VMEM((SUBLANES, LANES), F32)],
        compiler_params=_cp(("arbitrary",)),
    )(sinks, qs, do, kd, kd, vd, vd)


CONV_CHUNK = 64


def _conv_fwd(uug, w32, cb, lg, lb, name, tm=512):
    t = uug.shape[0]
    hb = tm // HALO

    def body(m_ref, h_ref, w_ref, cb_ref, lg_ref, lb_ref, y0_ref, y2_ref, a_ext):
        i = pl.program_id(0)
        a_ext[HALO:, :] = m_ref[:, 0:CONV_CH] * _sigmoid(m_ref[:, CONV_CH:])
        ah = h_ref[:, 0:CONV_CH] * _sigmoid(h_ref[:, CONV_CH:])
        a_ext[0:HALO, :] = jnp.where(i > 0, ah, 0.0)
        off = HALO - (CONV_WIDTH - 1)
        for c in range(tm // CONV_CHUNK):
            r0 = c * CONV_CHUNK
            acc = jnp.zeros((CONV_CHUNK, CONV_CH), F32)
            for k in range(CONV_WIDTH):
                acc = acc + w_ref[k:k + 1, :] * a_ext[r0 + off + k:r0 + off + k + CONV_CHUNK, :]
            y0 = acc + cb_ref[...]
            y0_ref[r0:r0 + CONV_CHUNK, :] = y0
            mu = jnp.mean(y0, axis=-1, keepdims=True)
            dlt = y0 - mu
            rstd = lax.rsqrt(jnp.mean(dlt * dlt, axis=-1, keepdims=True) + EPS)
            y1 = (dlt * rstd) * lg_ref[...] + lb_ref[...]
            y2_ref[r0:r0 + CONV_CHUNK, :] = (y1 * _sigmoid(y1)).astype(BF16)

    vec = pl.BlockSpec((1, CONV_CH), lambda i: (0, 0))
    return pl.pallas_call(
        body, name=name, grid=(t // tm,),
        in_specs=[pl.BlockSpec((tm, UUG_W), lambda i: (i, 0)),
                  pl.BlockSpec((HALO, UUG_W), lambda i: (jnp.maximum(i * hb - 1, 0), 0)),
                  pl.BlockSpec((HALO, CONV_CH), lambda i: (0, 0)), vec, vec, vec],
        out_specs=[pl.BlockSpec((tm, CONV_CH), lambda i: (i, 0)), pl.BlockSpec((tm, CONV_CH), lambda i: (i, 0))],
        out_shape=[jax.ShapeDtypeStruct((t, CONV_CH), F32), jax.ShapeDtypeStruct((t, CONV_CH), BF16)],
        scratch_shapes=[pltpu.VMEM((tm + HALO, CONV_CH), F32)],
        compiler_params=_cp(("parallel",)),
    )(uug, uug, w32, cb, lg, lb)


def _conv_bwd_ln(dy2, y0, lg, lb, name, tm=512):
    t = y0.shape[0]

    def body(dy2_ref, y0_ref, lg_ref, lb_ref, dy0_ref, dlg_ref, dlb_ref, dcb_ref):
        @pl.when(pl.program_id(0) == 0)
        def _():
            dlg_ref[...] = jnp.zeros_like(dlg_ref)
            dlb_ref[...] = jnp.zeros_like(dlb_ref)
            dcb_ref[...] = jnp.zeros_like(dcb_ref)

        y0 = y0_ref[...]
        mu = jnp.mean(y0, axis=-1, keepdims=True)
        dlt = y0 - mu
        rstd = lax.rsqrt(jnp.mean(dlt * dlt, axis=-1, keepdims=True) + EPS)
        yh = dlt * rstd
        y1 = yh * lg_ref[...] + lb_ref[...]
        sg = _sigmoid(y1)
        dy1 = dy2_ref[...] * (sg * (1.0 + y1 * (1.0 - sg)))
        dlg_ref[...] += _rowgroup_sum(dy1 * yh)
        dlb_ref[...] += _rowgroup_sum(dy1)
        dyh = dy1 * lg_ref[...]
        dy0 = rstd * (dyh - jnp.mean(dyh, axis=-1, keepdims=True)
                      - yh * jnp.mean(dyh * yh, axis=-1, keepdims=True))
        dcb_ref[...] += _rowgroup_sum(dy0)
        dy0_ref[...] = dy0

    row = pl.BlockSpec((tm, CONV_CH), lambda i: (i, 0))
    vec = pl.BlockSpec((1, CONV_CH), lambda i: (0, 0))
    acc = pl.BlockSpec((SUBLANES, CONV_CH), lambda i: (0, 0))
    accs = jax.ShapeDtypeStruct((SUBLANES, CONV_CH), F32)
    return pl.pallas_call(
        body, name=name, grid=(t // tm,),
        in_specs=[row, row, vec, vec], out_specs=[row, acc, acc, acc],
        out_shape=[jax.ShapeDtypeStruct((t, CONV_CH), F32), accs, accs, accs],
        compiler_params=_cp(("arbitrary",)),
    )(dy2, y0, lg, lb)


def _conv_bwd_taps(dy0, uug, w32, name, tm=512):
    t = uug.shape[0]
    hb = tm // HALO
    n_halo_blocks = t // HALO
    nt = t // tm

    def body(dm_ref, dn_ref, m_ref, h_ref, w_ref, duug_ref, dw_ref, a_ext, d_ext):
        i = pl.program_id(0)

        @pl.when(i == 0)
        def _():
            dw_ref[...] = jnp.zeros_like(dw_ref)

        u = m_ref[:, 0:CONV_CH]
        sg = _sigmoid(m_ref[:, CONV_CH:])
        a_ext[HALO:, :] = u * sg
        ah = h_ref[:, 0:CONV_CH] * _sigmoid(h_ref[:, CONV_CH:])
        a_ext[0:HALO, :] = jnp.where(i > 0, ah, 0.0)
        d_ext[0:tm, :] = dm_ref[...]
        d_ext[tm:, :] = jnp.where(i < nt - 1, dn_ref[...], 0.0)
        off = HALO - (CONV_WIDTH - 1)
        for c in range(tm // CONV_CHUNK):
            r0 = c * CONV_CHUNK
            da = jnp.zeros((CONV_CHUNK, CONV_CH), F32)
            for k in range(CONV_WIDTH):
                sh = CONV_WIDTH - 1 - k
                da = da + w_ref[k:k + 1, :] * d_ext[r0 + sh:r0 + sh + CONV_CHUNK, :]
            uc = u[r0:r0 + CONV_CHUNK, :]
            sc = sg[r0:r0 + CONV_CHUNK, :]
            duug_ref[r0:r0 + CONV_CHUNK, 0:CONV_CH] = (da * sc).astype(BF16)
            duug_ref[r0:r0 + CONV_CHUNK, CONV_CH:] = (da * uc * sc * (1.0 - sc)).astype(BF16)
            dch = d_ext[r0:r0 + CONV_CHUNK, :]
            for k in range(CONV_WIDTH):
                prod = dch * a_ext[r0 + off + k:r0 + off + k + CONV_CHUNK, :]
                dw_ref[k * SUBLANES:(k + 1) * SUBLANES, :] += _rowgroup_sum(prod)

    return pl.pallas_call(
        body, name=name, grid=(nt,),
        in_specs=[pl.BlockSpec((tm, CONV_CH), lambda i: (i, 0)),
                  pl.BlockSpec((HALO, CONV_CH), lambda i: (jnp.minimum((i + 1) * hb, n_halo_blocks - 1), 0)),
                  pl.BlockSpec((tm, UUG_W), lambda i: (i, 0)),
                  pl.BlockSpec((HALO, UUG_W), lambda i: (jnp.maximum(i * hb - 1, 0), 0)),
                  pl.BlockSpec((HALO, CONV_CH), lambda i: (0, 0))],
        out_specs=[pl.BlockSpec((tm, UUG_W), lambda i: (i, 0)),
                   pl.BlockSpec((CONV_WIDTH * SUBLANES, CONV_CH), lambda i: (0, 0))],
        out_shape=[jax.ShapeDtypeStruct((t, UUG_W), BF16),
                   jax.ShapeDtypeStruct((CONV_WIDTH * SUBLANES, CONV_CH), F32)],
        scratch_shapes=[pltpu.VMEM((tm + HALO, CONV_CH), F32), pltpu.VMEM((tm + HALO, CONV_CH), F32)],
        compiler_params=_cp(("arbitrary",)),
    )(dy0, dy0, uug, uug, w32)


def _merge_fwd(a_out, c_out, gg, name, tm=512):
    t, d = a_out.shape

    def body(a_ref, c_ref, g_ref, o_ref):
        o_ref[...] = (_sigmoid(g_ref[:, 0:d]) * a_ref[...] + _sigmoid(g_ref[:, d:]) * c_ref[...]).astype(BF16)

    row = pl.BlockSpec((tm, d), lambda i: (i, 0))
    return pl.pallas_call(
        body, name=name, grid=(t // tm,),
        in_specs=[row, row, pl.BlockSpec((tm, 2 * d), lambda i: (i, 0))], out_specs=row,
        out_shape=jax.ShapeDtypeStruct((t, d), BF16), compiler_params=_cp(("parallel",)),
    )(a_out, c_out, gg)


def _merge_bwd(dm, a_out, c_out, gg, name, tm=512):
    t, d = a_out.shape

    def body(dm_ref, a_ref, c_ref, g_ref, da_ref, dc_ref, dg_ref):
        dmv = dm_ref[...]
        sa = _sigmoid(g_ref[:, 0:d])
        sb = _sigmoid(g_ref[:, d:])
        da_ref[...] = dmv * sa
        dc_ref[...] = (dmv * sb).astype(BF16)
        dg_ref[:, 0:d] = (dmv * a_ref[...] * sa * (1.0 - sa)).astype(BF16)
        dg_ref[:, d:] = (dmv * c_ref[...] * sb * (1.0 - sb)).astype(BF16)

    row = pl.BlockSpec((tm, d), lambda i: (i, 0))
    wide = pl.BlockSpec((tm, 2 * d), lambda i: (i, 0))
    return pl.pallas_call(
        body, name=name, grid=(t // tm,),
        in_specs=[row, row, row, wide], out_specs=[row, row, wide],
        out_shape=[jax.ShapeDtypeStruct((t, d), F32), jax.ShapeDtypeStruct((t, d), BF16),
                   jax.ShapeDtypeStruct((t, 2 * d), BF16)],
        compiler_params=_cp(("parallel",)),
    )(dm, a_out, c_out, gg)


FF_TN = 1408


def _ffn_up_fwd(h2, wgu, name, tm=512):
    t, d = h2.shape
    nj = D_FF // FF_TN

    def body(h_ref, wg_ref, wu_ref, o_ref):
        hv = h_ref[...]
        g = jnp.dot(hv, wg_ref[...], preferred_element_type=F32)
        u = jnp.dot(hv, wu_ref[...], preferred_element_type=F32)
        o_ref[...] = ((g * _sigmoid(g)) * u).astype(BF16)

    return pl.pallas_call(
        body, name=name, grid=(nj, t // tm),
        in_specs=[pl.BlockSpec((tm, d), lambda j, i: (i, 0)),
                  pl.BlockSpec((d, FF_TN), lambda j, i: (0, j)),
                  pl.BlockSpec((d, FF_TN), lambda j, i: (0, j + nj))],
        out_specs=pl.BlockSpec((tm, FF_TN), lambda j, i: (i, j)),
        out_shape=jax.ShapeDtypeStruct((t, D_FF), BF16),
        compiler_params=_cp(("parallel", "parallel")),
    )(h2, wgu, wgu)


def _ffn_bwd_mid(h2, dx2, wgu, wd, name, tm=256):
    t, d = h2.shape
    nj = D_FF // FF_TN

    def body(h_ref, dx_ref, wg_ref, wu_ref, wd_ref, dg_ref, du_ref):
        hv = h_ref[...]
        g = jnp.dot(hv, wg_ref[...], preferred_element_type=F32)
        u = jnp.dot(hv, wu_ref[...], preferred_element_type=F32)
        dact = lax.dot_general(dx_ref[...].astype(BF16), wd_ref[...], _DIMS["nt"], preferred_element_type=F32)
        sg = _sigmoid(g)
        silu = g * sg
        dg_ref[...] = (dact * u * (sg * (1.0 + g * (1.0 - sg)))).astype(BF16)
        du_ref[...] = (dact * silu).astype(BF16)

    return pl.pallas_call(
        body, name=name, grid=(nj, t // tm),
        in_specs=[pl.BlockSpec((tm, d), lambda j, i: (i, 0)), pl.BlockSpec((tm, d), lambda j, i: (i, 0)),
                  pl.BlockSpec((d, FF_TN), lambda j, i: (0, j)),
                  pl.BlockSpec((d, FF_TN), lambda j, i: (0, j + nj)),
                  pl.BlockSpec((FF_TN, d), lambda j, i: (j, 0))],
        out_specs=[pl.BlockSpec((tm, FF_TN), lambda j, i: (i, j)), pl.BlockSpec((tm, FF_TN), lambda j, i: (i, j))],
        out_shape=[jax.ShapeDtypeStruct((t, D_FF), BF16), jax.ShapeDtypeStruct((t, D_FF), BF16)],
        compiler_params=_cp(("parallel", "parallel")),
    )(h2, dx2, wgu, wgu, wd)


def _loss_head(y, target, name, tm=512):
    t, d = y.shape

    def body(y_ref, t_ref, dy_ref, loss_ref):
        @pl.when(pl.program_id(0) == 0)
        def _():
            loss_ref[...] = jnp.zeros_like(loss_ref)

        e = y_ref[...] - t_ref[...]
        dy_ref[...] = e * (1.0 / d)
        s = _rowgroup_sum(e * e)
        acc = s[:, 0:LANES]
        for c in range(1, d // LANES):
            acc = acc + s[:, c * LANES:(c + 1) * LANES]
        loss_ref[...] += acc * (0.5 / d)

    row = pl.BlockSpec((tm, d), lambda i: (i, 0))
    return pl.pallas_call(
        body, name=name, grid=(t // tm,),
        in_specs=[row, row], out_specs=[row, pl.BlockSpec((SUBLANES, LANES), lambda i: (0, 0))],
        out_shape=[jax.ShapeDtypeStruct((t, d), F32), jax.ShapeDtypeStruct((SUBLANES, LANES), F32)],
        compiler_params=_cp(("arbitrary",)),
    )(y, target)


def _exchange(arrays, scatter, name):
    n = len(arrays)

    def body(*refs):
        ins, outs = refs[:n], refs[n:2 * n]
        send_sems, recv_sems, local_sems = refs[2 * n:]
        x, y, c = lax.axis_index("x"), lax.axis_index("y"), lax.axis_index("c")
        me = 4 * x + 2 * y + c

        def peer(k):
            px, py, pc = x ^ ((k >> 2) & 1), y ^ ((k >> 1) & 1), c ^ (k & 1)
            return (px, py, pc), 4 * px + 2 * py + pc

        def src(a, dst_id):
            return ins[a].at[dst_id] if scatter else ins[a]

        locals_ = [pltpu.make_async_copy(src(a, me), outs[a].at[me], local_sems.at[a]) for a in range(n)]
        for cp in locals_:
            cp.start()
        sends = []
        for k in range(1, N_DEV):
            dev, pid = peer(k)
            for a in range(n):
                sends.append(pltpu.make_async_remote_copy(
                    src_ref=src(a, pid), dst_ref=outs[a].at[me],
                    send_sem=send_sems.at[a, k], recv_sem=recv_sems.at[a, k],
                    device_id=dev, device_id_type=pl.DeviceIdType.MESH))
        for cp in sends:
            cp.start()
        for k in range(1, N_DEV):
            dev, pid = peer(k)
            for a in range(n):
                pltpu.make_async_remote_copy(
                    src_ref=src(a, pid), dst_ref=outs[a].at[pid],
                    send_sem=send_sems.at[a, k], recv_sem=recv_sems.at[a, k],
                    device_id=dev, device_id_type=pl.DeviceIdType.MESH).wait_recv()
        for cp in sends:
            cp.wait_send()
        for cp in locals_:
            cp.wait()

    def out_shape(a):
        return jax.ShapeDtypeStruct(a.shape if scatter else (N_DEV,) + a.shape, a.dtype)

    anyspec = pl.BlockSpec(memory_space=pl.ANY)
    return pl.pallas_call(
        body, name=name,
        in_specs=[anyspec] * n, out_specs=[anyspec] * n,
        out_shape=[out_shape(a) for a in arrays],
        scratch_shapes=[pltpu.SemaphoreType.DMA((n, N_DEV)), pltpu.SemaphoreType.DMA((n, N_DEV)),
                        pltpu.SemaphoreType.DMA((n,))],
    )(*arrays)


def _adamw(parts, w, m, v, name, tr):
    r, c = w.shape
    assert r % tr == 0, (name, r, tr)
    c1 = 1.0 - ADAM_B1 ** ADAM_STEP
    c2 = 1.0 - ADAM_B2 ** ADAM_STEP

    def body(p_ref, w_ref, m_ref, v_ref, g_ref, d_ref, nm_ref, nv_ref):
        g = p_ref[0].astype(F32)
        for s in range(1, N_DEV):
            g = g + p_ref[s].astype(F32)
        nm = ADAM_B1 * m_ref[...] + (1.0 - ADAM_B1) * g
        nv = ADAM_B2 * v_ref[...] + (1.0 - ADAM_B2) * (g * g)
        g_ref[...] = g
        nm_ref[...] = nm
        nv_ref[...] = nv
        d_ref[...] = -ADAM_LR * ((nm / c1) / (jnp.sqrt(nv / c2) + ADAM_EPS) + ADAM_WD * w_ref[...])

    row = pl.BlockSpec((tr, c), lambda i: (i, 0))
    o = jax.ShapeDtypeStruct((r, c), F32)
    return pl.pallas_call(
        body, name=name, grid=(r // tr,),
        in_specs=[pl.BlockSpec((N_DEV, tr, c), lambda i: (0, i, 0)), row, row, row],
        out_specs=[row, row, row, row], out_shape=[o, o, o, o],
        compiler_params=_cp(("parallel",)),
    )(parts, w, m, v)


def _layer_fwd(x, wl, sl, tabs, l):
    ctab, stab, bd = tabs
    n = f"l{l}_"
    h = _rmsnorm_fwd(x, sl["norm_mix"], n + "norm_mix")
    qkv = _mm(h, wl["w_qkv"], "nn", out_dtype=F32, name=n + "proj_qkv", tm=1024, tn=QKV_W, tk=D_MODEL)
    uug = _mm(h, wl["w_uug"], "nn", out_dtype=F32, name=n + "proj_uug", tm=1024, tn=UUG_W, tk=D_MODEL)
    gg = _mm(h, wl["w_gg"], "nn", out_dtype=F32, name=n + "proj_gg", tm=1024, tn=GG_W, tk=D_MODEL)
    qs, kd, vd = _qk_prep_fwd(qkv, sl["gq"], sl["gk"], ctab, stab, bd, n + "qk_prep")
    a_out = _attn_fwd(qs, kd, vd, sl["sinks"], n + "attn")
    y0, y2 = _conv_fwd(uug, sl["conv_w32"], sl["conv_b"], sl["ln_g"], sl["ln_b"], n + "conv")
    c_out = _mm(y2, wl["w_conv_out"], "nn", out_dtype=F32, name=n + "conv_out", tm=1024, tn=D_MODEL, tk=CONV_CH)
    merged = _merge_fwd(a_out, c_out, gg, n + "merge")
    x1 = _mm(merged, wl["w_out"], "nn", out_dtype=F32, name=n + "out_proj", tm=1024, tn=D_MODEL, tk=D_MODEL,
             resid=x)
    h2 = _rmsnorm_fwd(x1, sl["norm_ffn"], n + "norm_ffn")
    act = _ffn_up_fwd(h2, wl["w_gate_up"], n + "ffn_up")
    x2 = _mm(act, wl["w_down"], "nn", out_dtype=F32, name=n + "ffn_down", tm=1024, tn=D_MODEL, tk=FF_TN, resid=x1)
    saved = dict(x=x, h=h, qkv=qkv, uug=uug, gg=gg, qs=qs, kd=kd, vd=vd, a_out=a_out, y0=y0, y2=y2,
                 c_out=c_out, merged=merged, x1=x1, h2=h2, act=act)
    return x2, saved


def _layer_bwd(dx2, sv, wl, sl, tabs, l):
    ctab, stab, bd = tabs
    n = f"l{l}_b_"
    tk = 1024
    gw, gs = {}, {}
    gw["w_down"] = _mm(sv["act"], dx2, "tn", out_dtype=F32, name=n + "dw_down", tm=FF_TN, tn=D_MODEL, tk=tk)
    dg, du = _ffn_bwd_mid(sv["h2"], dx2, wl["w_gate_up"], wl["w_down"], n + "ffn_mid")
    dh2 = _mm(dg, wl["w_gate"], "nt", out_dtype=F32, name=n + "dh2_g", tm=1024, tn=D_MODEL, tk=FF_TN)
    dh2 = _mm(du, wl["w_up"], "nt", out_dtype=F32, name=n + "dh2_u", tm=1024, tn=D_MODEL, tk=FF_TN, resid=dh2)
    gw["w_gate"] = _mm(sv["h2"], dg, "tn", out_dtype=F32, name=n + "dw_gate", tm=D_MODEL, tn=FF_TN, tk=tk)
    gw["w_up"] = _mm(sv["h2"], du, "tn", out_dtype=F32, name=n + "dw_up", tm=D_MODEL, tn=FF_TN, tk=tk)
    dx1, gs["norm_ffn"] = _rmsnorm_bwd(dh2, sv["x1"], sl["norm_ffn"], dx2, n + "norm_ffn")
    dmerged = _mm(dx1, wl["w_out"], "nt", out_dtype=F32, name=n + "dmerged", tm=1024, tn=D_MODEL, tk=D_MODEL)
    gw["w_out"] = _mm(sv["merged"], dx1, "tn", out_dtype=F32, name=n + "dw_out", tm=D_MODEL, tn=D_MODEL, tk=tk)
    da_out, dc_out, dgg = _merge_bwd(dmerged, sv["a_out"], sv["c_out"], sv["gg"], n + "merge")
    dy2 = _mm(dc_out, wl["w_conv_out"], "nt", out_dtype=F32, name=n + "dy2", tm=1024, tn=CONV_CH, tk=D_MODEL)
    gw["w_conv_out"] = _mm(sv["y2"], dc_out, "tn", out_dtype=F32, name=n + "dw_conv_out", tm=CONV_CH, tn=D_MODEL,
                           tk=tk)
    dy0, gs["ln_g"], gs["ln_b"], gs["conv_b"] = _conv_bwd_ln(dy2, sv["y0"], sl["ln_g"], sl["ln_b"], n + "conv_ln")
    duug, gs["conv_w"] = _conv_bwd_taps(dy0, sv["uug"], sl["conv_w32"], n + "conv_taps")
    dqs, dkd, dvd, gs["sinks"] = _attn_bwd(sv["qs"], sv["kd"], sv["vd"], sl["sinks"], da_out, n + "attn")
    dqkv, gs["gq"], gs["gk"] = _qk_prep_bwd(dqs, dkd, dvd, sv["qkv"], sl["gq"], sl["gk"], ctab, stab, bd,
                                            n + "qk_prep")
    dh = _mm(dqkv, wl["w_qkv"], "nt", out_dtype=F32, name=n + "dh_qkv", tm=1024, tn=D_MODEL, tk=QKV_W)
    dh = _mm(duug, wl["w_uug"], "nt", out_dtype=F32, name=n + "dh_uug", tm=1024, tn=D_MODEL, tk=UUG_W, resid=dh)
    dh = _mm(dgg, wl["w_gg"], "nt", out_dtype=F32, name=n + "dh_gg", tm=1024, tn=D_MODEL, tk=GG_W, resid=dh)
    gw["w_qkv"] = _mm(sv["h"], dqkv, "tn", out_dtype=F32, name=n + "dw_qkv", tm=D_MODEL, tn=QKV_W, tk=tk)
    gw["w_uug"] = _mm(sv["h"], duug, "tn", out_dtype=F32, name=n + "dw_uug", tm=D_MODEL, tn=UUG_W, tk=tk)
    gw["w_gg"] = _mm(sv["h"], dgg, "tn", out_dtype=F32, name=n + "dw_gg", tm=D_MODEL, tn=GG_W, tk=tk)
    dx, gs["norm_mix"] = _rmsnorm_bwd(dh, sv["x"], sl["norm_mix"], dx1, n + "norm_mix")
    return dx, gw, gs


def _cols_to_full(g):
    n, l, r, c = g.shape
    return jnp.transpose(g, (1, 2, 0, 3)).reshape(l, r, n * c)


def _rows_to_full(g):
    n, l, r, c = g.shape
    return jnp.transpose(g, (1, 0, 2, 3)).reshape(l, n * r, c)


def _full_to_cols(w):
    l, r, c = w.shape
    return jnp.transpose(w.reshape(l, r, N_DEV, c // N_DEV), (2, 0, 1, 3))


def _full_to_rows(w):
    l, r, c = w.shape
    return jnp.transpose(w.reshape(l, N_DEV, r // N_DEV, c), (1, 0, 2, 3))


SMALL = (("norm_mix", D_MODEL), ("q_norm", HEAD_DIM), ("k_norm", HEAD_DIM), ("sinks", N_HEADS),
         ("conv_w", CONV_WIDTH * CONV_CH), ("conv_b", CONV_CH), ("conv_ln_g", CONV_CH), ("conv_ln_b", CONV_CH),
         ("norm_ffn", D_MODEL))
SMALL_TOTAL = DEPTH * sum(s for _, s in SMALL)
SMALL_ROWS = -(-SMALL_TOTAL // (LANES * SUBLANES)) * SUBLANES


def _pack_small(d):
    flat = jnp.concatenate([d[k].reshape(-1).astype(F32) for k, _ in SMALL])
    flat = jnp.pad(flat, (0, SMALL_ROWS * LANES - SMALL_TOTAL))
    return flat.reshape(SMALL_ROWS, LANES)


def _unpack_small(buf, shapes):
    flat = buf.reshape(-1)
    out, o = {}, 0
    for k, s in SMALL:
        out[k] = flat[o:o + DEPTH * s].reshape(shapes[k])
        o += DEPTH * s
    return out


def kernel(x, norm_mix, w_in, q_norm, k_norm, sinks, conv_w, conv_b, conv_ln_g, conv_ln_b, w_conv_out, w_out, norm_ffn, w_gate_up, w_down, loss_target, m_norm_mix, m_w_in, m_q_norm, m_k_norm, m_sinks, m_conv_w, m_conv_b, m_conv_ln_g, m_conv_ln_b, m_w_conv_out, m_w_out, m_norm_ffn, m_w_gate_up, m_w_down, v_norm_mix, v_w_in, v_q_norm, v_k_norm, v_sinks, v_conv_w, v_conv_b, v_conv_ln_g, v_conv_ln_b, v_w_conv_out, v_w_out, v_norm_ffn, v_w_gate_up, v_w_down):
    t = x.shape[1]
    me = 4 * lax.axis_index("x") + 2 * lax.axis_index("y") + lax.axis_index("c")
    xs = x.reshape(t, D_MODEL)
    target = loss_target.reshape(t, D_MODEL)

    g_in, g_co, g_out, g_gu, g_dn, g_cw = _exchange(
        [w_in.astype(BF16), w_conv_out.astype(BF16), w_out.astype(BF16), w_gate_up.astype(BF16),
         w_down.astype(BF16), conv_w], scatter=False, name="gather_weights")
    f_in = _cols_to_full(g_in)
    f_co = _cols_to_full(g_co)
    f_out = _rows_to_full(g_out)
    f_gu = _cols_to_full(g_gu)
    f_dn = _rows_to_full(g_dn)
    f_cw = _cols_to_full(g_cw)

    tabs = _rope_tables(t) + (_block_diag_ones(),)

    def layer_weights(l):
        return dict(w_qkv=f_in[l, :, :QKV_W], w_uug=f_in[l, :, QKV_W:QKV_W + UUG_W], w_gg=f_in[l, :, QKV_W + UUG_W:],
                    w_conv_out=f_co[l], w_out=f_out[l], w_gate_up=f_gu[l], w_gate=f_gu[l, :, :D_FF],
                    w_up=f_gu[l, :, D_FF:], w_down=f_dn[l])

    def layer_small(l):
        return dict(norm_mix=norm_mix[l][None], norm_ffn=norm_ffn[l][None],
                    gq=jnp.tile(q_norm[l], 2)[None], gk=jnp.tile(k_norm[l], 2)[None], sinks=sinks[l],
                    conv_w32=jnp.pad(f_cw[l], ((0, HALO - CONV_WIDTH), (0, 0))),
                    conv_b=conv_b[l][None], ln_g=conv_ln_g[l][None], ln_b=conv_ln_b[l][None])

    wls = [layer_weights(l) for l in range(DEPTH)]
    sls = [layer_small(l) for l in range(DEPTH)]
    saved = []
    cur = xs
    for l in range(DEPTH):
        cur, sv = _layer_fwd(cur, wls[l], sls[l], tabs, l)
        saved.append(sv)
    dy, loss_part = _loss_head(cur, target, "loss_head")
    loss = lax.psum(jnp.sum(loss_part), ("x", "y", "c"))

    gws, gss = [None] * DEPTH, [None] * DEPTH
    dcur = dy
    for l in reversed(range(DEPTH)):
        dcur, gws[l], gss[l] = _layer_bwd(dcur, saved[l], wls[l], sls[l], tabs, l)
    grad_x = dcur.reshape(x.shape)

    def stack(k):
        return jnp.stack([gws[l][k] for l in range(DEPTH)])

    d_in = jnp.concatenate([stack("w_qkv"), stack("w_uug"), stack("w_gg")], axis=2)
    d_gu = jnp.concatenate([stack("w_gate"), stack("w_up")], axis=2)
    sends = [_full_to_cols(d_in).astype(BF16), _full_to_cols(stack("w_conv_out")).astype(BF16),
             _full_to_rows(stack("w_out")).astype(BF16), _full_to_cols(d_gu).astype(BF16),
             _full_to_rows(stack("w_down")).astype(BF16)]
    parts = _exchange(sends, scatter=True, name="scatter_grads")

    def update(p, w, m, v, name, tr):
        shp = w.shape
        r = shp[0] * shp[1]
        flat = lambda a: a.reshape(r, shp[2])
        outs = _adamw(p.reshape(N_DEV, r, shp[2]), flat(w), flat(m), flat(v), name, tr)
        return [o.reshape(shp) for o in outs]

    u_in = update(parts[0], w_in, m_w_in, v_w_in, "adamw_w_in", 512)
    u_co = update(parts[1], w_conv_out, m_w_conv_out, v_w_conv_out, "adamw_w_conv_out", 512)
    u_out = update(parts[2], w_out, m_w_out, v_w_out, "adamw_w_out", 256)
    u_gu = update(parts[3], w_gate_up, m_w_gate_up, v_w_gate_up, "adamw_w_gate_up", 512)
    u_dn = update(parts[4], w_down, m_w_down, v_w_down, "adamw_w_down", 352)

    def fold_rows(a):
        return jnp.sum(a, axis=0)

    def fold_heads(a):
        return jnp.sum(a, axis=0).reshape(2, HEAD_DIM).sum(axis=0)

    small_g = {
        "norm_mix": jnp.stack([fold_rows(gss[l]["norm_mix"]) for l in range(DEPTH)]),
        "q_norm": jnp.stack([fold_heads(gss[l]["gq"]) for l in range(DEPTH)]),
        "k_norm": jnp.stack([fold_heads(gss[l]["gk"]) for l in range(DEPTH)]),
        "sinks": jnp.stack([gss[l]["sinks"][0, :N_HEADS] for l in range(DEPTH)]),
        "conv_w": jnp.stack([gss[l]["conv_w"].reshape(CONV_WIDTH, SUBLANES, CONV_CH).sum(axis=1)
                             for l in range(DEPTH)]),
        "conv_b": jnp.stack([fold_rows(gss[l]["conv_b"]) for l in range(DEPTH)]),
        "conv_ln_g": jnp.stack([fold_rows(gss[l]["ln_g"]) for l in range(DEPTH)]),
        "conv_ln_b": jnp.stack([fold_rows(gss[l]["ln_b"]) for l in range(DEPTH)]),
        "norm_ffn": jnp.stack([fold_rows(gss[l]["norm_ffn"]) for l in range(DEPTH)]),
    }
    (small_parts,) = _exchange([_pack_small(small_g)], scatter=False, name="gather_small_grads")
    shapes = {"norm_mix": norm_mix.shape, "q_norm": q_norm.shape, "k_norm": k_norm.shape, "sinks": sinks.shape,
              "conv_w": (DEPTH, CONV_WIDTH, CONV_CH), "conv_b": conv_b.shape, "conv_ln_g": conv_ln_g.shape,
              "conv_ln_b": conv_ln_b.shape, "norm_ffn": norm_ffn.shape}

    def widen(a):
        z = jnp.zeros((DEPTH, CONV_WIDTH, N_DEV, CONV_CH // N_DEV), F32)
        z = lax.dynamic_update_slice(z, a[:, :, None, :], (0, 0, me, 0))
        return z.reshape(DEPTH, CONV_WIDTH, CONV_CH)

    sw = _pack_small(dict(norm_mix=norm_mix, q_norm=q_norm, k_norm=k_norm, sinks=sinks, conv_w=widen(conv_w),
                          conv_b=conv_b, conv_ln_g=conv_ln_g, conv_ln_b=conv_ln_b, norm_ffn=norm_ffn))
    sm = _pack_small(dict(norm_mix=m_norm_mix, q_norm=m_q_norm, k_norm=m_k_norm, sinks=m_sinks,
                          conv_w=widen(m_conv_w), conv_b=m_conv_b, conv_ln_g=m_conv_ln_g, conv_ln_b=m_conv_ln_b,
                          norm_ffn=m_norm_ffn))
    sv_ = _pack_small(dict(norm_mix=v_norm_mix, q_norm=v_q_norm, k_norm=v_k_norm, sinks=v_sinks,
                           conv_w=widen(v_conv_w), conv_b=v_conv_b,
                           conv_ln_g=v_conv_ln_g, conv_ln_b=v_conv_ln_b, norm_ffn=v_norm_ffn))
    s_outs = [_unpack_small(o, shapes) for o in _adamw(small_parts, sw, sm, sv_, "adamw_small", SMALL_ROWS)]

    def narrow(a):
        a4 = a.reshape(DEPTH, CONV_WIDTH, N_DEV, CONV_CH // N_DEV)
        return lax.dynamic_slice(a4, (0, 0, me, 0), (DEPTH, CONV_WIDTH, 1, CONV_CH // N_DEV)).reshape(
            DEPTH, CONV_WIDTH, CONV_CH // N_DEV)

    big = {"w_in": u_in, "w_conv_out": u_co, "w_out": u_out, "w_gate_up": u_gu, "w_down": u_dn}
    order = ["norm_mix", "w_in", "q_norm", "k_norm", "sinks", "conv_w", "conv_b", "conv_ln_g", "conv_ln_b",
             "w_conv_out", "w_out", "norm_ffn", "w_gate_up", "w_down"]
    outs = [loss, grad_x]
    for kind in range(4):
        for name in order:
            if name in big:
                outs.append(big[name][kind])
            elif name == "conv_w":
                outs.append(narrow(s_outs[kind][name]))
            else:
                outs.append(s_outs[kind][name])
    return tuple(outs)
```

```python
import functools
import math

import jax
import jax.numpy as jnp
from jax import lax
from jax.experimental import pallas as pl
from jax.experimental.pallas import tpu as pltpu

F32 = jnp.float32
BF16 = jnp.bfloat16

D_MODEL = 1024
DEPTH = 4
N_HEADS = 16
N_KV_HEADS = 2
HEAD_DIM = 64
ROT_DIM = HEAD_DIM // 4
ROPE_THETA = 500000.0
BLOCK = 128
CONV_CH = D_MODEL // 2
CONV_WIDTH = 31
D_FF = 2816
EPS = 1e-6
Q_W = N_HEADS * HEAD_DIM
KV_W = N_KV_HEADS * HEAD_DIM
QKV_W = Q_W + 2 * KV_W
UUG_W = 2 * CONV_CH
GG_W = 2 * D_MODEL
IN_W = QKV_W + UUG_W + GG_W
N_DEV = 8

ADAM_LR = 0.001
ADAM_B1 = 0.9
ADAM_B2 = 0.999
ADAM_EPS = 1e-08
ADAM_WD = 0.01
ADAM_STEP = 10

LANES = 128
SUBLANES = 8
HALO = 32
VMEM_LIMIT = 56 * 1024 * 1024
NEG = -1e30


def _cp(sem=None):
    return pltpu.CompilerParams(dimension_semantics=sem, vmem_limit_bytes=VMEM_LIMIT)


def _sigmoid(z):
    return 1.0 / (1.0 + jnp.exp(-z))


def _rowgroup_sum(z):
    r, c = z.shape
    return jnp.sum(z.reshape(r // SUBLANES, SUBLANES, c), axis=0)


_DIMS = {"nn": (((1,), (0,)), ((), ())), "nt": (((1,), (1,)), ((), ())), "tn": (((0,), (0,)), ((), ()))}


def _mm(a, b, mode, *, out_dtype, name, tm, tn, tk, resid=None, after=None):
    if mode == "nn":
        (m, k), (k2, n) = a.shape, b.shape
        a_spec = pl.BlockSpec((tm, tk), lambda i, j, s: (i, s))
        b_spec = pl.BlockSpec((tk, tn), lambda i, j, s: (s, j))
    elif mode == "nt":
        (m, k), (n, k2) = a.shape, b.shape
        a_spec = pl.BlockSpec((tm, tk), lambda i, j, s: (i, s))
        b_spec = pl.BlockSpec((tn, tk), lambda i, j, s: (j, s))
    else:
        (k, m), (k2, n) = a.shape, b.shape
        a_spec = pl.BlockSpec((tk, tm), lambda i, j, s: (s, i))
        b_spec = pl.BlockSpec((tk, tn), lambda i, j, s: (s, j))
    assert k == k2 and m % tm == 0 and n % tn == 0 and k % tk == 0, (name, a.shape, b.shape, tm, tn, tk)
    nk = k // tk
    dims = _DIMS[mode]
    has_resid = resid is not None

    def body(*refs):
        a_ref, b_ref = refs[0], refs[1]
        r_ref = refs[2] if has_resid else None
        o_ref = refs[-1] if nk == 1 else refs[-2]
        part = lax.dot_general(a_ref[...].astype(BF16), b_ref[...].astype(BF16), dims, preferred_element_type=F32)

        def finish(acc):
            if has_resid:
                acc = acc + r_ref[...]
            o_ref[...] = acc.astype(out_dtype)

        if nk == 1:
            finish(part)
            return
        acc_ref = refs[-1]
        s = pl.program_id(2)

        @pl.when(s == 0)
        def _():
            acc_ref[...] = part

        @pl.when(s > 0)
        def _():
            acc_ref[...] += part

        @pl.when(s == nk - 1)
        def _():
            finish(acc_ref[...])

    in_specs = [a_spec, b_spec]
    args = [a, b]
    if has_resid:
        in_specs.append(pl.BlockSpec((tm, tn), lambda i, j, s: (i, j)))
        args.append(resid)
    if after is not None:
        in_specs.append(_ANY)
        args.append(after)
    return pl.pallas_call(
        body, name=name, grid=(m // tm, n // tn, nk),
        in_specs=in_specs, out_specs=pl.BlockSpec((tm, tn), lambda i, j, s: (i, j)),
        out_shape=jax.ShapeDtypeStruct((m, n), out_dtype),
        scratch_shapes=[] if nk == 1 else [pltpu.VMEM((tm, tn), F32)],
        compiler_params=_cp(("parallel", "parallel", "arbitrary")),
    )(*args)


def _rmsnorm_fwd(x, g, name, tm=512):
    t, d = x.shape

    def body(x_ref, g_ref, h_ref):
        xv = x_ref[...]
        r = lax.rsqrt(jnp.mean(xv * xv, axis=-1, keepdims=True) + EPS)
        h_ref[...] = ((xv * r) * g_ref[...]).astype(BF16)

    return pl.pallas_call(
        body, name=name, grid=(t // tm,),
        in_specs=[pl.BlockSpec((tm, d), lambda i: (i, 0)), pl.BlockSpec((1, d), lambda i: (0, 0))],
        out_specs=pl.BlockSpec((tm, d), lambda i: (i, 0)),
        out_shape=jax.ShapeDtypeStruct((t, d), BF16),
        compiler_params=_cp(("parallel",)),
    )(x, g)


def _rmsnorm_bwd(dh, x, g, resid, name, tm=512):
    t, d = x.shape

    def body(dh_ref, x_ref, g_ref, r_ref, dx_ref, dg_ref):
        @pl.when(pl.program_id(0) == 0)
        def _():
            dg_ref[...] = jnp.zeros_like(dg_ref)

        xv = x_ref[...]
        dhv = dh_ref[...]
        r = lax.rsqrt(jnp.mean(xv * xv, axis=-1, keepdims=True) + EPS)
        y = xv * r
        dg_ref[...] += _rowgroup_sum(dhv * y)
        dy = dhv * g_ref[...]
        dx_ref[...] = r_ref[...] + r * (dy - y * jnp.mean(dy * y, axis=-1, keepdims=True))

    row = pl.BlockSpec((tm, d), lambda i: (i, 0))
    return pl.pallas_call(
        body, name=name, grid=(t // tm,),
        in_specs=[row, row, pl.BlockSpec((1, d), lambda i: (0, 0)), row],
        out_specs=[row, pl.BlockSpec((SUBLANES, d), lambda i: (0, 0))],
        out_shape=[jax.ShapeDtypeStruct((t, d), F32), jax.ShapeDtypeStruct((SUBLANES, d), F32)],
        compiler_params=_cp(("arbitrary",)),
    )(dh, x, g, resid)


def _seg_sum(z, bd):
    hi = z.astype(BF16)
    lo = (z - hi.astype(F32)).astype(BF16)
    return jnp.dot(hi, bd, preferred_element_type=F32) + jnp.dot(lo, bd, preferred_element_type=F32)


def _partner(z, lane64):
    return jnp.where(lane64 < ROT_DIM // 2, pltpu.roll(z, LANES - ROT_DIM // 2, 1), pltpu.roll(z, ROT_DIM // 2, 1))


def _rope_tables(t):
    inv_freq = ROPE_THETA ** (-jnp.arange(0, ROT_DIM, 2, dtype=F32) / ROT_DIM)
    ang = jnp.arange(t, dtype=F32)[:, None] * inv_freq[None, :]
    cos, sin = jnp.cos(ang), jnp.sin(ang)
    c64 = jnp.concatenate([cos, cos, jnp.ones((t, HEAD_DIM - ROT_DIM), F32)], axis=1)
    s64 = jnp.concatenate([-sin, sin, jnp.zeros((t, HEAD_DIM - ROT_DIM), F32)], axis=1)
    return jnp.tile(c64, (1, 2)), jnp.tile(s64, (1, 2))


def _block_diag_ones():
    r = lax.broadcasted_iota(jnp.int32, (LANES, LANES), 0) // HEAD_DIM
    c = lax.broadcasted_iota(jnp.int32, (LANES, LANES), 1) // HEAD_DIM
    return (r == c).astype(BF16)


def _qk_prep_fwd(qkv, gq, gk, ctab, stab, bd, name, tm=512):
    t = qkv.shape[0]
    scale = HEAD_DIM ** -0.5
    n_qg = Q_W // LANES

    def body(q_ref, kv_ref, gq_ref, gk_ref, c_ref, s_ref, bd_ref, qs_ref, kd_ref, vd_ref):
        lane = lax.broadcasted_iota(jnp.int32, (tm, LANES), 1)
        lane64 = lane % HEAD_DIM
        lo_half = lane < HEAD_DIM
        cv, sv, bdv = c_ref[...], s_ref[...], bd_ref[...]

        def norm_rope(xg, g):
            r = lax.rsqrt(_seg_sum(xg * xg, bdv) * (1.0 / HEAD_DIM) + EPS)
            yn = (xg * r) * g
            return yn * cv + _partner(yn, lane64) * sv

        zero = jnp.zeros((BLOCK, LANES), BF16)
        lo_blk = lo_half[0:BLOCK]
        for c in range(n_qg):
            xg = q_ref[:, c * LANES:(c + 1) * LANES]
            qn = (norm_rope(xg, gq_ref[...]) * scale).astype(BF16)
            for b in range(tm // BLOCK):
                rows = qn[b * BLOCK:(b + 1) * BLOCK]
                qs_ref[b, 2 * c] = jnp.where(lo_blk, rows, zero)
                qs_ref[b, 2 * c + 1] = jnp.where(lo_blk, zero, rows)
        kk = norm_rope(kv_ref[:, 0:LANES], gk_ref[...])
        kr = pltpu.roll(kk, HEAD_DIM, 1)
        kd_ref[:, 0:LANES] = jnp.where(lo_half, kk, kr).astype(BF16)
        kd_ref[:, LANES:2 * LANES] = jnp.where(lo_half, kr, kk).astype(BF16)
        vv = kv_ref[:, LANES:2 * LANES]
        vr = pltpu.roll(vv, HEAD_DIM, 1)
        vd_ref[:, 0:LANES] = jnp.where(lo_half, vv, vr).astype(BF16)
        vd_ref[:, LANES:2 * LANES] = jnp.where(lo_half, vr, vv).astype(BF16)

    vec = pl.BlockSpec((1, LANES), lambda i: (0, 0))
    tab = pl.BlockSpec((tm, LANES), lambda i: (i, 0))
    return pl.pallas_call(
        body, name=name, grid=(t // tm,),
        in_specs=[pl.BlockSpec((tm, Q_W), lambda i: (i, 0)),
                  pl.BlockSpec((tm, 2 * KV_W), lambda i: (i, Q_W // (2 * KV_W))),
                  vec, vec, tab, tab, pl.BlockSpec((LANES, LANES), lambda i: (0, 0))],
        out_specs=[pl.BlockSpec((tm // BLOCK, N_HEADS, BLOCK, LANES), lambda i: (i, 0, 0, 0)),
                   pl.BlockSpec((tm, 2 * LANES), lambda i: (i, 0)),
                   pl.BlockSpec((tm, 2 * LANES), lambda i: (i, 0))],
        out_shape=[jax.ShapeDtypeStruct((t // BLOCK, N_HEADS, BLOCK, LANES), BF16),
                   jax.ShapeDtypeStruct((t, 2 * LANES), BF16), jax.ShapeDtypeStruct((t, 2 * LANES), BF16)],
        compiler_params=_cp(("parallel",)),
    )(qkv, qkv, gq, gk, ctab, stab, bd)


def _qk_prep_bwd(dqs, dkd, dvd, qkv, gq, gk, ctab, stab, bd, name, tm=512):
    t = qkv.shape[0]
    scale = HEAD_DIM ** -0.5
    n_qg = Q_W // LANES

    def body(dqs_ref, dkd_ref, dvd_ref, q_ref, kv_ref, gq_ref, gk_ref, c_ref, s_ref, bd_ref,
             dqkv_ref, dgq_ref, dgk_ref):
        @pl.when(pl.program_id(0) == 0)
        def _():
            dgq_ref[...] = jnp.zeros_like(dgq_ref)
            dgk_ref[...] = jnp.zeros_like(dgk_ref)

        lane = lax.broadcasted_iota(jnp.int32, (tm, LANES), 1)
        lane64 = lane % HEAD_DIM
        lo_half = lane < HEAD_DIM
        cv, sv, bdv = c_ref[...], s_ref[...], bd_ref[...]

        def bwd(xg, g, dout):
            r = lax.rsqrt(_seg_sum(xg * xg, bdv) * (1.0 / HEAD_DIM) + EPS)
            y = xg * r
            dyn = dout * cv + jnp.where(lane64 < ROT_DIM, _partner(dout * sv, lane64), 0.0)
            dy = dyn * g
            dx = r * (dy - y * (_seg_sum(dy * y, bdv) * (1.0 / HEAD_DIM)))
            return dx, _rowgroup_sum(dyn * y)

        dgq = jnp.zeros((SUBLANES, LANES), F32)
        for c in range(n_qg):
            sl = slice(c * LANES, (c + 1) * LANES)
            dx, dg = bwd(q_ref[:, sl], gq_ref[...], dqs_ref[:, sl] * scale)
            dqkv_ref[:, sl] = dx.astype(BF16)
            dgq = dgq + dg
        dgq_ref[...] += dgq
        dk = jnp.where(lo_half, dkd_ref[:, 0:LANES], dkd_ref[:, LANES:2 * LANES])
        dx, dg = bwd(kv_ref[:, 0:LANES], gk_ref[...], dk)
        dqkv_ref[:, Q_W:Q_W + LANES] = dx.astype(BF16)
        dgk_ref[...] += dg
        dv = jnp.where(lo_half, dvd_ref[:, 0:LANES], dvd_ref[:, LANES:2 * LANES])
        dqkv_ref[:, Q_W + LANES:Q_W + 2 * LANES] = dv.astype(BF16)

    vec = pl.BlockSpec((1, LANES), lambda i: (0, 0))
    tab = pl.BlockSpec((tm, LANES), lambda i: (i, 0))
    wide = pl.BlockSpec((tm, 2 * LANES), lambda i: (i, 0))
    acc = pl.BlockSpec((SUBLANES, LANES), lambda i: (0, 0))
    return pl.pallas_call(
        body, name=name, grid=(t // tm,),
        in_specs=[pl.BlockSpec((tm, Q_W), lambda i: (i, 0)), wide, wide,
                  pl.BlockSpec((tm, Q_W), lambda i: (i, 0)),
                  pl.BlockSpec((tm, 2 * KV_W), lambda i: (i, Q_W // (2 * KV_W))),
                  vec, vec, tab, tab, pl.BlockSpec((LANES, LANES), lambda i: (0, 0))],
        out_specs=[pl.BlockSpec((tm, QKV_W), lambda i: (i, 0)), acc, acc],
        out_shape=[jax.ShapeDtypeStruct((t, QKV_W), BF16), jax.ShapeDtypeStruct((SUBLANES, LANES), F32),
                   jax.ShapeDtypeStruct((SUBLANES, LANES), F32)],
        compiler_params=_cp(("arbitrary",)),
    )(dqs, dkd, dvd, qkv, qkv, gq, gk, ctab, stab, bd)


GROUP = N_HEADS // N_KV_HEADS
GROUP_ROWS = GROUP * BLOCK


def _attn_masks():
    row = lax.broadcasted_iota(jnp.int32, (BLOCK, BLOCK), 0)
    col = lax.broadcasted_iota(jnp.int32, (BLOCK, BLOCK), 1)
    return col <= row, col < HEAD_DIM


def _window_softmax(s_c, s_p, is_cur, has_prev, sink):
    s = jnp.where(is_cur, s_c, jnp.where(has_prev, s_p, NEG))
    m = jnp.maximum(jnp.max(s, axis=-1, keepdims=True), sink)
    e = jnp.exp(s - m)
    e_s = jnp.exp(sink - m)
    inv = 1.0 / (jnp.sum(e, axis=-1, keepdims=True) + e_s)
    return e * inv, e_s * inv


def _attn_fwd(qs, kd, vd, sinks, name):
    nb = qs.shape[0]
    t = nb * BLOCK

    def body(sink_ref, q_ref, kc_ref, kp_ref, vc_ref, vp_ref, o_ref, pc_scr, pp_scr):
        has_prev = pl.program_id(0) > 0
        is_cur, lo_half = _attn_masks()
        for j in range(N_KV_HEADS):
            ks = slice(j * LANES, (j + 1) * LANES)
            qg = q_ref[0, j * GROUP:(j + 1) * GROUP].reshape(GROUP_ROWS, LANES)
            s_c = lax.dot_general(qg, kc_ref[:, ks], _DIMS["nt"], preferred_element_type=F32)
            s_p = lax.dot_general(qg, kp_ref[:, ks], _DIMS["nt"], preferred_element_type=F32)
            for g in range(GROUP):
                rs = slice(g * BLOCK, (g + 1) * BLOCK)
                p, _ = _window_softmax(s_c[rs], s_p[rs], is_cur, has_prev, sink_ref[j * GROUP + g])
                pc_scr[rs, :] = jnp.where(is_cur, p, 0.0).astype(BF16)
                pp_scr[rs, :] = jnp.where(is_cur, 0.0, p).astype(BF16)
            o2 = (jnp.dot(pc_scr[...], vc_ref[:, ks], preferred_element_type=F32)
                  + jnp.dot(pp_scr[...], vp_ref[:, ks], preferred_element_type=F32))
            for pp in range(GROUP // 2):
                c0 = (j * (GROUP // 2) + pp) * LANES
                o_ref[:, c0:c0 + LANES] = jnp.where(lo_half, o2[2 * pp * BLOCK:(2 * pp + 1) * BLOCK],
                                                    o2[(2 * pp + 1) * BLOCK:(2 * pp + 2) * BLOCK])

    cur = lambda i: (i, 0)
    prev = lambda i: (jnp.maximum(i - 1, 0), 0)
    kvs = (BLOCK, 2 * LANES)
    return pl.pallas_call(
        body, name=name, grid=(nb,),
        in_specs=[pl.BlockSpec(memory_space=pltpu.SMEM),
                  pl.BlockSpec((1, N_HEADS, BLOCK, LANES), lambda i: (i, 0, 0, 0)),
                  pl.BlockSpec(kvs, cur), pl.BlockSpec(kvs, prev), pl.BlockSpec(kvs, cur), pl.BlockSpec(kvs, prev)],
        out_specs=pl.BlockSpec((BLOCK, Q_W), cur),
        out_shape=jax.ShapeDtypeStruct((t, Q_W), F32),
        scratch_shapes=[pltpu.VMEM((GROUP_ROWS, LANES), BF16), pltpu.VMEM((GROUP_ROWS, LANES), BF16)],
        compiler_params=_cp(("parallel",)),
    )(sinks, qs, kd, kd, vd, vd)


def _attn_bwd(qs, kd, vd, sinks, do, name):
    nb = qs.shape[0]
    t = nb * BLOCK

    def body(sink_ref, q_ref, do_ref, kc_ref, kp_ref, vc_ref, vp_ref,
             dq_ref, dk_ref, dv_ref, dsink_ref,
             carry_k, carry_v, dsink_acc, do_scr, pc_scr, pp_scr, dsc_scr, dsp_scr):
        i = pl.program_id(0)

        @pl.when(i == 0)
        def _():
            carry_k[...] = jnp.zeros_like(carry_k)
            carry_v[...] = jnp.zeros_like(carry_v)
            dsink_acc[...] = jnp.zeros_like(dsink_acc)

        @pl.when(i < nb)
        def _():
            has_prev = i > 0
            is_cur, lo_half = _attn_masks()
            srow = lax.broadcasted_iota(jnp.int32, (SUBLANES, LANES), 0)
            scol = lax.broadcasted_iota(jnp.int32, (SUBLANES, LANES), 1)
            dsink = jnp.zeros((SUBLANES, LANES), F32)
            for j in range(N_KV_HEADS):
                ks = slice(j * LANES, (j + 1) * LANES)
                kc, kp, vc, vp = kc_ref[:, ks], kp_ref[:, ks], vc_ref[:, ks], vp_ref[:, ks]
                qg = q_ref[0, j * GROUP:(j + 1) * GROUP].reshape(GROUP_ROWS, LANES)
                for pp in range(GROUP // 2):
                    c0 = (j * (GROUP // 2) + pp) * LANES
                    dop = do_ref[:, c0:c0 + LANES]
                    do_scr[2 * pp * BLOCK:(2 * pp + 1) * BLOCK, :] = jnp.where(lo_half, dop, 0.0).astype(BF16)
                    do_scr[(2 * pp + 1) * BLOCK:(2 * pp + 2) * BLOCK, :] = jnp.where(lo_half, 0.0, dop).astype(BF16)
                dog = do_scr[...]
                s_c = lax.dot_general(qg, kc, _DIMS["nt"], preferred_element_type=F32)
                s_p = lax.dot_general(qg, kp, _DIMS["nt"], preferred_element_type=F32)
                dp_c = lax.dot_general(dog, vc, _DIMS["nt"], preferred_element_type=F32)
                dp_p = lax.dot_general(dog, vp, _DIMS["nt"], preferred_element_type=F32)
                for g in range(GROUP):
                    rs = slice(g * BLOCK, (g + 1) * BLOCK)
                    h = j * GROUP + g
                    p, p_s = _window_softmax(s_c[rs], s_p[rs], is_cur, has_prev, sink_ref[h])
                    dp = jnp.where(is_cur, dp_c[rs], dp_p[rs])
                    delta = jnp.sum(p * dp, axis=-1, keepdims=True)
                    ds = p * (dp - delta)
                    dsv = -jnp.sum(p_s * delta, axis=0, keepdims=True)
                    dsink = dsink + jnp.where(jnp.logical_and(srow == 0, scol == h), dsv, 0.0)
                    pc_scr[rs, :] = jnp.where(is_cur, p, 0.0).astype(BF16)
                    pp_scr[rs, :] = jnp.where(is_cur, 0.0, p).astype(BF16)
                    dsc_scr[rs, :] = jnp.where(is_cur, ds, 0.0).astype(BF16)
                    dsp_scr[rs, :] = jnp.where(is_cur, 0.0, ds).astype(BF16)
                dsc, dsp = dsc_scr[...], dsp_scr[...]
                dq2 = jnp.dot(dsc, kc, preferred_element_type=F32) + jnp.dot(dsp, kp, preferred_element_type=F32)
                for pp in range(GROUP // 2):
                    c0 = (j * (GROUP // 2) + pp) * LANES
                    dq_ref[:, c0:c0 + LANES] = jnp.where(lo_half, dq2[2 * pp * BLOCK:(2 * pp + 1) * BLOCK],
                                                         dq2[(2 * pp + 1) * BLOCK:(2 * pp + 2) * BLOCK])
                dk_c = lax.dot_general(dsc, qg, _DIMS["tn"], preferred_element_type=F32)
                dk_p = lax.dot_general(dsp, qg, _DIMS["tn"], preferred_element_type=F32)
                dv_c = lax.dot_general(pc_scr[...], dog, _DIMS["tn"], preferred_element_type=F32)
                dv_p = lax.dot_general(pp_scr[...], dog, _DIMS["tn"], preferred_element_type=F32)
                dk_ref[:, ks] = carry_k[:, ks] + dk_p + pltpu.roll(dk_p, HEAD_DIM, 1)
                dv_ref[:, ks] = carry_v[:, ks] + dv_p + pltpu.roll(dv_p, HEAD_DIM, 1)
                carry_k[:, ks] = dk_c + pltpu.roll(dk_c, HEAD_DIM, 1)
                carry_v[:, ks] = dv_c + pltpu.roll(dv_c, HEAD_DIM, 1)
            dsink_acc[...] += dsink

        @pl.when(i == nb)
        def _():
            dk_ref[...] = carry_k[...]
            dv_ref[...] = carry_v[...]
            dsink_ref[...] = dsink_acc[...]

    last = nb - 1
    cur = lambda i: (jnp.minimum(i, last), 0)
    prev = lambda i: (jnp.clip(i - 1, 0, last), 0)
    kvs = (BLOCK, 2 * LANES)
    stk = pltpu.VMEM((GROUP_ROWS, LANES), BF16)
    return pl.pallas_call(
        body, name=name, grid=(nb + 1,),
        in_specs=[pl.BlockSpec(memory_space=pltpu.SMEM),
                  pl.BlockSpec((1, N_HEADS, BLOCK, LANES), lambda i: (jnp.minimum(i, last), 0, 0, 0)),
                  pl.BlockSpec((BLOCK, Q_W), cur),
                  pl.BlockSpec(kvs, cur), pl.BlockSpec(kvs, prev), pl.BlockSpec(kvs, cur), pl.BlockSpec(kvs, prev)],
        out_specs=[pl.BlockSpec((BLOCK, Q_W), cur), pl.BlockSpec(kvs, prev), pl.BlockSpec(kvs, prev),
                   pl.BlockSpec((SUBLANES, LANES), lambda i: (0, 0))],
        out_shape=[jax.ShapeDtypeStruct((t, Q_W), F32), jax.ShapeDtypeStruct((t, 2 * LANES), F32),
                   jax.ShapeDtypeStruct((t, 2 * LANES), F32), jax.ShapeDtypeStruct((SUBLANES, LANES), F32)],
        scratch_shapes=[pltpu.VMEM(kvs, F32), pltpu.VMEM(kvs, F32), pltpu.VMEM((SUBLANES, LANES), F32),
                        stk, stk, stk, stk, stk],
        compiler_params=_cp(("arbitrary",)),
    )(sinks, qs, do, kd, kd, vd, vd)


CONV_CHUNK = 64


def _conv_fwd(uug, w32, cb, lg, lb, name, tm=512):
    t = uug.shape[0]
    hb = tm // HALO

    def body(m_ref, h_ref, w_ref, cb_ref, lg_ref, lb_ref, y0_ref, y2_ref, a_ext):
        i = pl.program_id(0)
        a_ext[HALO:, :] = m_ref[:, 0:CONV_CH] * _sigmoid(m_ref[:, CONV_CH:])
        ah = h_ref[:, 0:CONV_CH] * _sigmoid(h_ref[:, CONV_CH:])
        a_ext[0:HALO, :] = jnp.where(i > 0, ah, 0.0)
        off = HALO - (CONV_WIDTH - 1)
        for c in range(tm // CONV_CHUNK):
            r0 = c * CONV_CHUNK
            acc = jnp.zeros((CONV_CHUNK, CONV_CH), F32)
            for k in range(CONV_WIDTH):
                acc = acc + w_ref[k:k + 1, :] * a_ext[r0 + off + k:r0 + off + k + CONV_CHUNK, :]
            y0 = acc + cb_ref[...]
            y0_ref[r0:r0 + CONV_CHUNK, :] = y0
            mu = jnp.mean(y0, axis=-1, keepdims=True)
            dlt = y0 - mu
            rstd = lax.rsqrt(jnp.mean(dlt * dlt, axis=-1, keepdims=True) + EPS)
            y1 = (dlt * rstd) * lg_ref[...] + lb_ref[...]
            y2_ref[r0:r0 + CONV_CHUNK, :] = (y1 * _sigmoid(y1)).astype(BF16)

    vec = pl.BlockSpec((1, CONV_CH), lambda i: (0, 0))
    return pl.pallas_call(
        body, name=name, grid=(t // tm,),
        in_specs=[pl.BlockSpec((tm, UUG_W), lambda i: (i, 0)),
                  pl.BlockSpec((HALO, UUG_W), lambda i: (jnp.maximum(i * hb - 1, 0), 0)),
                  pl.BlockSpec((HALO, CONV_CH), lambda i: (0, 0)), vec, vec, vec],
        out_specs=[pl.BlockSpec((tm, CONV_CH), lambda i: (i, 0)), pl.BlockSpec((tm, CONV_CH), lambda i: (i, 0))],
        out_shape=[jax.ShapeDtypeStruct((t, CONV_CH), F32), jax.ShapeDtypeStruct((t, CONV_CH), BF16)],
        scratch_shapes=[pltpu.VMEM((tm + HALO, CONV_CH), F32)],
        compiler_params=_cp(("parallel",)),
    )(uug, uug, w32, cb, lg, lb)


def _conv_bwd_ln(dy2, y0, lg, lb, name, tm=512):
    t = y0.shape[0]

    def body(dy2_ref, y0_ref, lg_ref, lb_ref, dy0_ref, dlg_ref, dlb_ref, dcb_ref):
        @pl.when(pl.program_id(0) == 0)
        def _():
            dlg_ref[...] = jnp.zeros_like(dlg_ref)
            dlb_ref[...] = jnp.zeros_like(dlb_ref)
            dcb_ref[...] = jnp.zeros_like(dcb_ref)

        y0 = y0_ref[...]
        mu = jnp.mean(y0, axis=-1, keepdims=True)
        dlt = y0 - mu
        rstd = lax.rsqrt(jnp.mean(dlt * dlt, axis=-1, keepdims=True) + EPS)
        yh = dlt * rstd
        y1 = yh * lg_ref[...] + lb_ref[...]
        sg = _sigmoid(y1)
        dy1 = dy2_ref[...] * (sg * (1.0 + y1 * (1.0 - sg)))
        dlg_ref[...] += _rowgroup_sum(dy1 * yh)
        dlb_ref[...] += _rowgroup_sum(dy1)
        dyh = dy1 * lg_ref[...]
        dy0 = rstd * (dyh - jnp.mean(dyh, axis=-1, keepdims=True)
                      - yh * jnp.mean(dyh * yh, axis=-1, keepdims=True))
        dcb_ref[...] += _rowgroup_sum(dy0)
        dy0_ref[...] = dy0

    row = pl.BlockSpec((tm, CONV_CH), lambda i: (i, 0))
    vec = pl.BlockSpec((1, CONV_CH), lambda i: (0, 0))
    acc = pl.BlockSpec((SUBLANES, CONV_CH), lambda i: (0, 0))
    accs = jax.ShapeDtypeStruct((SUBLANES, CONV_CH), F32)
    return pl.pallas_call(
        body, name=name, grid=(t // tm,),
        in_specs=[row, row, vec, vec], out_specs=[row, acc, acc, acc],
        out_shape=[jax.ShapeDtypeStruct((t, CONV_CH), F32), accs, accs, accs],
        compiler_params=_cp(("arbitrary",)),
    )(dy2, y0, lg, lb)


def _conv_bwd_taps(dy0, uug, w32, name, tm=512):
    t = uug.shape[0]
    hb = tm // HALO
    n_halo_blocks = t // HALO
    nt = t // tm

    def body(dm_ref, dn_ref, m_ref, h_ref, w_ref, duug_ref, dw_ref, a_ext, d_ext):
        i = pl.program_id(0)

        @pl.when(i == 0)
        def _():
            dw_ref[...] = jnp.zeros_like(dw_ref)

        u = m_ref[:, 0:CONV_CH]
        sg = _sigmoid(m_ref[:, CONV_CH:])
        a_ext[HALO:, :] = u * sg
        ah = h_ref[:, 0:CONV_CH] * _sigmoid(h_ref[:, CONV_CH:])
        a_ext[0:HALO, :] = jnp.where(i > 0, ah, 0.0)
        d_ext[0:tm, :] = dm_ref[...]
        d_ext[tm:, :] = jnp.where(i < nt - 1, dn_ref[...], 0.0)
        off = HALO - (CONV_WIDTH - 1)
        for c in range(tm // CONV_CHUNK):
            r0 = c * CONV_CHUNK
            da = jnp.zeros((CONV_CHUNK, CONV_CH), F32)
            for k in range(CONV_WIDTH):
                sh = CONV_WIDTH - 1 - k
                da = da + w_ref[k:k + 1, :] * d_ext[r0 + sh:r0 + sh + CONV_CHUNK, :]
            uc = u[r0:r0 + CONV_CHUNK, :]
            sc = sg[r0:r0 + CONV_CHUNK, :]
            duug_ref[r0:r0 + CONV_CHUNK, 0:CONV_CH] = (da * sc).astype(BF16)
            duug_ref[r0:r0 + CONV_CHUNK, CONV_CH:] = (da * uc * sc * (1.0 - sc)).astype(BF16)
            dch = d_ext[r0:r0 + CONV_CHUNK, :]
            for k in range(CONV_WIDTH):
                prod = dch * a_ext[r0 + off + k:r0 + off + k + CONV_CHUNK, :]
                dw_ref[k * SUBLANES:(k + 1) * SUBLANES, :] += _rowgroup_sum(prod)

    return pl.pallas_call(
        body, name=name, grid=(nt,),
        in_specs=[pl.BlockSpec((tm, CONV_CH), lambda i: (i, 0)),
                  pl.BlockSpec((HALO, CONV_CH), lambda i: (jnp.minimum((i + 1) * hb, n_halo_blocks - 1), 0)),
                  pl.BlockSpec((tm, UUG_W), lambda i: (i, 0)),
                  pl.BlockSpec((HALO, UUG_W), lambda i: (jnp.maximum(i * hb - 1, 0), 0)),
                  pl.BlockSpec((HALO, CONV_CH), lambda i: (0, 0))],
        out_specs=[pl.BlockSpec((tm, UUG_W), lambda i: (i, 0)),
                   pl.BlockSpec((CONV_WIDTH * SUBLANES, CONV_CH), lambda i: (0, 0))],
        out_shape=[jax.ShapeDtypeStruct((t, UUG_W), BF16),
                   jax.ShapeDtypeStruct((CONV_WIDTH * SUBLANES, CONV_CH), F32)],
        scratch_shapes=[pltpu.VMEM((tm + HALO, CONV_CH), F32), pltpu.VMEM((tm + HALO, CONV_CH), F32)],
        compiler_params=_cp(("arbitrary",)),
    )(dy0, dy0, uug, uug, w32)


def _merge_fwd(a_out, c_out, gg, name, tm=512):
    t, d = a_out.shape

    def body(a_ref, c_ref, g_ref, o_ref):
        o_ref[...] = (_sigmoid(g_ref[:, 0:d]) * a_ref[...] + _sigmoid(g_ref[:, d:]) * c_ref[...]).astype(BF16)

    row = pl.BlockSpec((tm, d), lambda i: (i, 0))
    return pl.pallas_call(
        body, name=name, grid=(t // tm,),
        in_specs=[row, row, pl.BlockSpec((tm, 2 * d), lambda i: (i, 0))], out_specs=row,
        out_shape=jax.ShapeDtypeStruct((t, d), BF16), compiler_params=_cp(("parallel",)),
    )(a_out, c_out, gg)


def _merge_bwd(dm, a_out, c_out, gg, name, tm=512):
    t, d = a_out.shape

    def body(dm_ref, a_ref, c_ref, g_ref, da_ref, dc_ref, dg_ref):
        dmv = dm_ref[...]
        sa = _sigmoid(g_ref[:, 0:d])
        sb = _sigmoid(g_ref[:, d:])
        da_ref[...] = dmv * sa
        dc_ref[...] = (dmv * sb).astype(BF16)
        dg_ref[:, 0:d] = (dmv * a_ref[...] * sa * (1.0 - sa)).astype(BF16)
        dg_ref[:, d:] = (dmv * c_ref[...] * sb * (1.0 - sb)).astype(BF16)

    row = pl.BlockSpec((tm, d), lambda i: (i, 0))
    wide = pl.BlockSpec((tm, 2 * d), lambda i: (i, 0))
    return pl.pallas_call(
        body, name=name, grid=(t // tm,),
        in_specs=[row, row, row, wide], out_specs=[row, row, wide],
        out_shape=[jax.ShapeDtypeStruct((t, d), F32), jax.ShapeDtypeStruct((t, d), BF16),
                   jax.ShapeDtypeStruct((t, 2 * d), BF16)],
        compiler_params=_cp(("parallel",)),
    )(dm, a_out, c_out, gg)


FF_TN = 1408


def _ffn_up_fwd(h2, wgu, name, tm=512):
    t, d = h2.shape
    nj = D_FF // FF_TN

    def body(h_ref, wg_ref, wu_ref, o_ref):
        hv = h_ref[...]
        g = jnp.dot(hv, wg_ref[...], preferred_element_type=F32)
        u = jnp.dot(hv, wu_ref[...], preferred_element_type=F32)
        o_ref[...] = ((g * _sigmoid(g)) * u).astype(BF16)

    return pl.pallas_call(
        body, name=name, grid=(nj, t // tm),
        in_specs=[pl.BlockSpec((tm, d), lambda j, i: (i, 0)),
                  pl.BlockSpec((d, FF_TN), lambda j, i: (0, j)),
                  pl.BlockSpec((d, FF_TN), lambda j, i: (0, j + nj))],
        out_specs=pl.BlockSpec((tm, FF_TN), lambda j, i: (i, j)),
        out_shape=jax.ShapeDtypeStruct((t, D_FF), BF16),
        compiler_params=_cp(("parallel", "parallel")),
    )(h2, wgu, wgu)


def _ffn_bwd_mid(h2, dx2, wgu, wd, name, tm=256):
    t, d = h2.shape
    nj = D_FF // FF_TN

    def body(h_ref, dx_ref, wg_ref, wu_ref, wd_ref, dg_ref, du_ref):
        hv = h_ref[...]
        g = jnp.dot(hv, wg_ref[...], preferred_element_type=F32)
        u = jnp.dot(hv, wu_ref[...], preferred_element_type=F32)
        dact = lax.dot_general(dx_ref[...].astype(BF16), wd_ref[...], _DIMS["nt"], preferred_element_type=F32)
        sg = _sigmoid(g)
        silu = g * sg
        dg_ref[...] = (dact * u * (sg * (1.0 + g * (1.0 - sg)))).astype(BF16)
        du_ref[...] = (dact * silu).astype(BF16)

    return pl.pallas_call(
        body, name=name, grid=(nj, t // tm),
        in_specs=[pl.BlockSpec((tm, d), lambda j, i: (i, 0)), pl.BlockSpec((tm, d), lambda j, i: (i, 0)),
                  pl.BlockSpec((d, FF_TN), lambda j, i: (0, j)),
                  pl.BlockSpec((d, FF_TN), lambda j, i: (0, j + nj)),
                  pl.BlockSpec((FF_TN, d), lambda j, i: (j, 0))],
        out_specs=[pl.BlockSpec((tm, FF_TN), lambda j, i: (i, j)), pl.BlockSpec((tm, FF_TN), lambda j, i: (i, j))],
        out_shape=[jax.ShapeDtypeStruct((t, D_FF), BF16), jax.ShapeDtypeStruct((t, D_FF), BF16)],
        compiler_params=_cp(("parallel", "parallel")),
    )(h2, dx2, wgu, wgu, wd)


def _loss_head(y, target, name, tm=512):
    t, d = y.shape

    def body(y_ref, t_ref, dy_ref, loss_ref):
        @pl.when(pl.program_id(0) == 0)
        def _():
            loss_ref[...] = jnp.zeros_like(loss_ref)

        e = y_ref[...] - t_ref[...]
        dy_ref[...] = e * (1.0 / d)
        s = _rowgroup_sum(e * e)
        acc = s[:, 0:LANES]
        for c in range(1, d // LANES):
            acc = acc + s[:, c * LANES:(c + 1) * LANES]
        loss_ref[...] += acc * (0.5 / d)

    row = pl.BlockSpec((tm, d), lambda i: (i, 0))
    return pl.pallas_call(
        body, name=name, grid=(t // tm,),
        in_specs=[row, row], out_specs=[row, pl.BlockSpec((SUBLANES, LANES), lambda i: (0, 0))],
        out_shape=[jax.ShapeDtypeStruct((t, d), F32), jax.ShapeDtypeStruct((SUBLANES, LANES), F32)],
        compiler_params=_cp(("arbitrary",)),
    )(y, target)


def _exchange(arrays, scatter, name):
    n = len(arrays)

    def body(*refs):
        ins, outs = refs[:n], refs[n:2 * n]
        send_sems, recv_sems, local_sems = refs[2 * n:]
        x, y, c = lax.axis_index("x"), lax.axis_index("y"), lax.axis_index("c")
        me = 4 * x + 2 * y + c

        def peer(k):
            px, py, pc = x ^ ((k >> 2) & 1), y ^ ((k >> 1) & 1), c ^ (k & 1)
            return (px, py, pc), 4 * px + 2 * py + pc

        def src(a, dst_id):
            return ins[a].at[dst_id] if scatter else ins[a]

        locals_ = [pltpu.make_async_copy(src(a, me), outs[a].at[me], local_sems.at[a]) for a in range(n)]
        for cp in locals_:
            cp.start()
        sends = []
        for k in range(1, N_DEV):
            dev, pid = peer(k)
            for a in range(n):
                sends.append(pltpu.make_async_remote_copy(
                    src_ref=src(a, pid), dst_ref=outs[a].at[me],
                    send_sem=send_sems.at[a, k], recv_sem=recv_sems.at[a, k],
                    device_id=dev, device_id_type=pl.DeviceIdType.MESH))
        for cp in sends:
            cp.start()
        for k in range(1, N_DEV):
            dev, pid = peer(k)
            for a in range(n):
                pltpu.make_async_remote_copy(
                    src_ref=src(a, pid), dst_ref=outs[a].at[pid],
                    send_sem=send_sems.at[a, k], recv_sem=recv_sems.at[a, k],
                    device_id=dev, device_id_type=pl.DeviceIdType.MESH).wait_recv()
        for cp in sends:
            cp.wait_send()
        for cp in locals_:
            cp.wait()

    def out_shape(a):
        return jax.ShapeDtypeStruct(a.shape if scatter else (N_DEV,) + a.shape, a.dtype)

    anyspec = pl.BlockSpec(memory_space=pl.ANY)
    return pl.pallas_call(
        body, name=name,
        in_specs=[anyspec] * n, out_specs=[anyspec] * n,
        out_shape=[out_shape(a) for a in arrays],
        scratch_shapes=[pltpu.SemaphoreType.DMA((n, N_DEV)), pltpu.SemaphoreType.DMA((n, N_DEV)),
                        pltpu.SemaphoreType.DMA((n,))],
    )(*arrays)


_HBM = pl.BlockSpec(memory_space=pltpu.HBM)
_SEM = pl.BlockSpec(memory_space=pltpu.SEMAPHORE)
_ANY = pl.BlockSpec(memory_space=pl.ANY)
_EFFECT = pltpu.SideEffectType.DATAFLOW_SIDE_EFFECTING


def _mesh_peer(k):
    x, y, c = lax.axis_index("x"), lax.axis_index("y"), lax.axis_index("c")
    px, py, pc = x ^ ((k >> 2) & 1), y ^ ((k >> 1) & 1), c ^ (k & 1)
    return (px, py, pc), 4 * px + 2 * py + pc


def _exchange_start(arrays, scatter, name, after):
    n = len(arrays)
    lands = [lax.empty(a.shape if scatter else (N_DEV,) + a.shape, a.dtype) for a in arrays]

    def body(*refs):
        ins, land_refs = refs[:n], refs[n:2 * n]
        send_sems, recv_sems = refs[2 * n + 1], refs[2 * n + 2]
        token = refs[-1]
        _, me = _mesh_peer(0)
        for k in range(1, N_DEV):
            dev, pid = _mesh_peer(k)
            for a in range(n):
                pltpu.make_async_remote_copy(
                    src_ref=ins[a].at[pid] if scatter else ins[a], dst_ref=land_refs[a].at[me],
                    send_sem=send_sems.at[a * N_DEV + k], recv_sem=recv_sems.at[a * N_DEV + k],
                    device_id=dev, device_id_type=pl.DeviceIdType.MESH).start()
        token[...] = jnp.zeros_like(token)

    hbm_in = [pltpu.with_memory_space_constraint(a, pltpu.HBM) for a in list(arrays) + lands]
    outs = pl.pallas_call(
        body, name=name,
        in_specs=[_HBM] * (2 * n) + [_ANY],
        out_specs=[_SEM, _SEM] + [_HBM] * (2 * n) + [pl.BlockSpec(memory_space=pltpu.VMEM)],
        out_shape=[pltpu.SemaphoreType.DMA((n * N_DEV,)), pltpu.SemaphoreType.DMA((n * N_DEV,))]
        + [pltpu.HBM(a.shape, a.dtype) for a in hbm_in]
        + [jax.ShapeDtypeStruct((SUBLANES, LANES), F32)],
        input_output_aliases={i: 2 + i for i in range(2 * n)},
        compiler_params=pltpu.CompilerParams(has_side_effects=_EFFECT),
    )(*hbm_in, after)
    return outs[0], outs[1], outs[2:2 + n], outs[2 + n:2 + 2 * n], outs[-1]


def _exchange_wait(started, scatter, name, after):
    send_sems, recv_sems, srcs, lands, _ = started
    n = len(srcs)

    def body(*refs):
        ins, land_refs = refs[:n], refs[n:2 * n]
        send_sems, recv_sems = refs[2 * n], refs[2 * n + 1]
        local_sems = refs[-1]
        _, me = _mesh_peer(0)
        locals_ = [pltpu.make_async_copy(ins[a].at[me] if scatter else ins[a], land_refs[a].at[me], local_sems.at[a])
                   for a in range(n)]
        for cp in locals_:
            cp.start()
        copies = []
        for k in range(1, N_DEV):
            dev, pid = _mesh_peer(k)
            for a in range(n):
                copies.append(pltpu.make_async_remote_copy(
                    src_ref=ins[a].at[pid] if scatter else ins[a], dst_ref=land_refs[a].at[pid],
                    send_sem=send_sems.at[a * N_DEV + k], recv_sem=recv_sems.at[a * N_DEV + k],
                    device_id=dev, device_id_type=pl.DeviceIdType.MESH))
        for cp in copies:
            cp.wait_recv()
        for cp in copies:
            cp.wait_send()
        for cp in locals_:
            cp.wait()

    outs = pl.pallas_call(
        body, name=name,
        in_specs=[_HBM] * (2 * n) + [_SEM, _SEM, _ANY],
        out_specs=[_HBM] * (2 * n),
        out_shape=[pltpu.HBM(a.shape, a.dtype) for a in list(srcs) + list(lands)],
        input_output_aliases={i: i for i in range(2 * n)},
        scratch_shapes=[pltpu.SemaphoreType.DMA((n,))],
        compiler_params=pltpu.CompilerParams(has_side_effects=_EFFECT),
    )(*srcs, *lands, send_sems, recv_sems, after)
    return outs[n:]


def _adamw(parts, w, m, v, name, tr):
    nl = len(parts)
    r, c = parts[0].shape[1:]
    assert w.shape == (nl * r, c) and r % tr == 0, (name, w.shape, r, tr)
    nt = r // tr
    c1 = 1.0 - ADAM_B1 ** ADAM_STEP
    c2 = 1.0 - ADAM_B2 ** ADAM_STEP

    def body(*refs):
        p_refs = refs[:nl]
        w_ref, m_ref, v_ref, g_ref, d_ref, nm_ref, nv_ref = refs[nl:]
        layer = pl.program_id(0)
        for k in range(nl):
            @pl.when(layer == k)
            def _(p_ref=p_refs[k]):
                g = p_ref[0].astype(F32)
                for s in range(1, N_DEV):
                    g = g + p_ref[s].astype(F32)
                nm = ADAM_B1 * m_ref[...] + (1.0 - ADAM_B1) * g
                nv = ADAM_B2 * v_ref[...] + (1.0 - ADAM_B2) * (g * g)
                g_ref[...] = g
                nm_ref[...] = nm
                nv_ref[...] = nv
                d_ref[...] = -ADAM_LR * ((nm / c1) / (jnp.sqrt(nv / c2) + ADAM_EPS) + ADAM_WD * w_ref[...])

    def part_spec(k):
        return pl.BlockSpec((N_DEV, tr, c), lambda l, i: (0, jnp.where(l == k, i, 0), 0))

    row = pl.BlockSpec((tr, c), lambda l, i: (l * nt + i, 0))
    o = jax.ShapeDtypeStruct((nl * r, c), F32)
    return pl.pallas_call(
        body, name=name, grid=(nl, nt),
        in_specs=[part_spec(k) for k in range(nl)] + [row, row, row],
        out_specs=[row, row, row, row], out_shape=[o, o, o, o],
        compiler_params=_cp(("arbitrary", "arbitrary")),
    )(*parts, w, m, v)


def _layer_fwd(x, wl, sl, tabs, l):
    ctab, stab, bd = tabs
    n = f"l{l}_"
    h = _rmsnorm_fwd(x, sl["norm_mix"], n + "norm_mix")
    qkv = _mm(h, wl["w_qkv"], "nn", out_dtype=F32, name=n + "proj_qkv", tm=1024, tn=QKV_W, tk=D_MODEL)
    uug = _mm(h, wl["w_uug"], "nn", out_dtype=F32, name=n + "proj_uug", tm=1024, tn=UUG_W, tk=D_MODEL)
    gg = _mm(h, wl["w_gg"], "nn", out_dtype=F32, name=n + "proj_gg", tm=1024, tn=GG_W, tk=D_MODEL)
    qs, kd, vd = _qk_prep_fwd(qkv, sl["gq"], sl["gk"], ctab, stab, bd, n + "qk_prep")
    a_out = _attn_fwd(qs, kd, vd, sl["sinks"], n + "attn")
    y0, y2 = _conv_fwd(uug, sl["conv_w32"], sl["conv_b"], sl["ln_g"], sl["ln_b"], n + "conv")
    c_out = _mm(y2, wl["w_conv_out"], "nn", out_dtype=F32, name=n + "conv_out", tm=1024, tn=D_MODEL, tk=CONV_CH)
    merged = _merge_fwd(a_out, c_out, gg, n + "merge")
    x1 = _mm(merged, wl["w_out"], "nn", out_dtype=F32, name=n + "out_proj", tm=1024, tn=D_MODEL, tk=D_MODEL,
             resid=x)
    h2 = _rmsnorm_fwd(x1, sl["norm_ffn"], n + "norm_ffn")
    act = _ffn_up_fwd(h2, wl["w_gate_up"], n + "ffn_up")
    x2 = _mm(act, wl["w_down"], "nn", out_dtype=F32, name=n + "ffn_down", tm=1024, tn=D_MODEL, tk=FF_TN, resid=x1)
    saved = dict(x=x, h=h, qkv=qkv, uug=uug, gg=gg, qs=qs, kd=kd, vd=vd, a_out=a_out, y0=y0, y2=y2,
                 c_out=c_out, merged=merged, x1=x1, h2=h2, act=act)
    return x2, saved


def _layer_bwd(dx2, sv, wl, sl, tabs, l, after=None):
    ctab, stab, bd = tabs
    n = f"l{l}_b_"
    tk = 1024
    gw, gs = {}, {}
    gw["w_down"] = _mm(sv["act"], dx2, "tn", out_dtype=F32, name=n + "dw_down", tm=FF_TN, tn=D_MODEL, tk=tk,
                       after=after)
    dg, du = _ffn_bwd_mid(sv["h2"], dx2, wl["w_gate_up"], wl["w_down"], n + "ffn_mid")
    dh2 = _mm(dg, wl["w_gate"], "nt", out_dtype=F32, name=n + "dh2_g", tm=1024, tn=D_MODEL, tk=FF_TN)
    dh2 = _mm(du, wl["w_up"], "nt", out_dtype=F32, name=n + "dh2_u", tm=1024, tn=D_MODEL, tk=FF_TN, resid=dh2)
    gw["w_gate"] = _mm(sv["h2"], dg, "tn", out_dtype=F32, name=n + "dw_gate", tm=D_MODEL, tn=FF_TN, tk=tk)
    gw["w_up"] = _mm(sv["h2"], du, "tn", out_dtype=F32, name=n + "dw_up", tm=D_MODEL, tn=FF_TN, tk=tk)
    dx1, gs["norm_ffn"] = _rmsnorm_bwd(dh2, sv["x1"], sl["norm_ffn"], dx2, n + "norm_ffn")
    dmerged = _mm(dx1, wl["w_out"], "nt", out_dtype=F32, name=n + "dmerged", tm=1024, tn=D_MODEL, tk=D_MODEL)
    gw["w_out"] = _mm(sv["merged"], dx1, "tn", out_dtype=F32, name=n + "dw_out", tm=D_MODEL, tn=D_MODEL, tk=tk)
    da_out, dc_out, dgg = _merge_bwd(dmerged, sv["a_out"], sv["c_out"], sv["gg"], n + "merge")
    dy2 = _mm(dc_out, wl["w_conv_out"], "nt", out_dtype=F32, name=n + "dy2", tm=1024, tn=CONV_CH, tk=D_MODEL)
    gw["w_conv_out"] = _mm(sv["y2"], dc_out, "tn", out_dtype=F32, name=n + "dw_conv_out", tm=CONV_CH, tn=D_MODEL,
                           tk=tk)
    dy0, gs["ln_g"], gs["ln_b"], gs["conv_b"] = _conv_bwd_ln(dy2, sv["y0"], sl["ln_g"], sl["ln_b"], n + "conv_ln")
    duug, gs["conv_w"] = _conv_bwd_taps(dy0, sv["uug"], sl["conv_w32"], n + "conv_taps")
    dqs, dkd, dvd, gs["sinks"] = _attn_bwd(sv["qs"], sv["kd"], sv["vd"], sl["sinks"], da_out, n + "attn")
    dqkv, gs["gq"], gs["gk"] = _qk_prep_bwd(dqs, dkd, dvd, sv["qkv"], sl["gq"], sl["gk"], ctab, stab, bd,
                                            n + "qk_prep")
    dh = _mm(dqkv, wl["w_qkv"], "nt", out_dtype=F32, name=n + "dh_qkv", tm=1024, tn=D_MODEL, tk=QKV_W)
    dh = _mm(duug, wl["w_uug"], "nt", out_dtype=F32, name=n + "dh_uug", tm=1024, tn=D_MODEL, tk=UUG_W, resid=dh)
    dh = _mm(dgg, wl["w_gg"], "nt", out_dtype=F32, name=n + "dh_gg", tm=1024, tn=D_MODEL, tk=GG_W, resid=dh)
    gw["w_qkv"] = _mm(sv["h"], dqkv, "tn", out_dtype=F32, name=n + "dw_qkv", tm=D_MODEL, tn=QKV_W, tk=tk)
    gw["w_uug"] = _mm(sv["h"], duug, "tn", out_dtype=F32, name=n + "dw_uug", tm=D_MODEL, tn=UUG_W, tk=tk)
    gw["w_gg"] = _mm(sv["h"], dgg, "tn", out_dtype=F32, name=n + "dw_gg", tm=D_MODEL, tn=GG_W, tk=tk)
    dx, gs["norm_mix"] = _rmsnorm_bwd(dh, sv["x"], sl["norm_mix"], dx1, n + "norm_mix")
    return dx, gw, gs


def _cols_to_full(g):
    n, l, r, c = g.shape
    return jnp.transpose(g, (1, 2, 0, 3)).reshape(l, r, n * c)


def _rows_to_full(g):
    n, l, r, c = g.shape
    return jnp.transpose(g, (1, 0, 2, 3)).reshape(l, n * r, c)


def _full_to_cols(w):
    l, r, c = w.shape
    return jnp.transpose(w.reshape(l, r, N_DEV, c // N_DEV), (2, 0, 1, 3))


def _full_to_rows(w):
    l, r, c = w.shape
    return jnp.transpose(w.reshape(l, N_DEV, r // N_DEV, c), (1, 0, 2, 3))


SMALL = (("norm_mix", D_MODEL), ("q_norm", HEAD_DIM), ("k_norm", HEAD_DIM), ("sinks", N_HEADS),
         ("conv_w", CONV_WIDTH * CONV_CH), ("conv_b", CONV_CH), ("conv_ln_g", CONV_CH), ("conv_ln_b", CONV_CH),
         ("norm_ffn", D_MODEL))
SMALL_TOTAL = DEPTH * sum(s for _, s in SMALL)
SMALL_ROWS = -(-SMALL_TOTAL // (LANES * SUBLANES)) * SUBLANES


def _pack_small(d):
    flat = jnp.concatenate([d[k].reshape(-1).astype(F32) for k, _ in SMALL])
    flat = jnp.pad(flat, (0, SMALL_ROWS * LANES - SMALL_TOTAL))
    return flat.reshape(SMALL_ROWS, LANES)


def _unpack_small(buf, shapes):
    flat = buf.reshape(-1)
    out, o = {}, 0
    for k, s in SMALL:
        out[k] = flat[o:o + DEPTH * s].reshape(shapes[k])
        o += DEPTH * s
    return out


def kernel(x, norm_mix, w_in, q_norm, k_norm, sinks, conv_w, conv_b, conv_ln_g, conv_ln_b, w_conv_out, w_out, norm_ffn, w_gate_up, w_down, loss_target, m_norm_mix, m_w_in, m_q_norm, m_k_norm, m_sinks, m_conv_w, m_conv_b, m_conv_ln_g, m_conv_ln_b, m_w_conv_out, m_w_out, m_norm_ffn, m_w_gate_up, m_w_down, v_norm_mix, v_w_in, v_q_norm, v_k_norm, v_sinks, v_conv_w, v_conv_b, v_conv_ln_g, v_conv_ln_b, v_w_conv_out, v_w_out, v_norm_ffn, v_w_gate_up, v_w_down):
    t = x.shape[1]
    me = 4 * lax.axis_index("x") + 2 * lax.axis_index("y") + lax.axis_index("c")
    xs = x.reshape(t, D_MODEL)
    target = loss_target.reshape(t, D_MODEL)

    def shards(l):
        return [w_in[l].astype(BF16), w_conv_out[l].astype(BF16), w_out[l].astype(BF16),
                w_gate_up[l].astype(BF16), w_down[l].astype(BF16)]

    def layer_weights(g):
        g_in, g_co, g_out, g_gu, g_dn = g[:5]
        f_in = _cols_to_full(g_in[:, None])[0]
        f_gu = _cols_to_full(g_gu[:, None])[0]
        return dict(w_qkv=f_in[:, :QKV_W], w_uug=f_in[:, QKV_W:QKV_W + UUG_W], w_gg=f_in[:, QKV_W + UUG_W:],
                    w_conv_out=_cols_to_full(g_co[:, None])[0], w_out=_rows_to_full(g_out[:, None])[0],
                    w_gate_up=f_gu, w_gate=f_gu[:, :D_FF], w_up=f_gu[:, D_FF:], w_down=_rows_to_full(g_dn[:, None])[0])

    gathered = _exchange(shards(0) + [conv_w], scatter=False, name="gather_weights_0")
    f_cw = _cols_to_full(gathered[5])
    tabs = _rope_tables(t) + (_block_diag_ones(),)

    def layer_small(l, token):
        gain = norm_mix[l][None] if token is None else norm_mix[l][None] + token[0:1, 0:1]
        return dict(norm_mix=gain, norm_ffn=norm_ffn[l][None],
                    gq=jnp.tile(q_norm[l], 2)[None], gk=jnp.tile(k_norm[l], 2)[None], sinks=sinks[l],
                    conv_w32=jnp.pad(f_cw[l], ((0, HALO - CONV_WIDTH), (0, 0))),
                    conv_b=conv_b[l][None], ln_g=conv_ln_g[l][None], ln_b=conv_ln_b[l][None])

    wls, sls, saved = [], [], []
    cur = xs
    for l in range(DEPTH):
        pending = None
        if l + 1 < DEPTH:
            pending = _exchange_start(shards(l + 1), False, f"gather_start_{l + 1}", after=gathered[0])
        wls.append(layer_weights(gathered))
        sls.append(layer_small(l, None if pending is None else pending[4]))
        cur, sv = _layer_fwd(cur, wls[l], sls[l], tabs, l)
        saved.append(sv)
        if pending is not None:
            gathered = _exchange_wait(pending, False, f"gather_wait_{l + 1}", after=cur)
    dy, loss_part = _loss_head(cur, target, "loss_head")
    loss = lax.psum(jnp.sum(loss_part), ("x", "y", "c"))

    def slabs(gw):
        d_in = jnp.concatenate([gw["w_qkv"], gw["w_uug"], gw["w_gg"]], axis=1)[None]
        d_gu = jnp.concatenate([gw["w_gate"], gw["w_up"]], axis=1)[None]
        return [_full_to_cols(d_in)[:, 0].astype(BF16), _full_to_cols(gw["w_conv_out"][None])[:, 0].astype(BF16),
                _full_to_rows(gw["w_out"][None])[:, 0].astype(BF16), _full_to_cols(d_gu)[:, 0].astype(BF16),
                _full_to_rows(gw["w_down"][None])[:, 0].astype(BF16)]

    gss, parts_l = [None] * DEPTH, [None] * DEPTH
    dcur = dy
    pending = None
    for l in reversed(range(DEPTH)):
        dcur, gw, gss[l] = _layer_bwd(dcur, saved[l], wls[l], sls[l], tabs, l,
                                      after=None if pending is None else pending[4])
        if pending is not None:
            parts_l[l + 1] = _exchange_wait(pending, True, f"scatter_wait_{l + 1}", after=dcur)
        if l > 0:
            pending = _exchange_start(slabs(gw), True, f"scatter_start_{l}",
                                      after=dcur if l == DEPTH - 1 else parts_l[l + 1][0])
        else:
            parts_l[0] = _exchange(slabs(gw), scatter=True, name="scatter_grads_0")
    grad_x = dcur.reshape(x.shape)
    parts = [[parts_l[l][a] for l in range(DEPTH)] for a in range(5)]

    def update(p, w, m, v, name, tr):
        shp = w.shape
        r = shp[0] * shp[1]
        flat = lambda a: a.reshape(r, shp[2])
        outs = _adamw(p, flat(w), flat(m), flat(v), name, tr)
        return [o.reshape(shp) for o in outs]

    u_in = update(parts[0], w_in, m_w_in, v_w_in, "adamw_w_in", 256)
    u_co = update(parts[1], w_conv_out, m_w_conv_out, v_w_conv_out, "adamw_w_conv_out", 512)
    u_out = update(parts[2], w_out, m_w_out, v_w_out, "adamw_w_out", 128)
    u_gu = update(parts[3], w_gate_up, m_w_gate_up, v_w_gate_up, "adamw_w_gate_up", 256)
    u_dn = update(parts[4], w_down, m_w_down, v_w_down, "adamw_w_down", 176)

    def fold_rows(a):
        return jnp.sum(a, axis=0)

    def fold_heads(a):
        return jnp.sum(a, axis=0).reshape(2, HEAD_DIM).sum(axis=0)

    small_g = {
        "norm_mix": jnp.stack([fold_rows(gss[l]["norm_mix"]) for l in range(DEPTH)]),
        "q_norm": jnp.stack([fold_heads(gss[l]["gq"]) for l in range(DEPTH)]),
        "k_norm": jnp.stack([fold_heads(gss[l]["gk"]) for l in range(DEPTH)]),
        "sinks": jnp.stack([gss[l]["sinks"][0, :N_HEADS] for l in range(DEPTH)]),
        "conv_w": jnp.stack([gss[l]["conv_w"].reshape(CONV_WIDTH, SUBLANES, CONV_CH).sum(axis=1)
                             for l in range(DEPTH)]),
        "conv_b": jnp.stack([fold_rows(gss[l]["conv_b"]) for l in range(DEPTH)]),
        "conv_ln_g": jnp.stack([fold_rows(gss[l]["ln_g"]) for l in range(DEPTH)]),
        "conv_ln_b": jnp.stack([fold_rows(gss[l]["ln_b"]) for l in range(DEPTH)]),
        "norm_ffn": jnp.stack([fold_rows(gss[l]["norm_ffn"]) for l in range(DEPTH)]),
    }
    (small_parts,) = _exchange([_pack_small(small_g)], scatter=False, name="gather_small_grads")
    shapes = {"norm_mix": norm_mix.shape, "q_norm": q_norm.shape, "k_norm": k_norm.shape, "sinks": sinks.shape,
              "conv_w": (DEPTH, CONV_WIDTH, CONV_CH), "conv_b": conv_b.shape, "conv_ln_g": conv_ln_g.shape,
              "conv_ln_b": conv_ln_b.shape, "norm_ffn": norm_ffn.shape}

    def widen(a):
        z = jnp.zeros((DEPTH, CONV_WIDTH, N_DEV, CONV_CH // N_DEV), F32)
        z = lax.dynamic_update_slice(z, a[:, :, None, :], (0, 0, me, 0))
        return z.reshape(DEPTH, CONV_WIDTH, CONV_CH)

    sw = _pack_small(dict(norm_mix=norm_mix, q_norm=q_norm, k_norm=k_norm, sinks=sinks, conv_w=widen(conv_w),
                          conv_b=conv_b, conv_ln_g=conv_ln_g, conv_ln_b=conv_ln_b, norm_ffn=norm_ffn))
    sm = _pack_small(dict(norm_mix=m_norm_mix, q_norm=m_q_norm, k_norm=m_k_norm, sinks=m_sinks,
                          conv_w=widen(m_conv_w), conv_b=m_conv_b, conv_ln_g=m_conv_ln_g, conv_ln_b=m_conv_ln_b,
                          norm_ffn=m_norm_ffn))
    sv_ = _pack_small(dict(norm_mix=v_norm_mix, q_norm=v_q_norm, k_norm=v_k_norm, sinks=v_sinks,
                           conv_w=widen(v_conv_w), conv_b=v_conv_b,
                           conv_ln_g=v_conv_ln_g, conv_ln_b=v_conv_ln_b, norm_ffn=v_norm_ffn))
    s_outs = [_unpack_small(o, shapes) for o in _adamw([small_parts], sw, sm, sv_, "adamw_small", SMALL_ROWS)]

    def narrow(a):
        a4 = a.reshape(DEPTH, CONV_WIDTH, N_DEV, CONV_CH // N_DEV)
        return lax.dynamic_slice(a4, (0, 0, me, 0), (DEPTH, CONV_WIDTH, 1, CONV_CH // N_DEV)).reshape(
            DEPTH, CONV_WIDTH, CONV_CH // N_DEV)

    big = {"w_in": u_in, "w_conv_out": u_co, "w_out": u_out, "w_gate_up": u_gu, "w_down": u_dn}
    order = ["norm_mix", "w_in", "q_norm", "k_norm", "sinks", "conv_w", "conv_b", "conv_ln_g", "conv_ln_b",
             "w_conv_out", "w_out", "norm_ffn", "w_gate_up", "w_down"]
    outs = [loss, grad_x]
    for kind in range(4):
        for name in order:
            if name in big:
                outs.append(big[name][kind])
            elif name == "conv_w":
                outs.append(narrow(s_outs[kind][name]))
            else:
                outs.append(s_outs[kind][name])
    return tuple(outs)
```

```python
import functools
import math

import jax
import jax.numpy as jnp
from jax import lax
from jax.experimental import pallas as pl
from jax.experimental.pallas import tpu as pltpu

F32 = jnp.float32
BF16 = jnp.bfloat16

D_MODEL = 1024
DEPTH = 4
N_HEADS = 16
N_KV_HEADS = 2
HEAD_DIM = 64
ROT_DIM = HEAD_DIM // 4
ROPE_THETA = 500000.0
BLOCK = 128
CONV_CH = D_MODEL // 2
CONV_WIDTH = 31
D_FF = 2816
EPS = 1e-6
Q_W = N_HEADS * HEAD_DIM
KV_W = N_KV_HEADS * HEAD_DIM
QKV_W = Q_W + 2 * KV_W
UUG_W = 2 * CONV_CH
GG_W = 2 * D_MODEL
IN_W = QKV_W + UUG_W + GG_W
N_DEV = 8

ADAM_LR = 0.001
ADAM_B1 = 0.9
ADAM_B2 = 0.999
ADAM_EPS = 1e-08
ADAM_WD = 0.01
ADAM_STEP = 10

LANES = 128
SUBLANES = 8
HALO = 32
VMEM_LIMIT = 56 * 1024 * 1024
NEG = -1e30


def _cp(sem=None):
    return pltpu.CompilerParams(dimension_semantics=sem, vmem_limit_bytes=VMEM_LIMIT)


def _sigmoid(z):
    return 1.0 / (1.0 + jnp.exp(-z))


def _rowgroup_sum(z):
    r, c = z.shape
    return jnp.sum(z.reshape(r // SUBLANES, SUBLANES, c), axis=0)


_DIMS = {"nn": (((1,), (0,)), ((), ())), "nt": (((1,), (1,)), ((), ())), "tn": (((0,), (0,)), ((), ()))}


def _mm(a, b, mode, *, out_dtype, name, tm, tn, tk, resid=None, after=None):
    if mode == "nn":
        (m, k), (k2, n) = a.shape, b.shape
        a_spec = pl.BlockSpec((tm, tk), lambda i, j, s: (i, s))
        b_spec = pl.BlockSpec((tk, tn), lambda i, j, s: (s, j))
    elif mode == "nt":
        (m, k), (n, k2) = a.shape, b.shape
        a_spec = pl.BlockSpec((tm, tk), lambda i, j, s: (i, s))
        b_spec = pl.BlockSpec((tn, tk), lambda i, j, s: (j, s))
    else:
        (k, m), (k2, n) = a.shape, b.shape
        a_spec = pl.BlockSpec((tk, tm), lambda i, j, s: (s, i))
        b_spec = pl.BlockSpec((tk, tn), lambda i, j, s: (s, j))
    assert k == k2 and m % tm == 0 and n % tn == 0 and k % tk == 0, (name, a.shape, b.shape, tm, tn, tk)
    nk = k // tk
    dims = _DIMS[mode]
    has_resid = resid is not None

    def body(*refs):
        a_ref, b_ref = refs[0], refs[1]
        r_ref = refs[2] if has_resid else None
        o_ref = refs[-1] if nk == 1 else refs[-2]
        part = lax.dot_general(a_ref[...].astype(BF16), b_ref[...].astype(BF16), dims, preferred_element_type=F32)

        def finish(acc):
            if has_resid:
                acc = acc + r_ref[...]
            o_ref[...] = acc.astype(out_dtype)

        if nk == 1:
            finish(part)
            return
        acc_ref = refs[-1]
        s = pl.program_id(2)

        @pl.when(s == 0)
        def _():
            acc_ref[...] = part

        @pl.when(s > 0)
        def _():
            acc_ref[...] += part

        @pl.when(s == nk - 1)
        def _():
            finish(acc_ref[...])

    in_specs = [a_spec, b_spec]
    args = [a, b]
    if has_resid:
        in_specs.append(pl.BlockSpec((tm, tn), lambda i, j, s: (i, j)))
        args.append(resid)
    if after is not None:
        in_specs.append(_ANY)
        args.append(after)
    return pl.pallas_call(
        body, name=name, grid=(m // tm, n // tn, nk),
        in_specs=in_specs, out_specs=pl.BlockSpec((tm, tn), lambda i, j, s: (i, j)),
        out_shape=jax.ShapeDtypeStruct((m, n), out_dtype),
        scratch_shapes=[] if nk == 1 else [pltpu.VMEM((tm, tn), F32)],
        compiler_params=_cp(("parallel", "parallel", "arbitrary")),
    )(*args)


def _rmsnorm_fwd(x, g, name, tm=512):
    t, d = x.shape

    def body(x_ref, g_ref, h_ref):
        xv = x_ref[...]
        r = lax.rsqrt(jnp.mean(xv * xv, axis=-1, keepdims=True) + EPS)
        h_ref[...] = ((xv * r) * g_ref[...]).astype(BF16)

    return pl.pallas_call(
        body, name=name, grid=(t // tm,),
        in_specs=[pl.BlockSpec((tm, d), lambda i: (i, 0)), pl.BlockSpec((1, d), lambda i: (0, 0))],
        out_specs=pl.BlockSpec((tm, d), lambda i: (i, 0)),
        out_shape=jax.ShapeDtypeStruct((t, d), BF16),
        compiler_params=_cp(("parallel",)),
    )(x, g)


def _rmsnorm_bwd(dh, x, g, resid, name, tm=512):
    t, d = x.shape

    def body(dh_ref, x_ref, g_ref, r_ref, dx_ref, dg_ref):
        @pl.when(pl.program_id(0) == 0)
        def _():
            dg_ref[...] = jnp.zeros_like(dg_ref)

        xv = x_ref[...]
        dhv = dh_ref[...]
        r = lax.rsqrt(jnp.mean(xv * xv, axis=-1, keepdims=True) + EPS)
        y = xv * r
        dg_ref[...] += _rowgroup_sum(dhv * y)
        dy = dhv * g_ref[...]
        dx_ref[...] = r_ref[...] + r * (dy - y * jnp.mean(dy * y, axis=-1, keepdims=True))

    row = pl.BlockSpec((tm, d), lambda i: (i, 0))
    return pl.pallas_call(
        body, name=name, grid=(t // tm,),
        in_specs=[row, row, pl.BlockSpec((1, d), lambda i: (0, 0)), row],
        out_specs=[row, pl.BlockSpec((SUBLANES, d), lambda i: (0, 0))],
        out_shape=[jax.ShapeDtypeStruct((t, d), F32), jax.ShapeDtypeStruct((SUBLANES, d), F32)],
        compiler_params=_cp(("arbitrary",)),
    )(dh, x, g, resid)


def _seg_sum(z, bd):
    hi = z.astype(BF16)
    lo = (z - hi.astype(F32)).astype(BF16)
    return jnp.dot(hi, bd, preferred_element_type=F32) + jnp.dot(lo, bd, preferred_element_type=F32)


def _partner(z, lane64):
    return jnp.where(lane64 < ROT_DIM // 2, pltpu.roll(z, LANES - ROT_DIM // 2, 1), pltpu.roll(z, ROT_DIM // 2, 1))


def _rope_tables(t):
    inv_freq = ROPE_THETA ** (-jnp.arange(0, ROT_DIM, 2, dtype=F32) / ROT_DIM)
    ang = jnp.arange(t, dtype=F32)[:, None] * inv_freq[None, :]
    cos, sin = jnp.cos(ang), jnp.sin(ang)
    c64 = jnp.concatenate([cos, cos, jnp.ones((t, HEAD_DIM - ROT_DIM), F32)], axis=1)
    s64 = jnp.concatenate([-sin, sin, jnp.zeros((t, HEAD_DIM - ROT_DIM), F32)], axis=1)
    return jnp.tile(c64, (1, 2)), jnp.tile(s64, (1, 2))


def _block_diag_ones():
    r = lax.broadcasted_iota(jnp.int32, (LANES, LANES), 0) // HEAD_DIM
    c = lax.broadcasted_iota(jnp.int32, (LANES, LANES), 1) // HEAD_DIM
    return (r == c).astype(BF16)


def _qk_prep_fwd(qkv, gq, gk, ctab, stab, bd, name, tm=512):
    t = qkv.shape[0]
    scale = HEAD_DIM ** -0.5
    n_qg = Q_W // LANES

    def body(q_ref, kv_ref, gq_ref, gk_ref, c_ref, s_ref, bd_ref, qs_ref, kd_ref, vd_ref):
        lane = lax.broadcasted_iota(jnp.int32, (tm, LANES), 1)
        lane64 = lane % HEAD_DIM
        lo_half = lane < HEAD_DIM
        cv, sv, bdv = c_ref[...], s_ref[...], bd_ref[...]

        def norm_rope(xg, g):
            r = lax.rsqrt(_seg_sum(xg * xg, bdv) * (1.0 / HEAD_DIM) + EPS)
            yn = (xg * r) * g
            return yn * cv + _partner(yn, lane64) * sv

        zero = jnp.zeros((BLOCK, LANES), BF16)
        lo_blk = lo_half[0:BLOCK]
        for c in range(n_qg):
            xg = q_ref[:, c * LANES:(c + 1) * LANES]
            qn = (norm_rope(xg, gq_ref[...]) * scale).astype(BF16)
            for b in range(tm // BLOCK):
                rows = qn[b * BLOCK:(b + 1) * BLOCK]
                qs_ref[b, 2 * c] = jnp.where(lo_blk, rows, zero)
                qs_ref[b, 2 * c + 1] = jnp.where(lo_blk, zero, rows)
        kk = norm_rope(kv_ref[:, 0:LANES], gk_ref[...])
        kr = pltpu.roll(kk, HEAD_DIM, 1)
        kd_ref[:, 0:LANES] = jnp.where(lo_half, kk, kr).astype(BF16)
        kd_ref[:, LANES:2 * LANES] = jnp.where(lo_half, kr, kk).astype(BF16)
        vv = kv_ref[:, LANES:2 * LANES]
        vr = pltpu.roll(vv, HEAD_DIM, 1)
        vd_ref[:, 0:LANES] = jnp.where(lo_half, vv, vr).astype(BF16)
        vd_ref[:, LANES:2 * LANES] = jnp.where(lo_half, vr, vv).astype(BF16)

    vec = pl.BlockSpec((1, LANES), lambda i: (0, 0))
    tab = pl.BlockSpec((tm, LANES), lambda i: (i, 0))
    return pl.pallas_call(
        body, name=name, grid=(t // tm,),
        in_specs=[pl.BlockSpec((tm, Q_W), lambda i: (i, 0)),
                  pl.BlockSpec((tm, 2 * KV_W), lambda i: (i, Q_W // (2 * KV_W))),
                  vec, vec, tab, tab, pl.BlockSpec((LANES, LANES), lambda i: (0, 0))],
        out_specs=[pl.BlockSpec((tm // BLOCK, N_HEADS, BLOCK, LANES), lambda i: (i, 0, 0, 0)),
                   pl.BlockSpec((tm, 2 * LANES), lambda i: (i, 0)),
                   pl.BlockSpec((tm, 2 * LANES), lambda i: (i, 0))],
        out_shape=[jax.ShapeDtypeStruct((t // BLOCK, N_HEADS, BLOCK, LANES), BF16),
                   jax.ShapeDtypeStruct((t, 2 * LANES), BF16), jax.ShapeDtypeStruct((t, 2 * LANES), BF16)],
        compiler_params=_cp(("parallel",)),
    )(qkv, qkv, gq, gk, ctab, stab, bd)


def _qk_prep_bwd(dqs, dkd, dvd, qkv, gq, gk, ctab, stab, bd, name, tm=512):
    t = qkv.shape[0]
    scale = HEAD_DIM ** -0.5
    n_qg = Q_W // LANES

    def body(dqs_ref, dkd_ref, dvd_ref, q_ref, kv_ref, gq_ref, gk_ref, c_ref, s_ref, bd_ref,
             dqkv_ref, dgq_ref, dgk_ref):
        @pl.when(pl.program_id(0) == 0)
        def _():
            dgq_ref[...] = jnp.zeros_like(dgq_ref)
            dgk_ref[...] = jnp.zeros_like(dgk_ref)

        lane = lax.broadcasted_iota(jnp.int32, (tm, LANES), 1)
        lane64 = lane % HEAD_DIM
        lo_half = lane < HEAD_DIM
        cv, sv, bdv = c_ref[...], s_ref[...], bd_ref[...]

        def bwd(xg, g, dout):
            r = lax.rsqrt(_seg_sum(xg * xg, bdv) * (1.0 / HEAD_DIM) + EPS)
            y = xg * r
            dyn = dout * cv + jnp.where(lane64 < ROT_DIM, _partner(dout * sv, lane64), 0.0)
            dy = dyn * g
            dx = r * (dy - y * (_seg_sum(dy * y, bdv) * (1.0 / HEAD_DIM)))
            return dx, _rowgroup_sum(dyn * y)

        dgq = jnp.zeros((SUBLANES, LANES), F32)
        for c in range(n_qg):
            sl = slice(c * LANES, (c + 1) * LANES)
            dx, dg = bwd(q_ref[:, sl], gq_ref[...], dqs_ref[:, sl] * scale)
            dqkv_ref[:, sl] = dx.astype(BF16)
            dgq = dgq + dg
        dgq_ref[...] += dgq
        dk = jnp.where(lo_half, dkd_ref[:, 0:LANES], dkd_ref[:, LANES:2 * LANES])
        dx, dg = bwd(kv_ref[:, 0:LANES], gk_ref[...], dk)
        dqkv_ref[:, Q_W:Q_W + LANES] = dx.astype(BF16)
        dgk_ref[...] += dg
        dv = jnp.where(lo_half, dvd_ref[:, 0:LANES], dvd_ref[:, LANES:2 * LANES])
        dqkv_ref[:, Q_W + LANES:Q_W + 2 * LANES] = dv.astype(BF16)

    vec = pl.BlockSpec((1, LANES), lambda i: (0, 0))
    tab = pl.BlockSpec((tm, LANES), lambda i: (i, 0))
    wide = pl.BlockSpec((tm, 2 * LANES), lambda i: (i, 0))
    acc = pl.BlockSpec((SUBLANES, LANES), lambda i: (0, 0))
    return pl.pallas_call(
        body, name=name, grid=(t // tm,),
        in_specs=[pl.BlockSpec((tm, Q_W), lambda i: (i, 0)), wide, wide,
                  pl.BlockSpec((tm, Q_W), lambda i: (i, 0)),
                  pl.BlockSpec((tm, 2 * KV_W), lambda i: (i, Q_W // (2 * KV_W))),
                  vec, vec, tab, tab, pl.BlockSpec((LANES, LANES), lambda i: (0, 0))],
        out_specs=[pl.BlockSpec((tm, QKV_W), lambda i: (i, 0)), acc, acc],
        out_shape=[jax.ShapeDtypeStruct((t, QKV_W), BF16), jax.ShapeDtypeStruct((SUBLANES, LANES), F32),
                   jax.ShapeDtypeStruct((SUBLANES, LANES), F32)],
        compiler_params=_cp(("arbitrary",)),
    )(dqs, dkd, dvd, qkv, qkv, gq, gk, ctab, stab, bd)


GROUP = N_HEADS // N_KV_HEADS
GROUP_ROWS = GROUP * BLOCK


def _attn_masks():
    row = lax.broadcasted_iota(jnp.int32, (BLOCK, BLOCK), 0)
    col = lax.broadcasted_iota(jnp.int32, (BLOCK, BLOCK), 1)
    return col <= row, col < HEAD_DIM


def _window_softmax(s_c, s_p, is_cur, has_prev, sink):
    s = jnp.where(is_cur, s_c, jnp.where(has_prev, s_p, NEG))
    m = jnp.maximum(jnp.max(s, axis=-1, keepdims=True), sink)
    e = jnp.exp(s - m)
    e_s = jnp.exp(sink - m)
    inv = 1.0 / (jnp.sum(e, axis=-1, keepdims=True) + e_s)
    return e * inv, e_s * inv


def _attn_fwd(qs, kd, vd, sinks, name):
    nb = qs.shape[0]
    t = nb * BLOCK

    def body(sink_ref, q_ref, kc_ref, kp_ref, vc_ref, vp_ref, o_ref, pc_scr, pp_scr):
        has_prev = pl.program_id(0) > 0
        is_cur, lo_half = _attn_masks()
        for j in range(N_KV_HEADS):
            ks = slice(j * LANES, (j + 1) * LANES)
            qg = q_ref[0, j * GROUP:(j + 1) * GROUP].reshape(GROUP_ROWS, LANES)
            s_c = lax.dot_general(qg, kc_ref[:, ks], _DIMS["nt"], preferred_element_type=F32)
            s_p = lax.dot_general(qg, kp_ref[:, ks], _DIMS["nt"], preferred_element_type=F32)
            for g in range(GROUP):
                rs = slice(g * BLOCK, (g + 1) * BLOCK)
                p, _ = _window_softmax(s_c[rs], s_p[rs], is_cur, has_prev, sink_ref[j * GROUP + g])
                pc_scr[rs, :] = jnp.where(is_cur, p, 0.0).astype(BF16)
                pp_scr[rs, :] = jnp.where(is_cur, 0.0, p).astype(BF16)
            o2 = (jnp.dot(pc_scr[...], vc_ref[:, ks], preferred_element_type=F32)
                  + jnp.dot(pp_scr[...], vp_ref[:, ks], preferred_element_type=F32))
            for pp in range(GROUP // 2):
                c0 = (j * (GROUP // 2) + pp) * LANES
                o_ref[:, c0:c0 + LANES] = jnp.where(lo_half, o2[2 * pp * BLOCK:(2 * pp + 1) * BLOCK],
                                                    o2[(2 * pp + 1) * BLOCK:(2 * pp + 2) * BLOCK])

    cur = lambda i: (i, 0)
    prev = lambda i: (jnp.maximum(i - 1, 0), 0)
    kvs = (BLOCK, 2 * LANES)
    return pl.pallas_call(
        body, name=name, grid=(nb,),
        in_specs=[pl.BlockSpec(memory_space=pltpu.SMEM),
                  pl.BlockSpec((1, N_HEADS, BLOCK, LANES), lambda i: (i, 0, 0, 0)),
                  pl.BlockSpec(kvs, cur), pl.BlockSpec(kvs, prev), pl.BlockSpec(kvs, cur), pl.BlockSpec(kvs, prev)],
        out_specs=pl.BlockSpec((BLOCK, Q_W), cur),
        out_shape=jax.ShapeDtypeStruct((t, Q_W), F32),
        scratch_shapes=[pltpu.VMEM((GROUP_ROWS, LANES), BF16), pltpu.VMEM((GROUP_ROWS, LANES), BF16)],
        compiler_params=_cp(("parallel",)),
    )(sinks, qs, kd, kd, vd, vd)


def _attn_bwd(qs, kd, vd, sinks, do, name):
    nb = qs.shape[0]
    t = nb * BLOCK

    def body(sink_ref, q_ref, do_ref, kc_ref, kp_ref, vc_ref, vp_ref,
             dq_ref, dk_ref, dv_ref, dsink_ref,
             carry_k, carry_v, dsink_acc, do_scr, pc_scr, pp_scr, dsc_scr, dsp_scr):
        i = pl.program_id(0)

        @pl.when(i == 0)
        def _():
            carry_k[...] = jnp.zeros_like(carry_k)
            carry_v[...] = jnp.zeros_like(carry_v)
            dsink_acc[...] = jnp.zeros_like(dsink_acc)

        @pl.when(i < nb)
        def _():
            has_prev = i > 0
            is_cur, lo_half = _attn_masks()
            srow = lax.broadcasted_iota(jnp.int32, (SUBLANES, LANES), 0)
            scol = lax.broadcasted_iota(jnp.int32, (SUBLANES, LANES), 1)
            dsink = jnp.zeros((SUBLANES, LANES), F32)
            for j in range(N_KV_HEADS):
                ks = slice(j * LANES, (j + 1) * LANES)
                kc, kp, vc, vp = kc_ref[:, ks], kp_ref[:, ks], vc_ref[:, ks], vp_ref[:, ks]
                qg = q_ref[0, j * GROUP:(j + 1) * GROUP].reshape(GROUP_ROWS, LANES)
                for pp in range(GROUP // 2):
                    c0 = (j * (GROUP // 2) + pp) * LANES
                    dop = do_ref[:, c0:c0 + LANES]
                    do_scr[2 * pp * BLOCK:(2 * pp + 1) * BLOCK, :] = jnp.where(lo_half, dop, 0.0).astype(BF16)
                    do_scr[(2 * pp + 1) * BLOCK:(2 * pp + 2) * BLOCK, :] = jnp.where(lo_half, 0.0, dop).astype(BF16)
                dog = do_scr[...]
                s_c = lax.dot_general(qg, kc, _DIMS["nt"], preferred_element_type=F32)
                s_p = lax.dot_general(qg, kp, _DIMS["nt"], preferred_element_type=F32)
                dp_c = lax.dot_general(dog, vc, _DIMS["nt"], preferred_element_type=F32)
                dp_p = lax.dot_general(dog, vp, _DIMS["nt"], preferred_element_type=F32)
                for g in range(GROUP):
                    rs = slice(g * BLOCK, (g + 1) * BLOCK)
                    h = j * GROUP + g
                    p, p_s = _window_softmax(s_c[rs], s_p[rs], is_cur, has_prev, sink_ref[h])
                    dp = jnp.where(is_cur, dp_c[rs], dp_p[rs])
                    delta = jnp.sum(p * dp, axis=-1, keepdims=True)
                    ds = p * (dp - delta)
                    dsv = -jnp.sum(p_s * delta, axis=0, keepdims=True)
                    dsink = dsink + jnp.where(jnp.logical_and(srow == 0, scol == h), dsv, 0.0)
                    pc_scr[rs, :] = jnp.where(is_cur, p, 0.0).astype(BF16)
                    pp_scr[rs, :] = jnp.where(is_cur, 0.0, p).astype(BF16)
                    dsc_scr[rs, :] = jnp.where(is_cur, ds, 0.0).astype(BF16)
                    dsp_scr[rs, :] = jnp.where(is_cur, 0.0, ds).astype(BF16)
                dsc, dsp = dsc_scr[...], dsp_scr[...]
                dq2 = jnp.dot(dsc, kc, preferred_element_type=F32) + jnp.dot(dsp, kp, preferred_element_type=F32)
                for pp in range(GROUP // 2):
                    c0 = (j * (GROUP // 2) + pp) * LANES
                    dq_ref[:, c0:c0 + LANES] = jnp.where(lo_half, dq2[2 * pp * BLOCK:(2 * pp + 1) * BLOCK],
                                                         dq2[(2 * pp + 1) * BLOCK:(2 * pp + 2) * BLOCK])
                dk_c = lax.dot_general(dsc, qg, _DIMS["tn"], preferred_element_type=F32)
                dk_p = lax.dot_general(dsp, qg, _DIMS["tn"], preferred_element_type=F32)
                dv_c = lax.dot_general(pc_scr[...], dog, _DIMS["tn"], preferred_element_type=F32)
                dv_p = lax.dot_general(pp_scr[...], dog, _DIMS["tn"], preferred_element_type=F32)
                dk_ref[:, ks] = carry_k[:, ks] + dk_p + pltpu.roll(dk_p, HEAD_DIM, 1)
                dv_ref[:, ks] = carry_v[:, ks] + dv_p + pltpu.roll(dv_p, HEAD_DIM, 1)
                carry_k[:, ks] = dk_c + pltpu.roll(dk_c, HEAD_DIM, 1)
                carry_v[:, ks] = dv_c + pltpu.roll(dv_c, HEAD_DIM, 1)
            dsink_acc[...] += dsink

        @pl.when(i == nb)
        def _():
            dk_ref[...] = carry_k[...]
            dv_ref[...] = carry_v[...]
            dsink_ref[...] = dsink_acc[...]

    last = nb - 1
    cur = lambda i: (jnp.minimum(i, last), 0)
    prev = lambda i: (jnp.clip(i - 1, 0, last), 0)
    kvs = (BLOCK, 2 * LANES)
    stk = pltpu.VMEM((GROUP_ROWS, LANES), BF16)
    return pl.pallas_call(
        body, name=name, grid=(nb + 1,),
        in_specs=[pl.BlockSpec(memory_space=pltpu.SMEM),
                  pl.BlockSpec((1, N_HEADS, BLOCK, LANES), lambda i: (jnp.minimum(i, last), 0, 0, 0)),
                  pl.BlockSpec((BLOCK, Q_W), cur),
                  pl.BlockSpec(kvs, cur), pl.BlockSpec(kvs, prev), pl.BlockSpec(kvs, cur), pl.BlockSpec(kvs, prev)],
        out_specs=[pl.BlockSpec((BLOCK, Q_W), cur), pl.BlockSpec(kvs, prev), pl.BlockSpec(kvs, prev),
                   pl.BlockSpec((SUBLANES, LANES), lambda i: (0, 0))],
        out_shape=[jax.ShapeDtypeStruct((t, Q_W), F32), jax.ShapeDtypeStruct((t, 2 * LANES), F32),
                   jax.ShapeDtypeStruct((t, 2 * LANES), F32), jax.ShapeDtypeStruct((SUBLANES, LANES), F32)],
        scratch_shapes=[pltpu.VMEM(kvs, F32), pltpu.VMEM(kvs, F32), pltpu.VMEM((SUBLANES, LANES), F32),
                        stk, stk, stk, stk, stk],
        compiler_params=_cp(("arbitrary",)),
    )(sinks, qs, do, kd, kd, vd, vd)


CONV_CHUNK = 64


def _conv_fwd(uug, w32, cb, lg, lb, name, tm=512):
    t = uug.shape[0]
    hb = tm // HALO

    def body(m_ref, h_ref, w_ref, cb_ref, lg_ref, lb_ref, y0_ref, y2_ref, a_ext):
        i = pl.program_id(0)
        a_ext[HALO:, :] = m_ref[:, 0:CONV_CH] * _sigmoid(m_ref[:, CONV_CH:])
        ah = h_ref[:, 0:CONV_CH] * _sigmoid(h_ref[:, CONV_CH:])
        a_ext[0:HALO, :] = jnp.where(i > 0, ah, 0.0)
        off = HALO - (CONV_WIDTH - 1)
        for c in range(tm // CONV_CHUNK):
            r0 = c * CONV_CHUNK
            acc = jnp.zeros((CONV_CHUNK, CONV_CH), F32)
            for k in range(CONV_WIDTH):
                acc = acc + w_ref[k:k + 1, :] * a_ext[r0 + off + k:r0 + off + k + CONV_CHUNK, :]
            y0 = acc + cb_ref[...]
            y0_ref[r0:r0 + CONV_CHUNK, :] = y0
            mu = jnp.mean(y0, axis=-1, keepdims=True)
            dlt = y0 - mu
            rstd = lax.rsqrt(jnp.mean(dlt * dlt, axis=-1, keepdims=True) + EPS)
            y1 = (dlt * rstd) * lg_ref[...] + lb_ref[...]
            y2_ref[r0:r0 + CONV_CHUNK, :] = (y1 * _sigmoid(y1)).astype(BF16)

    vec = pl.BlockSpec((1, CONV_CH), lambda i: (0, 0))
    return pl.pallas_call(
        body, name=name, grid=(t // tm,),
        in_specs=[pl.BlockSpec((tm, UUG_W), lambda i: (i, 0)),
                  pl.BlockSpec((HALO, UUG_W), lambda i: (jnp.maximum(i * hb - 1, 0), 0)),
                  pl.BlockSpec((HALO, CONV_CH), lambda i: (0, 0)), vec, vec, vec],
        out_specs=[pl.BlockSpec((tm, CONV_CH), lambda i: (i, 0)), pl.BlockSpec((tm, CONV_CH), lambda i: (i, 0))],
        out_shape=[jax.ShapeDtypeStruct((t, CONV_CH), F32), jax.ShapeDtypeStruct((t, CONV_CH), BF16)],
        scratch_shapes=[pltpu.VMEM((tm + HALO, CONV_CH), F32)],
        compiler_params=_cp(("parallel",)),
    )(uug, uug, w32, cb, lg, lb)


def _conv_bwd_ln(dy2, y0, lg, lb, name, tm=512):
    t = y0.shape[0]

    def body(dy2_ref, y0_ref, lg_ref, lb_ref, dy0_ref, dlg_ref, dlb_ref, dcb_ref):
        @pl.when(pl.program_id(0) == 0)
        def _():
            dlg_ref[...] = jnp.zeros_like(dlg_ref)
            dlb_ref[...] = jnp.zeros_like(dlb_ref)
            dcb_ref[...] = jnp.zeros_like(dcb_ref)

        y0 = y0_ref[...]
        mu = jnp.mean(y0, axis=-1, keepdims=True)
        dlt = y0 - mu
        rstd = lax.rsqrt(jnp.mean(dlt * dlt, axis=-1, keepdims=True) + EPS)
        yh = dlt * rstd
        y1 = yh * lg_ref[...] + lb_ref[...]
        sg = _sigmoid(y1)
        dy1 = dy2_ref[...] * (sg * (1.0 + y1 * (1.0 - sg)))
        dlg_ref[...] += _rowgroup_sum(dy1 * yh)
        dlb_ref[...] += _rowgroup_sum(dy1)
        dyh = dy1 * lg_ref[...]
        dy0 = rstd * (dyh - jnp.mean(dyh, axis=-1, keepdims=True)
                      - yh * jnp.mean(dyh * yh, axis=-1, keepdims=True))
        dcb_ref[...] += _rowgroup_sum(dy0)
        dy0_ref[...] = dy0

    row = pl.BlockSpec((tm, CONV_CH), lambda i: (i, 0))
    vec = pl.BlockSpec((1, CONV_CH), lambda i: (0, 0))
    acc = pl.BlockSpec((SUBLANES, CONV_CH), lambda i: (0, 0))
    accs = jax.ShapeDtypeStruct((SUBLANES, CONV_CH), F32)
    return pl.pallas_call(
        body, name=name, grid=(t // tm,),
        in_specs=[row, row, vec, vec], out_specs=[row, acc, acc, acc],
        out_shape=[jax.ShapeDtypeStruct((t, CONV_CH), F32), accs, accs, accs],
        compiler_params=_cp(("arbitrary",)),
    )(dy2, y0, lg, lb)


def _conv_bwd_taps(dy0, uug, w32, name, tm=512):
    t = uug.shape[0]
    hb = tm // HALO
    n_halo_blocks = t // HALO
    nt = t // tm

    def body(dm_ref, dn_ref, m_ref, h_ref, w_ref, duug_ref, dw_ref, a_ext, d_ext):
        i = pl.program_id(0)

        @pl.when(i == 0)
        def _():
            dw_ref[...] = jnp.zeros_like(dw_ref)

        u = m_ref[:, 0:CONV_CH]
        sg = _sigmoid(m_ref[:, CONV_CH:])
        a_ext[HALO:, :] = u * sg
        ah = h_ref[:, 0:CONV_CH] * _sigmoid(h_ref[:, CONV_CH:])
        a_ext[0:HALO, :] = jnp.where(i > 0, ah, 0.0)
        d_ext[0:tm, :] = dm_ref[...]
        d_ext[tm:, :] = jnp.where(i < nt - 1, dn_ref[...], 0.0)
        off = HALO - (CONV_WIDTH - 1)
        for c in range(tm // CONV_CHUNK):
            r0 = c * CONV_CHUNK
            da = jnp.zeros((CONV_CHUNK, CONV_CH), F32)
            for k in range(CONV_WIDTH):
                sh = CONV_WIDTH - 1 - k
                da = da + w_ref[k:k + 1, :] * d_ext[r0 + sh:r0 + sh + CONV_CHUNK, :]
            uc = u[r0:r0 + CONV_CHUNK, :]
            sc = sg[r0:r0 + CONV_CHUNK, :]
            duug_ref[r0:r0 + CONV_CHUNK, 0:CONV_CH] = (da * sc).astype(BF16)
            duug_ref[r0:r0 + CONV_CHUNK, CONV_CH:] = (da * uc * sc * (1.0 - sc)).astype(BF16)
            dch = d_ext[r0:r0 + CONV_CHUNK, :]
            for k in range(CONV_WIDTH):
                prod = dch * a_ext[r0 + off + k:r0 + off + k + CONV_CHUNK, :]
                dw_ref[k * SUBLANES:(k + 1) * SUBLANES, :] += _rowgroup_sum(prod)

    return pl.pallas_call(
        body, name=name, grid=(nt,),
        in_specs=[pl.BlockSpec((tm, CONV_CH), lambda i: (i, 0)),
                  pl.BlockSpec((HALO, CONV_CH), lambda i: (jnp.minimum((i + 1) * hb, n_halo_blocks - 1), 0)),
                  pl.BlockSpec((tm, UUG_W), lambda i: (i, 0)),
                  pl.BlockSpec((HALO, UUG_W), lambda i: (jnp.maximum(i * hb - 1, 0), 0)),
                  pl.BlockSpec((HALO, CONV_CH), lambda i: (0, 0))],
        out_specs=[pl.BlockSpec((tm, UUG_W), lambda i: (i, 0)),
                   pl.BlockSpec((CONV_WIDTH * SUBLANES, CONV_CH), lambda i: (0, 0))],
        out_shape=[jax.ShapeDtypeStruct((t, UUG_W), BF16),
                   jax.ShapeDtypeStruct((CONV_WIDTH * SUBLANES, CONV_CH), F32)],
        scratch_shapes=[pltpu.VMEM((tm + HALO, CONV_CH), F32), pltpu.VMEM((tm + HALO, CONV_CH), F32)],
        compiler_params=_cp(("arbitrary",)),
    )(dy0, dy0, uug, uug, w32)


def _merge_fwd(a_out, c_out, gg, name, tm=512):
    t, d = a_out.shape

    def body(a_ref, c_ref, g_ref, o_ref):
        o_ref[...] = (_sigmoid(g_ref[:, 0:d]) * a_ref[...] + _sigmoid(g_ref[:, d:]) * c_ref[...]).astype(BF16)

    row = pl.BlockSpec((tm, d), lambda i: (i, 0))
    return pl.pallas_call(
        body, name=name, grid=(t // tm,),
        in_specs=[row, row, pl.BlockSpec((tm, 2 * d), lambda i: (i, 0))], out_specs=row,
        out_shape=jax.ShapeDtypeStruct((t, d), BF16), compiler_params=_cp(("parallel",)),
    )(a_out, c_out, gg)


def _merge_bwd(dm, a_out, c_out, gg, name, tm=512):
    t, d = a_out.shape

    def body(dm_ref, a_ref, c_ref, g_ref, da_ref, dc_ref, dg_ref):
        dmv = dm_ref[...]
        sa = _sigmoid(g_ref[:, 0:d])
        sb = _sigmoid(g_ref[:, d:])
        da_ref[...] = dmv * sa
        dc_ref[...] = (dmv * sb).astype(BF16)
        dg_ref[:, 0:d] = (dmv * a_ref[...] * sa * (1.0 - sa)).astype(BF16)
        dg_ref[:, d:] = (dmv * c_ref[...] * sb * (1.0 - sb)).astype(BF16)

    row = pl.BlockSpec((tm, d), lambda i: (i, 0))
    wide = pl.BlockSpec((tm, 2 * d), lambda i: (i, 0))
    return pl.pallas_call(
        body, name=name, grid=(t // tm,),
        in_specs=[row, row, row, wide], out_specs=[row, row, wide],
        out_shape=[jax.ShapeDtypeStruct((t, d), F32), jax.ShapeDtypeStruct((t, d), BF16),
                   jax.ShapeDtypeStruct((t, 2 * d), BF16)],
        compiler_params=_cp(("parallel",)),
    )(dm, a_out, c_out, gg)


FF_TN = 1408


def _ffn_up_fwd(h2, wgu, name, tm=512):
    t, d = h2.shape
    nj = D_FF // FF_TN

    def body(h_ref, wg_ref, wu_ref, o_ref):
        hv = h_ref[...]
        g = jnp.dot(hv, wg_ref[...], preferred_element_type=F32)
        u = jnp.dot(hv, wu_ref[...], preferred_element_type=F32)
        o_ref[...] = ((g * _sigmoid(g)) * u).astype(BF16)

    return pl.pallas_call(
        body, name=name, grid=(nj, t // tm),
        in_specs=[pl.BlockSpec((tm, d), lambda j, i: (i, 0)),
                  pl.BlockSpec((d, FF_TN), lambda j, i: (0, j)),
                  pl.BlockSpec((d, FF_TN), lambda j, i: (0, j + nj))],
        out_specs=pl.BlockSpec((tm, FF_TN), lambda j, i: (i, j)),
        out_shape=jax.ShapeDtypeStruct((t, D_FF), BF16),
        compiler_params=_cp(("parallel", "parallel")),
    )(h2, wgu, wgu)


def _ffn_bwd_mid(h2, dx2, wgu, wd, name, tm=256, after=None):
    t, d = h2.shape
    nj = D_FF // FF_TN

    def body(*refs):
        h_ref, dx_ref, wg_ref, wu_ref, wd_ref = refs[:5]
        dg_ref, du_ref = refs[-2:]
        hv = h_ref[...]
        g = jnp.dot(hv, wg_ref[...], preferred_element_type=F32)
        u = jnp.dot(hv, wu_ref[...], preferred_element_type=F32)
        dact = lax.dot_general(dx_ref[...].astype(BF16), wd_ref[...], _DIMS["nt"], preferred_element_type=F32)
        sg = _sigmoid(g)
        silu = g * sg
        dg_ref[...] = (dact * u * (sg * (1.0 + g * (1.0 - sg)))).astype(BF16)
        du_ref[...] = (dact * silu).astype(BF16)

    return pl.pallas_call(
        body, name=name, grid=(nj, t // tm),
        in_specs=[pl.BlockSpec((tm, d), lambda j, i: (i, 0)), pl.BlockSpec((tm, d), lambda j, i: (i, 0)),
                  pl.BlockSpec((d, FF_TN), lambda j, i: (0, j)),
                  pl.BlockSpec((d, FF_TN), lambda j, i: (0, j + nj)),
                  pl.BlockSpec((FF_TN, d), lambda j, i: (j, 0))] + ([] if after is None else [_ANY]),
        out_specs=[pl.BlockSpec((tm, FF_TN), lambda j, i: (i, j)), pl.BlockSpec((tm, FF_TN), lambda j, i: (i, j))],
        out_shape=[jax.ShapeDtypeStruct((t, D_FF), BF16), jax.ShapeDtypeStruct((t, D_FF), BF16)],
        compiler_params=_cp(("parallel", "parallel")),
    )(h2, dx2, wgu, wgu, wd, *([] if after is None else [after]))


def _loss_head(y, target, name, tm=512):
    t, d = y.shape

    def body(y_ref, t_ref, dy_ref, loss_ref):
        @pl.when(pl.program_id(0) == 0)
        def _():
            loss_ref[...] = jnp.zeros_like(loss_ref)

        e = y_ref[...] - t_ref[...]
        dy_ref[...] = e * (1.0 / d)
        s = _rowgroup_sum(e * e)
        acc = s[:, 0:LANES]
        for c in range(1, d // LANES):
            acc = acc + s[:, c * LANES:(c + 1) * LANES]
        loss_ref[...] += acc * (0.5 / d)

    row = pl.BlockSpec((tm, d), lambda i: (i, 0))
    return pl.pallas_call(
        body, name=name, grid=(t // tm,),
        in_specs=[row, row], out_specs=[row, pl.BlockSpec((SUBLANES, LANES), lambda i: (0, 0))],
        out_shape=[jax.ShapeDtypeStruct((t, d), F32), jax.ShapeDtypeStruct((SUBLANES, LANES), F32)],
        compiler_params=_cp(("arbitrary",)),
    )(y, target)


def _exchange(arrays, scatter, name):
    n = len(arrays)

    def body(*refs):
        ins, outs = refs[:n], refs[n:2 * n]
        send_sems, recv_sems, local_sems = refs[2 * n:]
        x, y, c = lax.axis_index("x"), lax.axis_index("y"), lax.axis_index("c")
        me = 4 * x + 2 * y + c

        def peer(k):
            px, py, pc = x ^ ((k >> 2) & 1), y ^ ((k >> 1) & 1), c ^ (k & 1)
            return (px, py, pc), 4 * px + 2 * py + pc

        def src(a, dst_id):
            return ins[a].at[dst_id] if scatter else ins[a]

        locals_ = [pltpu.make_async_copy(src(a, me), outs[a].at[me], local_sems.at[a]) for a in range(n)]
        for cp in locals_:
            cp.start()
        sends = []
        for k in range(1, N_DEV):
            dev, pid = peer(k)
            for a in range(n):
                sends.append(pltpu.make_async_remote_copy(
                    src_ref=src(a, pid), dst_ref=outs[a].at[me],
                    send_sem=send_sems.at[a, k], recv_sem=recv_sems.at[a, k],
                    device_id=dev, device_id_type=pl.DeviceIdType.MESH))
        for cp in sends:
            cp.start()
        for k in range(1, N_DEV):
            dev, pid = peer(k)
            for a in range(n):
                pltpu.make_async_remote_copy(
                    src_ref=src(a, pid), dst_ref=outs[a].at[pid],
                    send_sem=send_sems.at[a, k], recv_sem=recv_sems.at[a, k],
                    device_id=dev, device_id_type=pl.DeviceIdType.MESH).wait_recv()
        for cp in sends:
            cp.wait_send()
        for cp in locals_:
            cp.wait()

    def out_shape(a):
        return jax.ShapeDtypeStruct(a.shape if scatter else (N_DEV,) + a.shape, a.dtype)

    anyspec = pl.BlockSpec(memory_space=pl.ANY)
    return pl.pallas_call(
        body, name=name,
        in_specs=[anyspec] * n, out_specs=[anyspec] * n,
        out_shape=[out_shape(a) for a in arrays],
        scratch_shapes=[pltpu.SemaphoreType.DMA((n, N_DEV)), pltpu.SemaphoreType.DMA((n, N_DEV)),
                        pltpu.SemaphoreType.DMA((n,))],
    )(*arrays)


_HBM = pl.BlockSpec(memory_space=pltpu.HBM)
_SEM = pl.BlockSpec(memory_space=pltpu.SEMAPHORE)
_ANY = pl.BlockSpec(memory_space=pl.ANY)
_EFFECT = pltpu.SideEffectType.DATAFLOW_SIDE_EFFECTING


def _mesh_peer(k):
    x, y, c = lax.axis_index("x"), lax.axis_index("y"), lax.axis_index("c")
    px, py, pc = x ^ ((k >> 2) & 1), y ^ ((k >> 1) & 1), c ^ (k & 1)
    return (px, py, pc), 4 * px + 2 * py + pc


def _exchange_start(arrays, scatter, name, after):
    n = len(arrays)
    lands = [lax.empty(a.shape if scatter else (N_DEV,) + a.shape, a.dtype) for a in arrays]

    def body(*refs):
        ins, land_refs = refs[:n], refs[n:2 * n]
        send_sems, recv_sems = refs[2 * n + 1], refs[2 * n + 2]
        token = refs[-1]
        _, me = _mesh_peer(0)
        for k in range(1, N_DEV):
            dev, pid = _mesh_peer(k)
            for a in range(n):
                pltpu.make_async_remote_copy(
                    src_ref=ins[a].at[pid] if scatter else ins[a], dst_ref=land_refs[a].at[me],
                    send_sem=send_sems.at[a * N_DEV + k], recv_sem=recv_sems.at[a * N_DEV + k],
                    device_id=dev, device_id_type=pl.DeviceIdType.MESH).start()
        token[...] = jnp.zeros_like(token)

    hbm_in = [pltpu.with_memory_space_constraint(a, pltpu.HBM) for a in list(arrays) + lands]
    outs = pl.pallas_call(
        body, name=name,
        in_specs=[_HBM] * (2 * n) + [_ANY],
        out_specs=[_SEM, _SEM] + [_HBM] * (2 * n) + [pl.BlockSpec(memory_space=pltpu.VMEM)],
        out_shape=[pltpu.SemaphoreType.DMA((n * N_DEV,)), pltpu.SemaphoreType.DMA((n * N_DEV,))]
        + [pltpu.HBM(a.shape, a.dtype) for a in hbm_in]
        + [jax.ShapeDtypeStruct((SUBLANES, LANES), F32)],
        input_output_aliases={i: 2 + i for i in range(2 * n)},
        compiler_params=pltpu.CompilerParams(has_side_effects=_EFFECT),
    )(*hbm_in, after)
    return outs[0], outs[1], outs[2:2 + n], outs[2 + n:2 + 2 * n], outs[-1]


def _exchange_wait(started, scatter, name, after):
    send_sems, recv_sems, srcs, lands, _ = started
    n = len(srcs)

    def body(*refs):
        ins, land_refs = refs[:n], refs[n:2 * n]
        send_sems, recv_sems = refs[2 * n], refs[2 * n + 1]
        copies = []
        for k in range(1, N_DEV):
            dev, pid = _mesh_peer(k)
            for a in range(n):
                copies.append(pltpu.make_async_remote_copy(
                    src_ref=ins[a].at[pid] if scatter else ins[a], dst_ref=land_refs[a].at[pid],
                    send_sem=send_sems.at[a * N_DEV + k], recv_sem=recv_sems.at[a * N_DEV + k],
                    device_id=dev, device_id_type=pl.DeviceIdType.MESH))
        for cp in copies:
            cp.wait_recv()
        for cp in copies:
            cp.wait_send()

    outs = pl.pallas_call(
        body, name=name,
        in_specs=[_HBM] * (2 * n) + [_SEM, _SEM, _ANY],
        out_specs=[_HBM] * (2 * n),
        out_shape=[pltpu.HBM(a.shape, a.dtype) for a in list(srcs) + list(lands)],
        input_output_aliases={i: i for i in range(2 * n)},
        compiler_params=pltpu.CompilerParams(has_side_effects=_EFFECT),
    )(*srcs, *lands, send_sems, recv_sems, after)
    me = 4 * lax.axis_index("x") + 2 * lax.axis_index("y") + lax.axis_index("c")
    filled = []
    for src, land in zip(outs[:n], outs[n:]):
        own = lax.dynamic_index_in_dim(src, me, 0, keepdims=True) if scatter else src[None]
        filled.append(lax.dynamic_update_slice(land, own, (me,) + (0,) * (land.ndim - 1)))
    return filled


def _adamw(parts, w, m, v, name, tr):
    nl = len(parts)
    r, c = parts[0].shape[1:]
    assert w.shape == (nl * r, c) and r % tr == 0, (name, w.shape, r, tr)
    nt = r // tr
    c1 = 1.0 - ADAM_B1 ** ADAM_STEP
    c2 = 1.0 - ADAM_B2 ** ADAM_STEP

    def body(*refs):
        p_refs = refs[:nl]
        w_ref, m_ref, v_ref, g_ref, d_ref, nm_ref, nv_ref = refs[nl:]
        layer = pl.program_id(0)
        for k in range(nl):
            @pl.when(layer == k)
            def _(p_ref=p_refs[k]):
                g = p_ref[0].astype(F32)
                for s in range(1, N_DEV):
                    g = g + p_ref[s].astype(F32)
                nm = ADAM_B1 * m_ref[...] + (1.0 - ADAM_B1) * g
                nv = ADAM_B2 * v_ref[...] + (1.0 - ADAM_B2) * (g * g)
                g_ref[...] = g
                nm_ref[...] = nm
                nv_ref[...] = nv
                d_ref[...] = -ADAM_LR * ((nm / c1) / (jnp.sqrt(nv / c2) + ADAM_EPS) + ADAM_WD * w_ref[...])

    def part_spec(k):
        return pl.BlockSpec((N_DEV, tr, c), lambda l, i: (0, jnp.where(l == k, i, 0), 0))

    row = pl.BlockSpec((tr, c), lambda l, i: (l * nt + i, 0))
    o = jax.ShapeDtypeStruct((nl * r, c), F32)
    return pl.pallas_call(
        body, name=name, grid=(nl, nt),
        in_specs=[part_spec(k) for k in range(nl)] + [row, row, row],
        out_specs=[row, row, row, row], out_shape=[o, o, o, o],
        compiler_params=_cp(("arbitrary", "arbitrary")),
    )(*parts, w, m, v)


def _with_token(gain, token):
    return gain if token is None else gain + token[0:1, 0:1]


def _layer_fwd(x, wl, sl, tabs, l, rest_fn=None):
    ctab, stab, bd = tabs
    n = f"l{l}_"
    h = _rmsnorm_fwd(x, sl["norm_mix"], n + "norm_mix")
    qkv = _mm(h, wl["w_qkv"], "nn", out_dtype=F32, name=n + "proj_qkv", tm=1024, tn=QKV_W, tk=D_MODEL)
    uug = _mm(h, wl["w_uug"], "nn", out_dtype=F32, name=n + "proj_uug", tm=1024, tn=UUG_W, tk=D_MODEL)
    gg = _mm(h, wl["w_gg"], "nn", out_dtype=F32, name=n + "proj_gg", tm=1024, tn=GG_W, tk=D_MODEL)
    qs, kd, vd = _qk_prep_fwd(qkv, sl["gq"], sl["gk"], ctab, stab, bd, n + "qk_prep")
    a_out = _attn_fwd(qs, kd, vd, sl["sinks"], n + "attn")
    y0, y2 = _conv_fwd(uug, sl["conv_w32"], sl["conv_b"], sl["ln_g"], sl["ln_b"], n + "conv")
    token = None
    if rest_fn is not None:
        rest, token = rest_fn(y2)
        wl = {**wl, **rest}
    c_out = _mm(y2, wl["w_conv_out"], "nn", out_dtype=F32, name=n + "conv_out", tm=1024, tn=D_MODEL, tk=CONV_CH)
    merged = _merge_fwd(a_out, c_out, gg, n + "merge")
    x1 = _mm(merged, wl["w_out"], "nn", out_dtype=F32, name=n + "out_proj", tm=1024, tn=D_MODEL, tk=D_MODEL,
             resid=x)
    h2 = _rmsnorm_fwd(x1, _with_token(sl["norm_ffn"], token), n + "norm_ffn")
    act = _ffn_up_fwd(h2, wl["w_gate_up"], n + "ffn_up")
    x2 = _mm(act, wl["w_down"], "nn", out_dtype=F32, name=n + "ffn_down", tm=1024, tn=D_MODEL, tk=FF_TN, resid=x1)
    saved = dict(x=x, h=h, qkv=qkv, uug=uug, gg=gg, qs=qs, kd=kd, vd=vd, a_out=a_out, y0=y0, y2=y2,
                 c_out=c_out, merged=merged, x1=x1, h2=h2, act=act)
    return x2, saved, wl


def _layer_bwd(dx2, sv, wl, sl, tabs, l, after=None, ffn_hook=None):
    ctab, stab, bd = tabs
    n = f"l{l}_b_"
    tk = 1024
    gw, gs = {}, {}
    gw["w_down"] = _mm(sv["act"], dx2, "tn", out_dtype=F32, name=n + "dw_down", tm=FF_TN, tn=D_MODEL, tk=tk,
                       after=after)
    dg, du = _ffn_bwd_mid(sv["h2"], dx2, wl["w_gate_up"], wl["w_down"], n + "ffn_mid", after=after)
    dh2 = _mm(dg, wl["w_gate"], "nt", out_dtype=F32, name=n + "dh2_g", tm=1024, tn=D_MODEL, tk=FF_TN)
    dh2 = _mm(du, wl["w_up"], "nt", out_dtype=F32, name=n + "dh2_u", tm=1024, tn=D_MODEL, tk=FF_TN, resid=dh2)
    gw["w_gate"] = _mm(sv["h2"], dg, "tn", out_dtype=F32, name=n + "dw_gate", tm=D_MODEL, tn=FF_TN, tk=tk)
    gw["w_up"] = _mm(sv["h2"], du, "tn", out_dtype=F32, name=n + "dw_up", tm=D_MODEL, tn=FF_TN, tk=tk)
    token = None if ffn_hook is None else ffn_hook(gw)
    dx1, gs["norm_ffn"] = _rmsnorm_bwd(dh2, sv["x1"], _with_token(sl["norm_ffn"], token), dx2, n + "norm_ffn")
    dmerged = _mm(dx1, wl["w_out"], "nt", out_dtype=F32, name=n + "dmerged", tm=1024, tn=D_MODEL, tk=D_MODEL)
    gw["w_out"] = _mm(sv["merged"], dx1, "tn", out_dtype=F32, name=n + "dw_out", tm=D_MODEL, tn=D_MODEL, tk=tk)
    da_out, dc_out, dgg = _merge_bwd(dmerged, sv["a_out"], sv["c_out"], sv["gg"], n + "merge")
    dy2 = _mm(dc_out, wl["w_conv_out"], "nt", out_dtype=F32, name=n + "dy2", tm=1024, tn=CONV_CH, tk=D_MODEL)
    gw["w_conv_out"] = _mm(sv["y2"], dc_out, "tn", out_dtype=F32, name=n + "dw_conv_out", tm=CONV_CH, tn=D_MODEL,
                           tk=tk)
    dy0, gs["ln_g"], gs["ln_b"], gs["conv_b"] = _conv_bwd_ln(dy2, sv["y0"], sl["ln_g"], sl["ln_b"], n + "conv_ln")
    duug, gs["conv_w"] = _conv_bwd_taps(dy0, sv["uug"], sl["conv_w32"], n + "conv_taps")
    dqs, dkd, dvd, gs["sinks"] = _attn_bwd(sv["qs"], sv["kd"], sv["vd"], sl["sinks"], da_out, n + "attn")
    dqkv, gs["gq"], gs["gk"] = _qk_prep_bwd(dqs, dkd, dvd, sv["qkv"], sl["gq"], sl["gk"], ctab, stab, bd,
                                            n + "qk_prep")
    dh = _mm(dqkv, wl["w_qkv"], "nt", out_dtype=F32, name=n + "dh_qkv", tm=1024, tn=D_MODEL, tk=QKV_W)
    dh = _mm(duug, wl["w_uug"], "nt", out_dtype=F32, name=n + "dh_uug", tm=1024, tn=D_MODEL, tk=UUG_W, resid=dh)
    dh = _mm(dgg, wl["w_gg"], "nt", out_dtype=F32, name=n + "dh_gg", tm=1024, tn=D_MODEL, tk=GG_W, resid=dh)
    gw["w_qkv"] = _mm(sv["h"], dqkv, "tn", out_dtype=F32, name=n + "dw_qkv", tm=D_MODEL, tn=QKV_W, tk=tk)
    gw["w_uug"] = _mm(sv["h"], duug, "tn", out_dtype=F32, name=n + "dw_uug", tm=D_MODEL, tn=UUG_W, tk=tk)
    gw["w_gg"] = _mm(sv["h"], dgg, "tn", out_dtype=F32, name=n + "dw_gg", tm=D_MODEL, tn=GG_W, tk=tk)
    dx, gs["norm_mix"] = _rmsnorm_bwd(dh, sv["x"], sl["norm_mix"], dx1, n + "norm_mix")
    return dx, gw, gs


def _cols_to_full(g):
    n, l, r, c = g.shape
    return jnp.transpose(g, (1, 2, 0, 3)).reshape(l, r, n * c)


def _rows_to_full(g):
    n, l, r, c = g.shape
    return jnp.transpose(g, (1, 0, 2, 3)).reshape(l, n * r, c)


def _full_to_cols(w):
    l, r, c = w.shape
    return jnp.transpose(w.reshape(l, r, N_DEV, c // N_DEV), (2, 0, 1, 3))


def _full_to_rows(w):
    l, r, c = w.shape
    return jnp.transpose(w.reshape(l, N_DEV, r // N_DEV, c), (1, 0, 2, 3))


SMALL = (("norm_mix", D_MODEL), ("q_norm", HEAD_DIM), ("k_norm", HEAD_DIM), ("sinks", N_HEADS),
         ("conv_w", CONV_WIDTH * CONV_CH), ("conv_b", CONV_CH), ("conv_ln_g", CONV_CH), ("conv_ln_b", CONV_CH),
         ("norm_ffn", D_MODEL))
SMALL_TOTAL = DEPTH * sum(s for _, s in SMALL)
SMALL_ROWS = -(-SMALL_TOTAL // (LANES * SUBLANES)) * SUBLANES


def _pack_small(d):
    flat = jnp.concatenate([d[k].reshape(-1).astype(F32) for k, _ in SMALL])
    flat = jnp.pad(flat, (0, SMALL_ROWS * LANES - SMALL_TOTAL))
    return flat.reshape(SMALL_ROWS, LANES)


def _unpack_small(buf, shapes):
    flat = buf.reshape(-1)
    out, o = {}, 0
    for k, s in SMALL:
        out[k] = flat[o:o + DEPTH * s].reshape(shapes[k])
        o += DEPTH * s
    return out


def kernel(x, norm_mix, w_in, q_norm, k_norm, sinks, conv_w, conv_b, conv_ln_g, conv_ln_b, w_conv_out, w_out, norm_ffn, w_gate_up, w_down, loss_target, m_norm_mix, m_w_in, m_q_norm, m_k_norm, m_sinks, m_conv_w, m_conv_b, m_conv_ln_g, m_conv_ln_b, m_w_conv_out, m_w_out, m_norm_ffn, m_w_gate_up, m_w_down, v_norm_mix, v_w_in, v_q_norm, v_k_norm, v_sinks, v_conv_w, v_conv_b, v_conv_ln_g, v_conv_ln_b, v_w_conv_out, v_w_out, v_norm_ffn, v_w_gate_up, v_w_down):
    t = x.shape[1]
    me = 4 * lax.axis_index("x") + 2 * lax.axis_index("y") + lax.axis_index("c")
    xs = x.reshape(t, D_MODEL)
    target = loss_target.reshape(t, D_MODEL)

    def shards_in(l):
        return [w_in[l].astype(BF16)]

    def shards_rest(l):
        return [w_conv_out[l].astype(BF16), w_out[l].astype(BF16), w_gate_up[l].astype(BF16), w_down[l].astype(BF16)]

    def weights_in(g_in):
        f_in = _cols_to_full(g_in[:, None])[0]
        return dict(w_qkv=f_in[:, :QKV_W], w_uug=f_in[:, QKV_W:QKV_W + UUG_W], w_gg=f_in[:, QKV_W + UUG_W:])

    def weights_rest(g):
        g_co, g_out, g_gu, g_dn = g
        f_gu = _cols_to_full(g_gu[:, None])[0]
        return dict(w_conv_out=_cols_to_full(g_co[:, None])[0], w_out=_rows_to_full(g_out[:, None])[0],
                    w_gate_up=f_gu, w_gate=f_gu[:, :D_FF], w_up=f_gu[:, D_FF:], w_down=_rows_to_full(g_dn[:, None])[0])

    g_in0, g_cw = _exchange(shards_in(0) + [conv_w], scatter=False, name="gather_w_in_0")
    f_cw = _cols_to_full(g_cw)
    tabs = _rope_tables(t) + (_block_diag_ones(),)

    def layer_small(l, token):
        return dict(norm_mix=_with_token(norm_mix[l][None], token), norm_ffn=norm_ffn[l][None],
                    gq=jnp.tile(q_norm[l], 2)[None], gk=jnp.tile(k_norm[l], 2)[None], sinks=sinks[l],
                    conv_w32=jnp.pad(f_cw[l], ((0, HALO - CONV_WIDTH), (0, 0))),
                    conv_b=conv_b[l][None], ln_g=conv_ln_g[l][None], ln_b=conv_ln_b[l][None])

    wls, sls, saved = [], [], []
    cur = xs
    flight = {"next": None}

    def start_next(l, after):
        flight["next"] = _exchange_start(shards_in(l + 1) + shards_rest(l + 1), False, f"gather_start_{l + 1}",
                                         after=after)
        return flight["next"][4]

    rest0 = _exchange_start(shards_rest(0), False, "gather_start_rest_0", after=g_in0)

    def rest_fn0(after):
        g = _exchange_wait(rest0, False, "gather_wait_rest_0", after=after)
        return weights_rest(g), start_next(0, g[0])

    gathered = None
    for l in range(DEPTH):
        if l == 0:
            w_first, token, rest_fn = weights_in(g_in0), rest0[4], rest_fn0
        else:
            w_first = {**weights_in(gathered[0]), **weights_rest(gathered[1:])}
            token = start_next(l, gathered[0]) if l + 1 < DEPTH else None
            rest_fn = None
        sls.append(layer_small(l, token))
        cur, sv, wl = _layer_fwd(cur, w_first, sls[l], tabs, l, rest_fn=rest_fn)
        wls.append(wl)
        saved.append(sv)
        if l + 1 < DEPTH:
            gathered = _exchange_wait(flight["next"], False, f"gather_wait_{l + 1}", after=cur)
    dy, loss_part = _loss_head(cur, target, "loss_head")
    loss = lax.psum(jnp.sum(loss_part), ("x", "y", "c"))

    def slabs_ffn(gw):
        d_gu = jnp.concatenate([gw["w_gate"], gw["w_up"]], axis=1)[None]
        return [_full_to_cols(d_gu)[:, 0].astype(BF16), _full_to_rows(gw["w_down"][None])[:, 0].astype(BF16)]

    def slabs_mix(gw):
        d_in = jnp.concatenate([gw["w_qkv"], gw["w_uug"], gw["w_gg"]], axis=1)[None]
        return [_full_to_cols(d_in)[:, 0].astype(BF16), _full_to_cols(gw["w_conv_out"][None])[:, 0].astype(BF16),
                _full_to_rows(gw["w_out"][None])[:, 0].astype(BF16)]

    gss = [None] * DEPTH
    parts_ffn, parts_mix = [None] * DEPTH, [None] * DEPTH
    dcur = dy
    state = {"mix": None, "ffn": None}

    def make_ffn_hook(l):
        def hook(gw):
            sends = slabs_ffn(gw)
            after = sends[0]
            if state["mix"] is not None:
                parts_mix[l + 1] = _exchange_wait(state["mix"], True, f"scatter_wait_mix_{l + 1}", after=sends[0])
                after = parts_mix[l + 1][0]
            state["ffn"] = _exchange_start(sends, True, f"scatter_start_ffn_{l}", after=after)
            return state["ffn"][4]
        return hook

    for l in reversed(range(DEPTH)):
        dcur, gw, gss[l] = _layer_bwd(dcur, saved[l], wls[l], sls[l], tabs, l,
                                      after=None if state["mix"] is None else state["mix"][4],
                                      ffn_hook=make_ffn_hook(l))
        parts_ffn[l] = _exchange_wait(state["ffn"], True, f"scatter_wait_ffn_{l}", after=dcur)
        if l > 0:
            state["mix"] = _exchange_start(slabs_mix(gw), True, f"scatter_start_mix_{l}", after=parts_ffn[l][0])
        else:
            parts_mix[0] = _exchange(slabs_mix(gw), scatter=True, name="scatter_grads_mix_0")
    grad_x = dcur.reshape(x.shape)
    parts = [[parts_mix[l][0] for l in range(DEPTH)], [parts_mix[l][1] for l in range(DEPTH)],
             [parts_mix[l][2] for l in range(DEPTH)], [parts_ffn[l][0] for l in range(DEPTH)],
             [parts_ffn[l][1] for l in range(DEPTH)]]

    def update(p, w, m, v, name, tr):
        shp = w.shape
        r = shp[0] * shp[1]
        flat = lambda a: a.reshape(r, shp[2])
        outs = _adamw(p, flat(w), flat(m), flat(v), name, tr)
        return [o.reshape(shp) for o in outs]

    u_in = update(parts[0], w_in, m_w_in, v_w_in, "adamw_w_in", 256)
    u_co = update(parts[1], w_conv_out, m_w_conv_out, v_w_conv_out, "adamw_w_conv_out", 512)
    u_out = update(parts[2], w_out, m_w_out, v_w_out, "adamw_w_out", 128)
    u_gu = update(parts[3], w_gate_up, m_w_gate_up, v_w_gate_up, "adamw_w_gate_up", 256)
    u_dn = update(parts[4], w_down, m_w_down, v_w_down, "adamw_w_down", 176)

    def fold_rows(a):
        return jnp.sum(a, axis=0)

    def fold_heads(a):
        return jnp.sum(a, axis=0).reshape(2, HEAD_DIM).sum(axis=0)

    small_g = {
        "norm_mix": jnp.stack([fold_rows(gss[l]["norm_mix"]) for l in range(DEPTH)]),
        "q_norm": jnp.stack([fold_heads(gss[l]["gq"]) for l in range(DEPTH)]),
        "k_norm": jnp.stack([fold_heads(gss[l]["gk"]) for l in range(DEPTH)]),
        "sinks": jnp.stack([gss[l]["sinks"][0, :N_HEADS] for l in range(DEPTH)]),
        "conv_w": jnp.stack([gss[l]["conv_w"].reshape(CONV_WIDTH, SUBLANES, CONV_CH).sum(axis=1)
                             for l in range(DEPTH)]),
        "conv_b": jnp.stack([fold_rows(gss[l]["conv_b"]) for l in range(DEPTH)]),
        "conv_ln_g": jnp.stack([fold_rows(gss[l]["ln_g"]) for l in range(DEPTH)]),
        "conv_ln_b": jnp.stack([fold_rows(gss[l]["ln_b"]) for l in range(DEPTH)]),
        "norm_ffn": jnp.stack([fold_rows(gss[l]["norm_ffn"]) for l in range(DEPTH)]),
    }
    (small_parts,) = _exchange([_pack_small(small_g)], scatter=False, name="gather_small_grads")
    shapes = {"norm_mix": norm_mix.shape, "q_norm": q_norm.shape, "k_norm": k_norm.shape, "sinks": sinks.shape,
              "conv_w": (DEPTH, CONV_WIDTH, CONV_CH), "conv_b": conv_b.shape, "conv_ln_g": conv_ln_g.shape,
              "conv_ln_b": conv_ln_b.shape, "norm_ffn": norm_ffn.shape}

    def widen(a):
        z = jnp.zeros((DEPTH, CONV_WIDTH, N_DEV, CONV_CH // N_DEV), F32)
        z = lax.dynamic_update_slice(z, a[:, :, None, :], (0, 0, me, 0))
        return z.reshape(DEPTH, CONV_WIDTH, CONV_CH)

    sw = _pack_small(dict(norm_mix=norm_mix, q_norm=q_norm, k_norm=k_norm, sinks=sinks, conv_w=widen(conv_w),
                          conv_b=conv_b, conv_ln_g=conv_ln_g, conv_ln_b=conv_ln_b, norm_ffn=norm_ffn))
    sm = _pack_small(dict(norm_mix=m_norm_mix, q_norm=m_q_norm, k_norm=m_k_norm, sinks=m_sinks,
                          conv_w=widen(m_conv_w), conv_b=m_conv_b, conv_ln_g=m_conv_ln_g, conv_ln_b=m_conv_ln_b,
                          norm_ffn=m_norm_ffn))
    sv_ = _pack_small(dict(norm_mix=v_norm_mix, q_norm=v_q_norm, k_norm=v_k_norm, sinks=v_sinks,
                           conv_w=widen(v_conv_w), conv_b=v_conv_b,
                           conv_ln_g=v_conv_ln_g, conv_ln_b=v_conv_ln_b, norm_ffn=v_norm_ffn))
    s_outs = [_unpack_small(o, shapes) for o in _adamw([small_parts], sw, sm, sv_, "adamw_small", SMALL_ROWS)]

    def narrow(a):
        a4 = a.reshape(DEPTH, CONV_WIDTH, N_DEV, CONV_CH // N_DEV)
        return lax.dynamic_slice(a4, (0, 0, me, 0), (DEPTH, CONV_WIDTH, 1, CONV_CH // N_DEV)).reshape(
            DEPTH, CONV_WIDTH, CONV_CH // N_DEV)

    big = {"w_in": u_in, "w_conv_out": u_co, "w_out": u_out, "w_gate_up": u_gu, "w_down": u_dn}
    order = ["norm_mix", "w_in", "q_norm", "k_norm", "sinks", "conv_w", "conv_b", "conv_ln_g", "conv_ln_b",
             "w_conv_out", "w_out", "norm_ffn", "w_gate_up", "w_down"]
    outs = [loss, grad_x]
    for kind in range(4):
        for name in order:
            if name in big:
                outs.append(big[name][kind])
            elif name == "conv_w":
                outs.append(narrow(s_outs[kind][name]))
            else:
                outs.append(s_outs[kind][name])
    return tuple(outs)
```

```python
import functools
import math

import jax
import jax.numpy as jnp
from jax import lax
from jax.experimental import pallas as pl
from jax.experimental.pallas import tpu as pltpu

F32 = jnp.float32
BF16 = jnp.bfloat16

D_MODEL = 1024
DEPTH = 4
N_HEADS = 16
N_KV_HEADS = 2
HEAD_DIM = 64
ROT_DIM = HEAD_DIM // 4
ROPE_THETA = 500000.0
BLOCK = 128
CONV_CH = D_MODEL // 2
CONV_WIDTH = 31
D_FF = 2816
EPS = 1e-6
Q_W = N_HEADS * HEAD_DIM
KV_W = N_KV_HEADS * HEAD_DIM
QKV_W = Q_W + 2 * KV_W
UUG_W = 2 * CONV_CH
GG_W = 2 * D_MODEL
IN_W = QKV_W + UUG_W + GG_W
N_DEV = 8

ADAM_LR = 0.001
ADAM_B1 = 0.9
ADAM_B2 = 0.999
ADAM_EPS = 1e-08
ADAM_WD = 0.01
ADAM_STEP = 10

LANES = 128
SUBLANES = 8
HALO = 32
VMEM_LIMIT = 56 * 1024 * 1024
NEG = -1e30


def _cp(sem=None):
    return pltpu.CompilerParams(dimension_semantics=sem, vmem_limit_bytes=VMEM_LIMIT)


def _sigmoid(z):
    return 1.0 / (1.0 + jnp.exp(-z))


def _rowgroup_sum(z):
    r, c = z.shape
    return jnp.sum(z.reshape(r // SUBLANES, SUBLANES, c), axis=0)


_DIMS = {"nn": (((1,), (0,)), ((), ())), "nt": (((1,), (1,)), ((), ())), "tn": (((0,), (0,)), ((), ()))}


def _mm(a, b, mode, *, out_dtype, name, tm, tn, tk, resid=None, after=None):
    if mode == "nn":
        (m, k), (k2, n) = a.shape, b.shape
        a_spec = pl.BlockSpec((tm, tk), lambda i, j, s: (i, s))
        b_spec = pl.BlockSpec((tk, tn), lambda i, j, s: (s, j))
    elif mode == "nt":
        (m, k), (n, k2) = a.shape, b.shape
        a_spec = pl.BlockSpec((tm, tk), lambda i, j, s: (i, s))
        b_spec = pl.BlockSpec((tn, tk), lambda i, j, s: (j, s))
    else:
        (k, m), (k2, n) = a.shape, b.shape
        a_spec = pl.BlockSpec((tk, tm), lambda i, j, s: (s, i))
        b_spec = pl.BlockSpec((tk, tn), lambda i, j, s: (s, j))
    assert k == k2 and m % tm == 0 and n % tn == 0 and k % tk == 0, (name, a.shape, b.shape, tm, tn, tk)
    nk = k // tk
    dims = _DIMS[mode]
    has_resid = resid is not None

    def body(*refs):
        a_ref, b_ref = refs[0], refs[1]
        r_ref = refs[2] if has_resid else None
        o_ref = refs[-1] if nk == 1 else refs[-2]
        part = lax.dot_general(a_ref[...].astype(BF16), b_ref[...].astype(BF16), dims, preferred_element_type=F32)

        def finish(acc):
            if has_resid:
                acc = acc + r_ref[...]
            o_ref[...] = acc.astype(out_dtype)

        if nk == 1:
            finish(part)
            return
        acc_ref = refs[-1]
        s = pl.program_id(2)

        @pl.when(s == 0)
        def _():
            acc_ref[...] = part

        @pl.when(s > 0)
        def _():
            acc_ref[...] += part

        @pl.when(s == nk - 1)
        def _():
            finish(acc_ref[...])

    in_specs = [a_spec, b_spec]
    args = [a, b]
    if has_resid:
        in_specs.append(pl.BlockSpec((tm, tn), lambda i, j, s: (i, j)))
        args.append(resid)
    if after is not None:
        in_specs.append(_ANY)
        args.append(after)
    return pl.pallas_call(
        body, name=name, grid=(m // tm, n // tn, nk),
        in_specs=in_specs, out_specs=pl.BlockSpec((tm, tn), lambda i, j, s: (i, j)),
        out_shape=jax.ShapeDtypeStruct((m, n), out_dtype),
        scratch_shapes=[] if nk == 1 else [pltpu.VMEM((tm, tn), F32)],
        compiler_params=_cp(("parallel", "parallel", "arbitrary")),
    )(*args)


def _mm_rows(pairs, mode, *, name, tm, resid=None, norm_fwd=None, norm_bwd=None, after=None):
    m = pairs[0][0].shape[0]
    n = pairs[0][1].shape[1] if mode == "nn" else pairs[0][1].shape[0]
    assert m % tm == 0 and not (norm_fwd is not None and norm_bwd is not None), name
    dims = _DIMS[mode]
    np_ = len(pairs)
    row = pl.BlockSpec((tm, n), lambda i: (i, 0))
    vec = pl.BlockSpec((1, n), lambda i: (0, 0))
    in_specs, args = [], []
    for a, b in pairs:
        k = a.shape[1]
        assert a.shape[0] == m and (b.shape == (k, n) if mode == "nn" else b.shape == (n, k)), (name, a.shape, b.shape)
        in_specs += [pl.BlockSpec((tm, k), lambda i: (i, 0)), pl.BlockSpec(b.shape, lambda i: (0, 0))]
        args += [a, b]
    if resid is not None:
        in_specs.append(row)
        args.append(resid)
    if norm_fwd is not None:
        in_specs.append(vec)
        args.append(norm_fwd)
    if norm_bwd is not None:
        in_specs += [row, vec, row]
        args += list(norm_bwd)
    if after is not None:
        in_specs.append(_ANY)
        args.append(after)
    n_out = 1 if (norm_fwd is None and norm_bwd is None) else 2

    def body(*refs):
        outs = refs[len(refs) - n_out:]
        pos = 2 * np_
        acc = None
        for p in range(np_):
            part = lax.dot_general(refs[2 * p][...].astype(BF16), refs[2 * p + 1][...].astype(BF16), dims,
                                   preferred_element_type=F32)
            acc = part if acc is None else acc + part
        if resid is not None:
            acc = acc + refs[pos][...]
            pos += 1
        if norm_fwd is not None:
            r = lax.rsqrt(jnp.mean(acc * acc, axis=-1, keepdims=True) + EPS)
            outs[1][...] = ((acc * r) * refs[pos][...]).astype(BF16)
        if norm_bwd is not None:
            @pl.when(pl.program_id(0) == 0)
            def _():
                outs[1][...] = jnp.zeros_like(outs[1])

            xv = refs[pos][...]
            r = lax.rsqrt(jnp.mean(xv * xv, axis=-1, keepdims=True) + EPS)
            y = xv * r
            outs[1][...] += _rowgroup_sum(acc * y)
            dy = acc * refs[pos + 1][...]
            acc = refs[pos + 2][...] + r * (dy - y * jnp.mean(dy * y, axis=-1, keepdims=True))
        outs[0][...] = acc

    out_specs = [row]
    out_shape = [jax.ShapeDtypeStruct((m, n), F32)]
    if norm_fwd is not None:
        out_specs.append(row)
        out_shape.append(jax.ShapeDtypeStruct((m, n), BF16))
    if norm_bwd is not None:
        out_specs.append(pl.BlockSpec((SUBLANES, n), lambda i: (0, 0)))
        out_shape.append(jax.ShapeDtypeStruct((SUBLANES, n), F32))
    res = pl.pallas_call(
        body, name=name, grid=(m // tm,), in_specs=in_specs, out_specs=out_specs, out_shape=out_shape,
        compiler_params=_cp(("arbitrary",) if norm_bwd is not None else ("parallel",)),
    )(*args)
    return res[0] if n_out == 1 else res


def _rmsnorm_fwd(x, g, name, tm=512):
    t, d = x.shape

    def body(x_ref, g_ref, h_ref):
        xv = x_ref[...]
        r = lax.rsqrt(jnp.mean(xv * xv, axis=-1, keepdims=True) + EPS)
        h_ref[...] = ((xv * r) * g_ref[...]).astype(BF16)

    return pl.pallas_call(
        body, name=name, grid=(t // tm,),
        in_specs=[pl.BlockSpec((tm, d), lambda i: (i, 0)), pl.BlockSpec((1, d), lambda i: (0, 0))],
        out_specs=pl.BlockSpec((tm, d), lambda i: (i, 0)),
        out_shape=jax.ShapeDtypeStruct((t, d), BF16),
        compiler_params=_cp(("parallel",)),
    )(x, g)


def _rmsnorm_bwd(dh, x, g, resid, name, tm=512):
    t, d = x.shape

    def body(dh_ref, x_ref, g_ref, r_ref, dx_ref, dg_ref):
        @pl.when(pl.program_id(0) == 0)
        def _():
            dg_ref[...] = jnp.zeros_like(dg_ref)

        xv = x_ref[...]
        dhv = dh_ref[...]
        r = lax.rsqrt(jnp.mean(xv * xv, axis=-1, keepdims=True) + EPS)
        y = xv * r
        dg_ref[...] += _rowgroup_sum(dhv * y)
        dy = dhv * g_ref[...]
        dx_ref[...] = r_ref[...] + r * (dy - y * jnp.mean(dy * y, axis=-1, keepdims=True))

    row = pl.BlockSpec((tm, d), lambda i: (i, 0))
    return pl.pallas_call(
        body, name=name, grid=(t // tm,),
        in_specs=[row, row, pl.BlockSpec((1, d), lambda i: (0, 0)), row],
        out_specs=[row, pl.BlockSpec((SUBLANES, d), lambda i: (0, 0))],
        out_shape=[jax.ShapeDtypeStruct((t, d), F32), jax.ShapeDtypeStruct((SUBLANES, d), F32)],
        compiler_params=_cp(("arbitrary",)),
    )(dh, x, g, resid)


def _seg_sum(z, bd):
    hi = z.astype(BF16)
    lo = (z - hi.astype(F32)).astype(BF16)
    return jnp.dot(hi, bd, preferred_element_type=F32) + jnp.dot(lo, bd, preferred_element_type=F32)


def _partner(z, lane64):
    return jnp.where(lane64 < ROT_DIM // 2, pltpu.roll(z, LANES - ROT_DIM // 2, 1), pltpu.roll(z, ROT_DIM // 2, 1))


def _rope_tables(t):
    inv_freq = ROPE_THETA ** (-jnp.arange(0, ROT_DIM, 2, dtype=F32) / ROT_DIM)
    ang = jnp.arange(t, dtype=F32)[:, None] * inv_freq[None, :]
    cos, sin = jnp.cos(ang), jnp.sin(ang)
    c64 = jnp.concatenate([cos, cos, jnp.ones((t, HEAD_DIM - ROT_DIM), F32)], axis=1)
    s64 = jnp.concatenate([-sin, sin, jnp.zeros((t, HEAD_DIM - ROT_DIM), F32)], axis=1)
    return jnp.tile(c64, (1, 2)), jnp.tile(s64, (1, 2))


def _block_diag_ones():
    r = lax.broadcasted_iota(jnp.int32, (LANES, LANES), 0) // HEAD_DIM
    c = lax.broadcasted_iota(jnp.int32, (LANES, LANES), 1) // HEAD_DIM
    return (r == c).astype(BF16)


def _qk_prep_fwd(qkv, gq, gk, ctab, stab, bd, name, tm=512):
    t = qkv.shape[0]
    scale = HEAD_DIM ** -0.5
    n_qg = Q_W // LANES

    def body(q_ref, kv_ref, gq_ref, gk_ref, c_ref, s_ref, bd_ref, qs_ref, kd_ref, vd_ref):
        lane = lax.broadcasted_iota(jnp.int32, (tm, LANES), 1)
        lane64 = lane % HEAD_DIM
        lo_half = lane < HEAD_DIM
        cv, sv, bdv = c_ref[...], s_ref[...], bd_ref[...]

        def norm_rope(xg, g):
            r = lax.rsqrt(_seg_sum(xg * xg, bdv) * (1.0 / HEAD_DIM) + EPS)
            yn = (xg * r) * g
            return yn * cv + _partner(yn, lane64) * sv

        zero = jnp.zeros((BLOCK, LANES), BF16)
        lo_blk = lo_half[0:BLOCK]
        for c in range(n_qg):
            xg = q_ref[:, c * LANES:(c + 1) * LANES]
            qn = (norm_rope(xg, gq_ref[...]) * scale).astype(BF16)
            for b in range(tm // BLOCK):
                rows = qn[b * BLOCK:(b + 1) * BLOCK]
                qs_ref[b, 2 * c] = jnp.where(lo_blk, rows, zero)
                qs_ref[b, 2 * c + 1] = jnp.where(lo_blk, zero, rows)
        kk = norm_rope(kv_ref[:, 0:LANES], gk_ref[...])
        kr = pltpu.roll(kk, HEAD_DIM, 1)
        kd_ref[:, 0:LANES] = jnp.where(lo_half, kk, kr).astype(BF16)
        kd_ref[:, LANES:2 * LANES] = jnp.where(lo_half, kr, kk).astype(BF16)
        vv = kv_ref[:, LANES:2 * LANES]
        vr = pltpu.roll(vv, HEAD_DIM, 1)
        vd_ref[:, 0:LANES] = jnp.where(lo_half, vv, vr).astype(BF16)
        vd_ref[:, LANES:2 * LANES] = jnp.where(lo_half, vr, vv).astype(BF16)

    vec = pl.BlockSpec((1, LANES), lambda i: (0, 0))
    tab = pl.BlockSpec((tm, LANES), lambda i: (i, 0))
    return pl.pallas_call(
        body, name=name, grid=(t // tm,),
        in_specs=[pl.BlockSpec((tm, Q_W), lambda i: (i, 0)),
                  pl.BlockSpec((tm, 2 * KV_W), lambda i: (i, Q_W // (2 * KV_W))),
                  vec, vec, tab, tab, pl.BlockSpec((LANES, LANES), lambda i: (0, 0))],
        out_specs=[pl.BlockSpec((tm // BLOCK, N_HEADS, BLOCK, LANES), lambda i: (i, 0, 0, 0)),
                   pl.BlockSpec((tm, 2 * LANES), lambda i: (i, 0)),
                   pl.BlockSpec((tm, 2 * LANES), lambda i: (i, 0))],
        out_shape=[jax.ShapeDtypeStruct((t // BLOCK, N_HEADS, BLOCK, LANES), BF16),
                   jax.ShapeDtypeStruct((t, 2 * LANES), BF16), jax.ShapeDtypeStruct((t, 2 * LANES), BF16)],
        compiler_params=_cp(("parallel",)),
    )(qkv, qkv, gq, gk, ctab, stab, bd)


def _qk_prep_bwd(dqs, dkd, dvd, qkv, gq, gk, ctab, stab, bd, name, tm=512):
    t = qkv.shape[0]
    scale = HEAD_DIM ** -0.5
    n_qg = Q_W // LANES

    def body(dqs_ref, dkd_ref, dvd_ref, q_ref, kv_ref, gq_ref, gk_ref, c_ref, s_ref, bd_ref,
             dqkv_ref, dgq_ref, dgk_ref):
        @pl.when(pl.program_id(0) == 0)
        def _():
            dgq_ref[...] = jnp.zeros_like(dgq_ref)
            dgk_ref[...] = jnp.zeros_like(dgk_ref)

        lane = lax.broadcasted_iota(jnp.int32, (tm, LANES), 1)
        lane64 = lane % HEAD_DIM
        lo_half = lane < HEAD_DIM
        cv, sv, bdv = c_ref[...], s_ref[...], bd_ref[...]

        def bwd(xg, g, dout):
            r = lax.rsqrt(_seg_sum(xg * xg, bdv) * (1.0 / HEAD_DIM) + EPS)
            y = xg * r
            dyn = dout * cv + jnp.where(lane64 < ROT_DIM, _partner(dout * sv, lane64), 0.0)
            dy = dyn * g
            dx = r * (dy - y * (_seg_sum(dy * y, bdv) * (1.0 / HEAD_DIM)))
            return dx, _rowgroup_sum(dyn * y)

        dgq = jnp.zeros((SUBLANES, LANES), F32)
        for c in range(n_qg):
            sl = slice(c * LANES, (c + 1) * LANES)
            dx, dg = bwd(q_ref[:, sl], gq_ref[...], dqs_ref[:, sl] * scale)
            dqkv_ref[:, sl] = dx.astype(BF16)
            dgq = dgq + dg
        dgq_ref[...] += dgq
        dk = jnp.where(lo_half, dkd_ref[:, 0:LANES], dkd_ref[:, LANES:2 * LANES])
        dx, dg = bwd(kv_ref[:, 0:LANES], gk_ref[...], dk)
        dqkv_ref[:, Q_W:Q_W + LANES] = dx.astype(BF16)
        dgk_ref[...] += dg
        dv = jnp.where(lo_half, dvd_ref[:, 0:LANES], dvd_ref[:, LANES:2 * LANES])
        dqkv_ref[:, Q_W + LANES:Q_W + 2 * LANES] = dv.astype(BF16)

    vec = pl.BlockSpec((1, LANES), lambda i: (0, 0))
    tab = pl.BlockSpec((tm, LANES), lambda i: (i, 0))
    wide = pl.BlockSpec((tm, 2 * LANES), lambda i: (i, 0))
    acc = pl.BlockSpec((SUBLANES, LANES), lambda i: (0, 0))
    return pl.pallas_call(
        body, name=name, grid=(t // tm,),
        in_specs=[pl.BlockSpec((tm, Q_W), lambda i: (i, 0)), wide, wide,
                  pl.BlockSpec((tm, Q_W), lambda i: (i, 0)),
                  pl.BlockSpec((tm, 2 * KV_W), lambda i: (i, Q_W // (2 * KV_W))),
                  vec, vec, tab, tab, pl.BlockSpec((LANES, LANES), lambda i: (0, 0))],
        out_specs=[pl.BlockSpec((tm, QKV_W), lambda i: (i, 0)), acc, acc],
        out_shape=[jax.ShapeDtypeStruct((t, QKV_W), BF16), jax.ShapeDtypeStruct((SUBLANES, LANES), F32),
                   jax.ShapeDtypeStruct((SUBLANES, LANES), F32)],
        compiler_params=_cp(("arbitrary",)),
    )(dqs, dkd, dvd, qkv, qkv, gq, gk, ctab, stab, bd)


GROUP = N_HEADS // N_KV_HEADS
GROUP_ROWS = GROUP * BLOCK


def _attn_masks():
    row = lax.broadcasted_iota(jnp.int32, (BLOCK, BLOCK), 0)
    col = lax.broadcasted_iota(jnp.int32, (BLOCK, BLOCK), 1)
    return col <= row, col < HEAD_DIM


def _window_softmax(s_c, s_p, is_cur, has_prev, sink):
    s = jnp.where(is_cur, s_c, jnp.where(has_prev, s_p, NEG))
    m = jnp.maximum(jnp.max(s, axis=-1, keepdims=True), sink)
    e = jnp.exp(s - m)
    e_s = jnp.exp(sink - m)
    inv = 1.0 / (jnp.sum(e, axis=-1, keepdims=True) + e_s)
    return e * inv, e_s * inv


def _attn_fwd(qs, kd, vd, sinks, name):
    nb = qs.shape[0]
    t = nb * BLOCK

    def body(sink_ref, q_ref, kc_ref, kp_ref, vc_ref, vp_ref, o_ref, pc_scr, pp_scr):
        has_prev = pl.program_id(0) > 0
        is_cur, lo_half = _attn_masks()
        for j in range(N_KV_HEADS):
            ks = slice(j * LANES, (j + 1) * LANES)
            qg = q_ref[0, j * GROUP:(j + 1) * GROUP].reshape(GROUP_ROWS, LANES)
            s_c = lax.dot_general(qg, kc_ref[:, ks], _DIMS["nt"], preferred_element_type=F32)
            s_p = lax.dot_general(qg, kp_ref[:, ks], _DIMS["nt"], preferred_element_type=F32)
            for g in range(GROUP):
                rs = slice(g * BLOCK, (g + 1) * BLOCK)
                p, _ = _window_softmax(s_c[rs], s_p[rs], is_cur, has_prev, sink_ref[j * GROUP + g])
                pc_scr[rs, :] = jnp.where(is_cur, p, 0.0).astype(BF16)
                pp_scr[rs, :] = jnp.where(is_cur, 0.0, p).astype(BF16)
            o2 = (jnp.dot(pc_scr[...], vc_ref[:, ks], preferred_element_type=F32)
                  + jnp.dot(pp_scr[...], vp_ref[:, ks], preferred_element_type=F32))
            for pp in range(GROUP // 2):
                c0 = (j * (GROUP // 2) + pp) * LANES
                o_ref[:, c0:c0 + LANES] = jnp.where(lo_half, o2[2 * pp * BLOCK:(2 * pp + 1) * BLOCK],
                                                    o2[(2 * pp + 1) * BLOCK:(2 * pp + 2) * BLOCK])

    cur = lambda i: (i, 0)
    prev = lambda i: (jnp.maximum(i - 1, 0), 0)
    kvs = (BLOCK, 2 * LANES)
    return pl.pallas_call(
        body, name=name, grid=(nb,),
        in_specs=[pl.BlockSpec(memory_space=pltpu.SMEM),
                  pl.BlockSpec((1, N_HEADS, BLOCK, LANES), lambda i: (i, 0, 0, 0)),
                  pl.BlockSpec(kvs, cur), pl.BlockSpec(kvs, prev), pl.BlockSpec(kvs, cur), pl.BlockSpec(kvs, prev)],
        out_specs=pl.BlockSpec((BLOCK, Q_W), cur),
        out_shape=jax.ShapeDtypeStruct((t, Q_W), F32),
        scratch_shapes=[pltpu.VMEM((GROUP_ROWS, LANES), BF16), pltpu.VMEM((GROUP_ROWS, LANES), BF16)],
        compiler_params=_cp(("parallel",)),
    )(sinks, qs, kd, kd, vd, vd)


def _attn_bwd(qs, kd, vd, sinks, do, name):
    nb = qs.shape[0]
    t = nb * BLOCK

    def body(sink_ref, q_ref, do_ref, kc_ref, kp_ref, vc_ref, vp_ref,
             dq_ref, dk_ref, dv_ref, dsink_ref,
             carry_k, carry_v, dsink_acc, do_scr, pc_scr, pp_scr, dsc_scr, dsp_scr):
        i = pl.program_id(0)

        @pl.when(i == 0)
        def _():
            carry_k[...] = jnp.zeros_like(carry_k)
            carry_v[...] = jnp.zeros_like(carry_v)
            dsink_acc[...] = jnp.zeros_like(dsink_acc)

        @pl.when(i < nb)
        def _():
            has_prev = i > 0
            is_cur, lo_half = _attn_masks()
            srow = lax.broadcasted_iota(jnp.int32, (SUBLANES, LANES), 0)
            scol = lax.broadcasted_iota(jnp.int32, (SUBLANES, LANES), 1)
            dsink = jnp.zeros((SUBLANES, LANES), F32)
            for j in range(N_KV_HEADS):
                ks = slice(j * LANES, (j + 1) * LANES)
                kc, kp, vc, vp = kc_ref[:, ks], kp_ref[:, ks], vc_ref[:, ks], vp_ref[:, ks]
                qg = q_ref[0, j * GROUP:(j + 1) * GROUP].reshape(GROUP_ROWS, LANES)
                for pp in range(GROUP // 2):
                    c0 = (j * (GROUP // 2) + pp) * LANES
                    dop = do_ref[:, c0:c0 + LANES]
                    do_scr[2 * pp * BLOCK:(2 * pp + 1) * BLOCK, :] = jnp.where(lo_half, dop, 0.0).astype(BF16)
                    do_scr[(2 * pp + 1) * BLOCK:(2 * pp + 2) * BLOCK, :] = jnp.where(lo_half, 0.0, dop).astype(BF16)
                dog = do_scr[...]
                s_c = lax.dot_general(qg, kc, _DIMS["nt"], preferred_element_type=F32)
                s_p = lax.dot_general(qg, kp, _DIMS["nt"], preferred_element_type=F32)
                dp_c = lax.dot_general(dog, vc, _DIMS["nt"], preferred_element_type=F32)
                dp_p = lax.dot_general(dog, vp, _DIMS["nt"], preferred_element_type=F32)
                for g in range(GROUP):
                    rs = slice(g * BLOCK, (g + 1) * BLOCK)
                    h = j * GROUP + g
                    p, p_s = _window_softmax(s_c[rs], s_p[rs], is_cur, has_prev, sink_ref[h])
                    dp = jnp.where(is_cur, dp_c[rs], dp_p[rs])
                    delta = jnp.sum(p * dp, axis=-1, keepdims=True)
                    ds = p * (dp - delta)
                    dsv = -jnp.sum(p_s * delta, axis=0, keepdims=True)
                    dsink = dsink + jnp.where(jnp.logical_and(srow == 0, scol == h), dsv, 0.0)
                    pc_scr[rs, :] = jnp.where(is_cur, p, 0.0).astype(BF16)
                    pp_scr[rs, :] = jnp.where(is_cur, 0.0, p).astype(BF16)
                    dsc_scr[rs, :] = jnp.where(is_cur, ds, 0.0).astype(BF16)
                    dsp_scr[rs, :] = jnp.where(is_cur, 0.0, ds).astype(BF16)
                dsc, dsp = dsc_scr[...], dsp_scr[...]
                dq2 = jnp.dot(dsc, kc, preferred_element_type=F32) + jnp.dot(dsp, kp, preferred_element_type=F32)
                for pp in range(GROUP // 2):
                    c0 = (j * (GROUP // 2) + pp) * LANES
                    dq_ref[:, c0:c0 + LANES] = jnp.where(lo_half, dq2[2 * pp * BLOCK:(2 * pp + 1) * BLOCK],
                                                         dq2[(2 * pp + 1) * BLOCK:(2 * pp + 2) * BLOCK])
                dk_c = lax.dot_general(dsc, qg, _DIMS["tn"], preferred_element_type=F32)
                dk_p = lax.dot_general(dsp, qg, _DIMS["tn"], preferred_element_type=F32)
                dv_c = lax.dot_general(pc_scr[...], dog, _DIMS["tn"], preferred_element_type=F32)
                dv_p = lax.dot_general(pp_scr[...], dog, _DIMS["tn"], preferred_element_type=F32)
                dk_ref[:, ks] = carry_k[:, ks] + dk_p + pltpu.roll(dk_p, HEAD_DIM, 1)
                dv_ref[:, ks] = carry_v[:, ks] + dv_p + pltpu.roll(dv_p, HEAD_DIM, 1)
                carry_k[:, ks] = dk_c + pltpu.roll(dk_c, HEAD_DIM, 1)
                carry_v[:, ks] = dv_c + pltpu.roll(dv_c, HEAD_DIM, 1)
            dsink_acc[...] += dsink

        @pl.when(i == nb)
        def _():
            dk_ref[...] = carry_k[...]
            dv_ref[...] = carry_v[...]
            dsink_ref[...] = dsink_acc[...]

    last = nb - 1
    cur = lambda i: (jnp.minimum(i, last), 0)
    prev = lambda i: (jnp.clip(i - 1, 0, last), 0)
    kvs = (BLOCK, 2 * LANES)
    stk = pltpu.VMEM((GROUP_ROWS, LANES), BF16)
    return pl.pallas_call(
        body, name=name, grid=(nb + 1,),
        in_specs=[pl.BlockSpec(memory_space=pltpu.SMEM),
                  pl.BlockSpec((1, N_HEADS, BLOCK, LANES), lambda i: (jnp.minimum(i, last), 0, 0, 0)),
                  pl.BlockSpec((BLOCK, Q_W), cur),
                  pl.BlockSpec(kvs, cur), pl.BlockSpec(kvs, prev), pl.BlockSpec(kvs, cur), pl.BlockSpec(kvs, prev)],
        out_specs=[pl.BlockSpec((BLOCK, Q_W), cur), pl.BlockSpec(kvs, prev), pl.BlockSpec(kvs, prev),
                   pl.BlockSpec((SUBLANES, LANES), lambda i: (0, 0))],
        out_shape=[jax.ShapeDtypeStruct((t, Q_W), F32), jax.ShapeDtypeStruct((t, 2 * LANES), F32),
                   jax.ShapeDtypeStruct((t, 2 * LANES), F32), jax.ShapeDtypeStruct((SUBLANES, LANES), F32)],
        scratch_shapes=[pltpu.VMEM(kvs, F32), pltpu.VMEM(kvs, F32), pltpu.VMEM((SUBLANES, LANES), F32),
                        stk, stk, stk, stk, stk],
        compiler_params=_cp(("arbitrary",)),
    )(sinks, qs, do, kd, kd, vd, vd)


CONV_CHUNK = 64


def _fill_row_shifts(sh):
    rows = sh.shape[1] - SUBLANES
    for r in range(1, SUBLANES):
        sh[r, 0:rows, :] = sh[0, r:r + rows, :]


def _shifted_rows(sh, start, size):
    q, r = divmod(start, SUBLANES)
    return sh[r, q * SUBLANES:q * SUBLANES + size, :]


def _conv_fwd(uug, w32, cb, lg, lb, name, tm=512):
    t = uug.shape[0]
    hb = tm // HALO

    def body(m_ref, h_ref, w_ref, cb_ref, lg_ref, lb_ref, y0_ref, y2_ref, a_sh):
        i = pl.program_id(0)
        a_sh[0, HALO:, :] = m_ref[:, 0:CONV_CH] * _sigmoid(m_ref[:, CONV_CH:])
        ah = h_ref[:, 0:CONV_CH] * _sigmoid(h_ref[:, CONV_CH:])
        a_sh[0, 0:HALO, :] = jnp.where(i > 0, ah, 0.0)
        _fill_row_shifts(a_sh)
        off = HALO - (CONV_WIDTH - 1)
        for c in range(tm // CONV_CHUNK):
            r0 = c * CONV_CHUNK
            acc = jnp.zeros((CONV_CHUNK, CONV_CH), F32)
            for k in range(CONV_WIDTH):
                acc = acc + w_ref[k:k + 1, :] * _shifted_rows(a_sh, r0 + off + k, CONV_CHUNK)
            y0 = acc + cb_ref[...]
            y0_ref[r0:r0 + CONV_CHUNK, :] = y0
            mu = jnp.mean(y0, axis=-1, keepdims=True)
            dlt = y0 - mu
            rstd = lax.rsqrt(jnp.mean(dlt * dlt, axis=-1, keepdims=True) + EPS)
            y1 = (dlt * rstd) * lg_ref[...] + lb_ref[...]
            y2_ref[r0:r0 + CONV_CHUNK, :] = (y1 * _sigmoid(y1)).astype(BF16)

    vec = pl.BlockSpec((1, CONV_CH), lambda i: (0, 0))
    return pl.pallas_call(
        body, name=name, grid=(t // tm,),
        in_specs=[pl.BlockSpec((tm, UUG_W), lambda i: (i, 0)),
                  pl.BlockSpec((HALO, UUG_W), lambda i: (jnp.maximum(i * hb - 1, 0), 0)),
                  pl.BlockSpec((HALO, CONV_CH), lambda i: (0, 0)), vec, vec, vec],
        out_specs=[pl.BlockSpec((tm, CONV_CH), lambda i: (i, 0)), pl.BlockSpec((tm, CONV_CH), lambda i: (i, 0))],
        out_shape=[jax.ShapeDtypeStruct((t, CONV_CH), F32), jax.ShapeDtypeStruct((t, CONV_CH), BF16)],
        scratch_shapes=[pltpu.VMEM((SUBLANES, tm + HALO, CONV_CH), F32)],
        compiler_params=_cp(("parallel",)),
    )(uug, uug, w32, cb, lg, lb)


def _conv_bwd_ln(dy2, y0, lg, lb, name, tm=512):
    t = y0.shape[0]

    def body(dy2_ref, y0_ref, lg_ref, lb_ref, dy0_ref, dlg_ref, dlb_ref, dcb_ref):
        @pl.when(pl.program_id(0) == 0)
        def _():
            dlg_ref[...] = jnp.zeros_like(dlg_ref)
            dlb_ref[...] = jnp.zeros_like(dlb_ref)
            dcb_ref[...] = jnp.zeros_like(dcb_ref)

        y0 = y0_ref[...]
        mu = jnp.mean(y0, axis=-1, keepdims=True)
        dlt = y0 - mu
        rstd = lax.rsqrt(jnp.mean(dlt * dlt, axis=-1, keepdims=True) + EPS)
        yh = dlt * rstd
        y1 = yh * lg_ref[...] + lb_ref[...]
        sg = _sigmoid(y1)
        dy1 = dy2_ref[...] * (sg * (1.0 + y1 * (1.0 - sg)))
        dlg_ref[...] += _rowgroup_sum(dy1 * yh)
        dlb_ref[...] += _rowgroup_sum(dy1)
        dyh = dy1 * lg_ref[...]
        dy0 = rstd * (dyh - jnp.mean(dyh, axis=-1, keepdims=True)
                      - yh * jnp.mean(dyh * yh, axis=-1, keepdims=True))
        dcb_ref[...] += _rowgroup_sum(dy0)
        dy0_ref[...] = dy0

    row = pl.BlockSpec((tm, CONV_CH), lambda i: (i, 0))
    vec = pl.BlockSpec((1, CONV_CH), lambda i: (0, 0))
    acc = pl.BlockSpec((SUBLANES, CONV_CH), lambda i: (0, 0))
    accs = jax.ShapeDtypeStruct((SUBLANES, CONV_CH), F32)
    return pl.pallas_call(
        body, name=name, grid=(t // tm,),
        in_specs=[row, row, vec, vec], out_specs=[row, acc, acc, acc],
        out_shape=[jax.ShapeDtypeStruct((t, CONV_CH), F32), accs, accs, accs],
        compiler_params=_cp(("arbitrary",)),
    )(dy2, y0, lg, lb)


def _conv_bwd_taps(dy0, uug, w32, name, tm=512):
    t = uug.shape[0]
    hb = tm // HALO
    n_halo_blocks = t // HALO
    nt = t // tm

    def body(dm_ref, dn_ref, m_ref, h_ref, w_ref, duug_ref, dw_ref, a_sh, d_sh):
        i = pl.program_id(0)

        @pl.when(i == 0)
        def _():
            dw_ref[...] = jnp.zeros_like(dw_ref)

        u = m_ref[:, 0:CONV_CH]
        sg = _sigmoid(m_ref[:, CONV_CH:])
        a_sh[0, HALO:, :] = u * sg
        ah = h_ref[:, 0:CONV_CH] * _sigmoid(h_ref[:, CONV_CH:])
        a_sh[0, 0:HALO, :] = jnp.where(i > 0, ah, 0.0)
        d_sh[0, 0:tm, :] = dm_ref[...]
        d_sh[0, tm:, :] = jnp.where(i < nt - 1, dn_ref[...], 0.0)
        _fill_row_shifts(a_sh)
        _fill_row_shifts(d_sh)
        off = HALO - (CONV_WIDTH - 1)
        for c in range(tm // CONV_CHUNK):
            r0 = c * CONV_CHUNK
            da = jnp.zeros((CONV_CHUNK, CONV_CH), F32)
            for k in range(CONV_WIDTH):
                sh = CONV_WIDTH - 1 - k
                da = da + w_ref[k:k + 1, :] * _shifted_rows(d_sh, r0 + sh, CONV_CHUNK)
            uc = u[r0:r0 + CONV_CHUNK, :]
            sc = sg[r0:r0 + CONV_CHUNK, :]
            duug_ref[r0:r0 + CONV_CHUNK, 0:CONV_CH] = (da * sc).astype(BF16)
            duug_ref[r0:r0 + CONV_CHUNK, CONV_CH:] = (da * uc * sc * (1.0 - sc)).astype(BF16)
            dch = d_sh[0, r0:r0 + CONV_CHUNK, :]
            for k in range(CONV_WIDTH):
                prod = dch * _shifted_rows(a_sh, r0 + off + k, CONV_CHUNK)
                dw_ref[k * SUBLANES:(k + 1) * SUBLANES, :] += _rowgroup_sum(prod)

    return pl.pallas_call(
        body, name=name, grid=(nt,),
        in_specs=[pl.BlockSpec((tm, CONV_CH), lambda i: (i, 0)),
                  pl.BlockSpec((HALO, CONV_CH), lambda i: (jnp.minimum((i + 1) * hb, n_halo_blocks - 1), 0)),
                  pl.BlockSpec((tm, UUG_W), lambda i: (i, 0)),
                  pl.BlockSpec((HALO, UUG_W), lambda i: (jnp.maximum(i * hb - 1, 0), 0)),
                  pl.BlockSpec((HALO, CONV_CH), lambda i: (0, 0))],
        out_specs=[pl.BlockSpec((tm, UUG_W), lambda i: (i, 0)),
                   pl.BlockSpec((CONV_WIDTH * SUBLANES, CONV_CH), lambda i: (0, 0))],
        out_shape=[jax.ShapeDtypeStruct((t, UUG_W), BF16),
                   jax.ShapeDtypeStruct((CONV_WIDTH * SUBLANES, CONV_CH), F32)],
        scratch_shapes=[pltpu.VMEM((SUBLANES, tm + HALO, CONV_CH), F32),
                        pltpu.VMEM((SUBLANES, tm + HALO, CONV_CH), F32)],
        compiler_params=_cp(("arbitrary",)),
    )(dy0, dy0, uug, uug, w32)


def _merge_fwd(a_out, c_out, gg, name, tm=512):
    t, d = a_out.shape

    def body(a_ref, c_ref, g_ref, o_ref):
        o_ref[...] = (_sigmoid(g_ref[:, 0:d]) * a_ref[...] + _sigmoid(g_ref[:, d:]) * c_ref[...]).astype(BF16)

    row = pl.BlockSpec((tm, d), lambda i: (i, 0))
    return pl.pallas_call(
        body, name=name, grid=(t // tm,),
        in_specs=[row, row, pl.BlockSpec((tm, 2 * d), lambda i: (i, 0))], out_specs=row,
        out_shape=jax.ShapeDtypeStruct((t, d), BF16), compiler_params=_cp(("parallel",)),
    )(a_out, c_out, gg)


def _merge_bwd(dm, a_out, c_out, gg, name, tm=512):
    t, d = a_out.shape

    def body(dm_ref, a_ref, c_ref, g_ref, da_ref, dc_ref, dg_ref):
        dmv = dm_ref[...]
        sa = _sigmoid(g_ref[:, 0:d])
        sb = _sigmoid(g_ref[:, d:])
        da_ref[...] = dmv * sa
        dc_ref[...] = (dmv * sb).astype(BF16)
        dg_ref[:, 0:d] = (dmv * a_ref[...] * sa * (1.0 - sa)).astype(BF16)
        dg_ref[:, d:] = (dmv * c_ref[...] * sb * (1.0 - sb)).astype(BF16)

    row = pl.BlockSpec((tm, d), lambda i: (i, 0))
    wide = pl.BlockSpec((tm, 2 * d), lambda i: (i, 0))
    return pl.pallas_call(
        body, name=name, grid=(t // tm,),
        in_specs=[row, row, row, wide], out_specs=[row, row, wide],
        out_shape=[jax.ShapeDtypeStruct((t, d), F32), jax.ShapeDtypeStruct((t, d), BF16),
                   jax.ShapeDtypeStruct((t, 2 * d), BF16)],
        compiler_params=_cp(("parallel",)),
    )(dm, a_out, c_out, gg)


FF_TN = 1408


def _ffn_up_fwd(h2, wgu, name, tm=512):
    t, d = h2.shape
    nj = D_FF // FF_TN

    def body(h_ref, wg_ref, wu_ref, o_ref):
        hv = h_ref[...]
        g = jnp.dot(hv, wg_ref[...], preferred_element_type=F32)
        u = jnp.dot(hv, wu_ref[...], preferred_element_type=F32)
        o_ref[...] = ((g * _sigmoid(g)) * u).astype(BF16)

    return pl.pallas_call(
        body, name=name, grid=(nj, t // tm),
        in_specs=[pl.BlockSpec((tm, d), lambda j, i: (i, 0)),
                  pl.BlockSpec((d, FF_TN), lambda j, i: (0, j)),
                  pl.BlockSpec((d, FF_TN), lambda j, i: (0, j + nj))],
        out_specs=pl.BlockSpec((tm, FF_TN), lambda j, i: (i, j)),
        out_shape=jax.ShapeDtypeStruct((t, D_FF), BF16),
        compiler_params=_cp(("parallel", "parallel")),
    )(h2, wgu, wgu)


def _ffn_bwd_mid(h2, dx2, wgu, wd, name, tm=256, after=None):
    t, d = h2.shape
    nj = D_FF // FF_TN

    def body(*refs):
        h_ref, dx_ref, wg_ref, wu_ref, wd_ref = refs[:5]
        dg_ref, du_ref = refs[-2:]
        hv = h_ref[...]
        g = jnp.dot(hv, wg_ref[...], preferred_element_type=F32)
        u = jnp.dot(hv, wu_ref[...], preferred_element_type=F32)
        dact = lax.dot_general(dx_ref[...].astype(BF16), wd_ref[...], _DIMS["nt"], preferred_element_type=F32)
        sg = _sigmoid(g)
        silu = g * sg
        dg_ref[...] = (dact * u * (sg * (1.0 + g * (1.0 - sg)))).astype(BF16)
        du_ref[...] = (dact * silu).astype(BF16)

    return pl.pallas_call(
        body, name=name, grid=(nj, t // tm),
        in_specs=[pl.BlockSpec((tm, d), lambda j, i: (i, 0)), pl.BlockSpec((tm, d), lambda j, i: (i, 0)),
                  pl.BlockSpec((d, FF_TN), lambda j, i: (0, j)),
                  pl.BlockSpec((d, FF_TN), lambda j, i: (0, j + nj)),
                  pl.BlockSpec((FF_TN, d), lambda j, i: (j, 0))] + ([] if after is None else [_ANY]),
        out_specs=[pl.BlockSpec((tm, FF_TN), lambda j, i: (i, j)), pl.BlockSpec((tm, FF_TN), lambda j, i: (i, j))],
        out_shape=[jax.ShapeDtypeStruct((t, D_FF), BF16), jax.ShapeDtypeStruct((t, D_FF), BF16)],
        compiler_params=_cp(("parallel", "parallel")),
    )(h2, dx2, wgu, wgu, wd, *([] if after is None else [after]))


def _loss_head(y, target, name, tm=512):
    t, d = y.shape

    def body(y_ref, t_ref, dy_ref, loss_ref):
        @pl.when(pl.program_id(0) == 0)
        def _():
            loss_ref[...] = jnp.zeros_like(loss_ref)

        e = y_ref[...] - t_ref[...]
        dy_ref[...] = e * (1.0 / d)
        s = _rowgroup_sum(e * e)
        acc = s[:, 0:LANES]
        for c in range(1, d // LANES):
            acc = acc + s[:, c * LANES:(c + 1) * LANES]
        loss_ref[...] += acc * (0.5 / d)

    row = pl.BlockSpec((tm, d), lambda i: (i, 0))
    return pl.pallas_call(
        body, name=name, grid=(t // tm,),
        in_specs=[row, row], out_specs=[row, pl.BlockSpec((SUBLANES, LANES), lambda i: (0, 0))],
        out_shape=[jax.ShapeDtypeStruct((t, d), F32), jax.ShapeDtypeStruct((SUBLANES, LANES), F32)],
        compiler_params=_cp(("arbitrary",)),
    )(y, target)


def _exchange(arrays, scatter, name):
    n = len(arrays)

    def body(*refs):
        ins, outs = refs[:n], refs[n:2 * n]
        send_sems, recv_sems, local_sems = refs[2 * n:]
        x, y, c = lax.axis_index("x"), lax.axis_index("y"), lax.axis_index("c")
        me = 4 * x + 2 * y + c

        def peer(k):
            px, py, pc = x ^ ((k >> 2) & 1), y ^ ((k >> 1) & 1), c ^ (k & 1)
            return (px, py, pc), 4 * px + 2 * py + pc

        def src(a, dst_id):
            return ins[a].at[dst_id] if scatter else ins[a]

        locals_ = [pltpu.make_async_copy(src(a, me), outs[a].at[me], local_sems.at[a]) for a in range(n)]
        for cp in locals_:
            cp.start()
        sends = []
        for k in range(1, N_DEV):
            dev, pid = peer(k)
            for a in range(n):
                sends.append(pltpu.make_async_remote_copy(
                    src_ref=src(a, pid), dst_ref=outs[a].at[me],
                    send_sem=send_sems.at[a, k], recv_sem=recv_sems.at[a, k],
                    device_id=dev, device_id_type=pl.DeviceIdType.MESH))
        for cp in sends:
            cp.start()
        for k in range(1, N_DEV):
            dev, pid = peer(k)
            for a in range(n):
                pltpu.make_async_remote_copy(
                    src_ref=src(a, pid), dst_ref=outs[a].at[pid],
                    send_sem=send_sems.at[a, k], recv_sem=recv_sems.at[a, k],
                    device_id=dev, device_id_type=pl.DeviceIdType.MESH).wait_recv()
        for cp in sends:
            cp.wait_send()
        for cp in locals_:
            cp.wait()

    def out_shape(a):
        return jax.ShapeDtypeStruct(a.shape if scatter else (N_DEV,) + a.shape, a.dtype)

    anyspec = pl.BlockSpec(memory_space=pl.ANY)
    return pl.pallas_call(
        body, name=name,
        in_specs=[anyspec] * n, out_specs=[anyspec] * n,
        out_shape=[out_shape(a) for a in arrays],
        scratch_shapes=[pltpu.SemaphoreType.DMA((n, N_DEV)), pltpu.SemaphoreType.DMA((n, N_DEV)),
                        pltpu.SemaphoreType.DMA((n,))],
    )(*arrays)


_HBM = pl.BlockSpec(memory_space=pltpu.HBM)
_SEM = pl.BlockSpec(memory_space=pltpu.SEMAPHORE)
_ANY = pl.BlockSpec(memory_space=pl.ANY)
_EFFECT = pltpu.SideEffectType.DATAFLOW_SIDE_EFFECTING


def _mesh_peer(k):
    x, y, c = lax.axis_index("x"), lax.axis_index("y"), lax.axis_index("c")
    px, py, pc = x ^ ((k >> 2) & 1), y ^ ((k >> 1) & 1), c ^ (k & 1)
    return (px, py, pc), 4 * px + 2 * py + pc


def _exchange_start(arrays, scatter, name, after):
    n = len(arrays)
    lands = [lax.empty(a.shape if scatter else (N_DEV,) + a.shape, a.dtype) for a in arrays]

    def body(*refs):
        ins, land_refs = refs[:n], refs[n:2 * n]
        send_sems, recv_sems = refs[2 * n + 1], refs[2 * n + 2]
        token = refs[-1]
        _, me = _mesh_peer(0)
        for k in range(1, N_DEV):
            dev, pid = _mesh_peer(k)
            for a in range(n):
                pltpu.make_async_remote_copy(
                    src_ref=ins[a].at[pid] if scatter else ins[a], dst_ref=land_refs[a].at[me],
                    send_sem=send_sems.at[a * N_DEV + k], recv_sem=recv_sems.at[a * N_DEV + k],
                    device_id=dev, device_id_type=pl.DeviceIdType.MESH).start()
        token[...] = jnp.zeros_like(token)

    hbm_in = [pltpu.with_memory_space_constraint(a, pltpu.HBM) for a in list(arrays) + lands]
    outs = pl.pallas_call(
        body, name=name,
        in_specs=[_HBM] * (2 * n) + [_ANY],
        out_specs=[_SEM, _SEM] + [_HBM] * (2 * n) + [pl.BlockSpec(memory_space=pltpu.VMEM)],
        out_shape=[pltpu.SemaphoreType.DMA((n * N_DEV,)), pltpu.SemaphoreType.DMA((n * N_DEV,))]
        + [pltpu.HBM(a.shape, a.dtype) for a in hbm_in]
        + [jax.ShapeDtypeStruct((SUBLANES, LANES), F32)],
        input_output_aliases={i: 2 + i for i in range(2 * n)},
        compiler_params=pltpu.CompilerParams(has_side_effects=_EFFECT),
    )(*hbm_in, after)
    return outs[0], outs[1], outs[2:2 + n], outs[2 + n:2 + 2 * n], outs[-1]


def _exchange_wait(started, scatter, name, after):
    send_sems, recv_sems, srcs, lands, _ = started
    n = len(srcs)

    def body(*refs):
        ins, land_refs = refs[:n], refs[n:2 * n]
        send_sems, recv_sems = refs[2 * n], refs[2 * n + 1]
        copies = []
        for k in range(1, N_DEV):
            dev, pid = _mesh_peer(k)
            for a in range(n):
                copies.append(pltpu.make_async_remote_copy(
                    src_ref=ins[a].at[pid] if scatter else ins[a], dst_ref=land_refs[a].at[pid],
                    send_sem=send_sems.at[a * N_DEV + k], recv_sem=recv_sems.at[a * N_DEV + k],
                    device_id=dev, device_id_type=pl.DeviceIdType.MESH))
        for cp in copies:
            cp.wait_recv()
        for cp in copies:
            cp.wait_send()

    outs = pl.pallas_call(
        body, name=name,
        in_specs=[_HBM] * (2 * n) + [_SEM, _SEM, _ANY],
        out_specs=[_HBM] * (2 * n),
        out_shape=[pltpu.HBM(a.shape, a.dtype) for a in list(srcs) + list(lands)],
        input_output_aliases={i: i for i in range(2 * n)},
        compiler_params=pltpu.CompilerParams(has_side_effects=_EFFECT),
    )(*srcs, *lands, send_sems, recv_sems, after)
    me = 4 * lax.axis_index("x") + 2 * lax.axis_index("y") + lax.axis_index("c")
    filled = []
    for src, land in zip(outs[:n], outs[n:]):
        own = lax.dynamic_index_in_dim(src, me, 0, keepdims=True) if scatter else src[None]
        filled.append(lax.dynamic_update_slice(land, own, (me,) + (0,) * (land.ndim - 1)))
    return filled


def _adamw(parts, w, m, v, name, tr):
    nl = len(parts)
    r, c = parts[0].shape[1:]
    assert w.shape == (nl * r, c) and r % tr == 0, (name, w.shape, r, tr)
    nt = r // tr
    c1 = 1.0 - ADAM_B1 ** ADAM_STEP
    c2 = 1.0 - ADAM_B2 ** ADAM_STEP

    def body(*refs):
        p_refs = refs[:nl]
        w_ref, m_ref, v_ref, g_ref, d_ref, nm_ref, nv_ref = refs[nl:]
        layer = pl.program_id(0)
        for k in range(nl):
            @pl.when(layer == k)
            def _(p_ref=p_refs[k]):
                g = p_ref[0].astype(F32)
                for s in range(1, N_DEV):
                    g = g + p_ref[s].astype(F32)
                nm = ADAM_B1 * m_ref[...] + (1.0 - ADAM_B1) * g
                nv = ADAM_B2 * v_ref[...] + (1.0 - ADAM_B2) * (g * g)
                g_ref[...] = g
                nm_ref[...] = nm
                nv_ref[...] = nv
                d_ref[...] = -ADAM_LR * ((nm / c1) / (jnp.sqrt(nv / c2) + ADAM_EPS) + ADAM_WD * w_ref[...])

    def part_spec(k):
        return pl.BlockSpec((N_DEV, tr, c), lambda l, i: (0, jnp.where(l == k, i, 0), 0))

    row = pl.BlockSpec((tr, c), lambda l, i: (l * nt + i, 0))
    o = jax.ShapeDtypeStruct((nl * r, c), F32)
    return pl.pallas_call(
        body, name=name, grid=(nl, nt),
        in_specs=[part_spec(k) for k in range(nl)] + [row, row, row],
        out_specs=[row, row, row, row], out_shape=[o, o, o, o],
        compiler_params=_cp(("arbitrary", "arbitrary")),
    )(*parts, w, m, v)


def _with_token(gain, token):
    return gain if token is None else gain + token[0:1, 0:1]


def _layer_fwd(x, wl, sl, tabs, l, rest_fn=None, h=None, after=None, next_gain=None):
    ctab, stab, bd = tabs
    n = f"l{l}_"
    if h is None:
        h = _rmsnorm_fwd(x, sl["norm_mix"], n + "norm_mix")
    qkv = _mm(h, wl["w_qkv"], "nn", out_dtype=F32, name=n + "proj_qkv", tm=1024, tn=QKV_W, tk=D_MODEL, after=after)
    uug = _mm(h, wl["w_uug"], "nn", out_dtype=F32, name=n + "proj_uug", tm=1024, tn=UUG_W, tk=D_MODEL, after=after)
    gg = _mm(h, wl["w_gg"], "nn", out_dtype=F32, name=n + "proj_gg", tm=1024, tn=GG_W, tk=D_MODEL, after=after)
    qs, kd, vd = _qk_prep_fwd(qkv, sl["gq"], sl["gk"], ctab, stab, bd, n + "qk_prep")
    a_out = _attn_fwd(qs, kd, vd, sl["sinks"], n + "attn")
    y0, y2 = _conv_fwd(uug, sl["conv_w32"], sl["conv_b"], sl["ln_g"], sl["ln_b"], n + "conv")
    token = None
    if rest_fn is not None:
        rest, token = rest_fn(y2)
        wl = {**wl, **rest}
    c_out = _mm(y2, wl["w_conv_out"], "nn", out_dtype=F32, name=n + "conv_out", tm=1024, tn=D_MODEL, tk=CONV_CH)
    merged = _merge_fwd(a_out, c_out, gg, n + "merge")
    x1, h2 = _mm_rows([(merged, wl["w_out"])], "nn", name=n + "out_proj", tm=512, resid=x,
                      norm_fwd=_with_token(sl["norm_ffn"], token))
    act = _ffn_up_fwd(h2, wl["w_gate_up"], n + "ffn_up")
    if next_gain is None:
        x2 = _mm_rows([(act, wl["w_down"])], "nn", name=n + "ffn_down", tm=512, resid=x1)
        h_next = None
    else:
        x2, h_next = _mm_rows([(act, wl["w_down"])], "nn", name=n + "ffn_down", tm=512, resid=x1, norm_fwd=next_gain)
    saved = dict(x=x, h=h, qkv=qkv, uug=uug, gg=gg, qs=qs, kd=kd, vd=vd, a_out=a_out, y0=y0, y2=y2,
                 c_out=c_out, merged=merged, x1=x1, h2=h2, act=act)
    return x2, saved, wl, h_next


def _layer_bwd(dx2, sv, wl, sl, tabs, l, after=None, ffn_hook=None):
    ctab, stab, bd = tabs
    n = f"l{l}_b_"
    tk = 2048
    gw, gs = {}, {}
    gw["w_down"] = _mm(sv["act"], dx2, "tn", out_dtype=BF16, name=n + "dw_down", tm=FF_TN, tn=D_MODEL, tk=tk,
                       after=after)
    dg, du = _ffn_bwd_mid(sv["h2"], dx2, wl["w_gate_up"], wl["w_down"], n + "ffn_mid", after=after)
    gw["w_gate"] = _mm(sv["h2"], dg, "tn", out_dtype=BF16, name=n + "dw_gate", tm=D_MODEL, tn=FF_TN, tk=tk)
    gw["w_up"] = _mm(sv["h2"], du, "tn", out_dtype=BF16, name=n + "dw_up", tm=D_MODEL, tn=FF_TN, tk=tk)
    token = None if ffn_hook is None else ffn_hook(gw)
    dx1, gs["norm_ffn"] = _mm_rows([(dg, wl["w_gate"]), (du, wl["w_up"])], "nt", name=n + "dh2", tm=256,
                                   norm_bwd=(sv["x1"], _with_token(sl["norm_ffn"], token), dx2))
    dmerged = _mm(dx1, wl["w_out"], "nt", out_dtype=F32, name=n + "dmerged", tm=1024, tn=D_MODEL, tk=D_MODEL)
    gw["w_out"] = _mm(sv["merged"], dx1, "tn", out_dtype=BF16, name=n + "dw_out", tm=D_MODEL, tn=D_MODEL, tk=tk)
    da_out, dc_out, dgg = _merge_bwd(dmerged, sv["a_out"], sv["c_out"], sv["gg"], n + "merge")
    dy2 = _mm(dc_out, wl["w_conv_out"], "nt", out_dtype=F32, name=n + "dy2", tm=1024, tn=CONV_CH, tk=D_MODEL)
    gw["w_conv_out"] = _mm(sv["y2"], dc_out, "tn", out_dtype=BF16, name=n + "dw_conv_out", tm=CONV_CH, tn=D_MODEL,
                           tk=tk)
    dy0, gs["ln_g"], gs["ln_b"], gs["conv_b"] = _conv_bwd_ln(dy2, sv["y0"], sl["ln_g"], sl["ln_b"], n + "conv_ln")
    duug, gs["conv_w"] = _conv_bwd_taps(dy0, sv["uug"], sl["conv_w32"], n + "conv_taps")
    dqs, dkd, dvd, gs["sinks"] = _attn_bwd(sv["qs"], sv["kd"], sv["vd"], sl["sinks"], da_out, n + "attn")
    dqkv, gs["gq"], gs["gk"] = _qk_prep_bwd(dqs, dkd, dvd, sv["qkv"], sl["gq"], sl["gk"], ctab, stab, bd,
                                            n + "qk_prep")
    dx, gs["norm_mix"] = _mm_rows([(dqkv, wl["w_qkv"]), (duug, wl["w_uug"]), (dgg, wl["w_gg"])], "nt",
                                  name=n + "dh", tm=512, norm_bwd=(sv["x"], sl["norm_mix"], dx1))
    gw["w_qkv"] = _mm(sv["h"], dqkv, "tn", out_dtype=BF16, name=n + "dw_qkv", tm=D_MODEL, tn=QKV_W, tk=tk)
    gw["w_uug"] = _mm(sv["h"], duug, "tn", out_dtype=BF16, name=n + "dw_uug", tm=D_MODEL, tn=UUG_W, tk=tk)
    gw["w_gg"] = _mm(sv["h"], dgg, "tn", out_dtype=BF16, name=n + "dw_gg", tm=D_MODEL, tn=GG_W, tk=tk)
    return dx, gw, gs


def _cols_to_full(g):
    n, l, r, c = g.shape
    return jnp.transpose(g, (1, 2, 0, 3)).reshape(l, r, n * c)


def _rows_to_full(g):
    n, l, r, c = g.shape
    return jnp.transpose(g, (1, 0, 2, 3)).reshape(l, n * r, c)


def _full_to_cols(w):
    l, r, c = w.shape
    return jnp.transpose(w.reshape(l, r, N_DEV, c // N_DEV), (2, 0, 1, 3))


def _full_to_rows(w):
    l, r, c = w.shape
    return jnp.transpose(w.reshape(l, N_DEV, r // N_DEV, c), (1, 0, 2, 3))


SMALL = (("norm_mix", D_MODEL), ("q_norm", HEAD_DIM), ("k_norm", HEAD_DIM), ("sinks", N_HEADS),
         ("conv_w", CONV_WIDTH * CONV_CH), ("conv_b", CONV_CH), ("conv_ln_g", CONV_CH), ("conv_ln_b", CONV_CH),
         ("norm_ffn", D_MODEL))
SMALL_TOTAL = DEPTH * sum(s for _, s in SMALL)
SMALL_ROWS = -(-SMALL_TOTAL // (LANES * SUBLANES)) * SUBLANES


def _pack_small(d):
    flat = jnp.concatenate([d[k].reshape(-1).astype(F32) for k, _ in SMALL])
    flat = jnp.pad(flat, (0, SMALL_ROWS * LANES - SMALL_TOTAL))
    return flat.reshape(SMALL_ROWS, LANES)


def _unpack_small(buf, shapes):
    flat = buf.reshape(-1)
    out, o = {}, 0
    for k, s in SMALL:
        out[k] = flat[o:o + DEPTH * s].reshape(shapes[k])
        o += DEPTH * s
    return out


def kernel(x, norm_mix, w_in, q_norm, k_norm, sinks, conv_w, conv_b, conv_ln_g, conv_ln_b, w_conv_out, w_out, norm_ffn, w_gate_up, w_down, loss_target, m_norm_mix, m_w_in, m_q_norm, m_k_norm, m_sinks, m_conv_w, m_conv_b, m_conv_ln_g, m_conv_ln_b, m_w_conv_out, m_w_out, m_norm_ffn, m_w_gate_up, m_w_down, v_norm_mix, v_w_in, v_q_norm, v_k_norm, v_sinks, v_conv_w, v_conv_b, v_conv_ln_g, v_conv_ln_b, v_w_conv_out, v_w_out, v_norm_ffn, v_w_gate_up, v_w_down):
    t = x.shape[1]
    me = 4 * lax.axis_index("x") + 2 * lax.axis_index("y") + lax.axis_index("c")
    xs = x.reshape(t, D_MODEL)
    target = loss_target.reshape(t, D_MODEL)

    def shards_in(l):
        return [w_in[l].astype(BF16)]

    def shards_rest(l):
        return [w_conv_out[l].astype(BF16), w_out[l].astype(BF16), w_gate_up[l].astype(BF16), w_down[l].astype(BF16)]

    def weights_in(g_in):
        f_in = _cols_to_full(g_in[:, None])[0]
        return dict(w_qkv=f_in[:, :QKV_W], w_uug=f_in[:, QKV_W:QKV_W + UUG_W], w_gg=f_in[:, QKV_W + UUG_W:])

    def weights_rest(g):
        g_co, g_out, g_gu, g_dn = g
        f_gu = _cols_to_full(g_gu[:, None])[0]
        return dict(w_conv_out=_cols_to_full(g_co[:, None])[0], w_out=_rows_to_full(g_out[:, None])[0],
                    w_gate_up=f_gu, w_gate=f_gu[:, :D_FF], w_up=f_gu[:, D_FF:], w_down=_rows_to_full(g_dn[:, None])[0])

    g_in0, g_cw = _exchange(shards_in(0) + [conv_w], scatter=False, name="gather_w_in_0")
    f_cw = _cols_to_full(g_cw)
    tabs = _rope_tables(t) + (_block_diag_ones(),)

    def layer_small(l, token):
        return dict(norm_mix=_with_token(norm_mix[l][None], token), norm_ffn=norm_ffn[l][None],
                    gq=jnp.tile(q_norm[l], 2)[None], gk=jnp.tile(k_norm[l], 2)[None], sinks=sinks[l],
                    conv_w32=jnp.pad(f_cw[l], ((0, HALO - CONV_WIDTH), (0, 0))),
                    conv_b=conv_b[l][None], ln_g=conv_ln_g[l][None], ln_b=conv_ln_b[l][None])

    wls, sls, saved = [], [], []
    cur = xs
    flight = {"next": None}

    def start_next(l, after):
        flight["next"] = _exchange_start(shards_in(l + 1) + shards_rest(l + 1), False, f"gather_start_{l + 1}",
                                         after=after)
        return flight["next"][4]

    rest0 = _exchange_start(shards_rest(0), False, "gather_start_rest_0", after=g_in0)

    def rest_fn0(after):
        g = _exchange_wait(rest0, False, "gather_wait_rest_0", after=after)
        return weights_rest(g), start_next(0, g[0])

    gathered, h_next = None, None
    for l in range(DEPTH):
        if l == 0:
            w_first, token, rest_fn = weights_in(g_in0), rest0[4], rest_fn0
        else:
            w_first = {**weights_in(gathered[0]), **weights_rest(gathered[1:])}
            token = start_next(l, gathered[0]) if l + 1 < DEPTH else None
            rest_fn = None
        sls.append(layer_small(l, token if l == 0 else None))
        cur, sv, wl, h_next = _layer_fwd(cur, w_first, sls[l], tabs, l, rest_fn=rest_fn, h=h_next,
                                         after=None if l == 0 else token,
                                         next_gain=norm_mix[l + 1][None] if l + 1 < DEPTH else None)
        wls.append(wl)
        saved.append(sv)
        if l + 1 < DEPTH:
            gathered = _exchange_wait(flight["next"], False, f"gather_wait_{l + 1}", after=cur)
    dy, loss_part = _loss_head(cur, target, "loss_head")
    loss = lax.psum(jnp.sum(loss_part), ("x", "y", "c"))

    def slabs_ffn(gw):
        d_gu = jnp.concatenate([gw["w_gate"], gw["w_up"]], axis=1)[None]
        return [_full_to_cols(d_gu)[:, 0].astype(BF16), _full_to_rows(gw["w_down"][None])[:, 0].astype(BF16)]

    def slabs_mix(gw):
        d_in = jnp.concatenate([gw["w_qkv"], gw["w_uug"], gw["w_gg"]], axis=1)[None]
        return [_full_to_cols(d_in)[:, 0].astype(BF16), _full_to_cols(gw["w_conv_out"][None])[:, 0].astype(BF16),
                _full_to_rows(gw["w_out"][None])[:, 0].astype(BF16)]

    gss = [None] * DEPTH
    parts_ffn, parts_mix = [None] * DEPTH, [None] * DEPTH
    dcur = dy
    state = {"mix": None, "ffn": None}

    def make_ffn_hook(l):
        def hook(gw):
            sends = slabs_ffn(gw)
            after = sends[0]
            if state["mix"] is not None:
                parts_mix[l + 1] = _exchange_wait(state["mix"], True, f"scatter_wait_mix_{l + 1}", after=sends[0])
                after = parts_mix[l + 1][0]
            state["ffn"] = _exchange_start(sends, True, f"scatter_start_ffn_{l}", after=after)
            return state["ffn"][4]
        return hook

    for l in reversed(range(DEPTH)):
        dcur, gw, gss[l] = _layer_bwd(dcur, saved[l], wls[l], sls[l], tabs, l,
                                      after=None if state["mix"] is None else state["mix"][4],
                                      ffn_hook=make_ffn_hook(l))
        parts_ffn[l] = _exchange_wait(state["ffn"], True, f"scatter_wait_ffn_{l}", after=dcur)
        if l > 0:
            state["mix"] = _exchange_start(slabs_mix(gw), True, f"scatter_start_mix_{l}", after=parts_ffn[l][0])
        else:
            parts_mix[0] = _exchange(slabs_mix(gw), scatter=True, name="scatter_grads_mix_0")
    grad_x = dcur.reshape(x.shape)
    parts = [[parts_mix[l][0] for l in range(DEPTH)], [parts_mix[l][1] for l in range(DEPTH)],
             [parts_mix[l][2] for l in range(DEPTH)], [parts_ffn[l][0] for l in range(DEPTH)],
             [parts_ffn[l][1] for l in range(DEPTH)]]

    def update(p, w, m, v, name, tr):
        shp = w.shape
        r = shp[0] * shp[1]
        flat = lambda a: a.reshape(r, shp[2])
        outs = _adamw(p, flat(w), flat(m), flat(v), name, tr)
        return [o.reshape(shp) for o in outs]

    u_in = update(parts[0], w_in, m_w_in, v_w_in, "adamw_w_in", 256)
    u_co = update(parts[1], w_conv_out, m_w_conv_out, v_w_conv_out, "adamw_w_conv_out", 512)
    u_out = update(parts[2], w_out, m_w_out, v_w_out, "adamw_w_out", 128)
    u_gu = update(parts[3], w_gate_up, m_w_gate_up, v_w_gate_up, "adamw_w_gate_up", 256)
    u_dn = update(parts[4], w_down, m_w_down, v_w_down, "adamw_w_down", 176)

    def fold_rows(a):
        return jnp.sum(a, axis=0)

    def fold_heads(a):
        return jnp.sum(a, axis=0).reshape(2, HEAD_DIM).sum(axis=0)

    small_g = {
        "norm_mix": jnp.stack([fold_rows(gss[l]["norm_mix"]) for l in range(DEPTH)]),
        "q_norm": jnp.stack([fold_heads(gss[l]["gq"]) for l in range(DEPTH)]),
        "k_norm": jnp.stack([fold_heads(gss[l]["gk"]) for l in range(DEPTH)]),
        "sinks": jnp.stack([gss[l]["sinks"][0, :N_HEADS] for l in range(DEPTH)]),
        "conv_w": jnp.stack([gss[l]["conv_w"].reshape(CONV_WIDTH, SUBLANES, CONV_CH).sum(axis=1)
                             for l in range(DEPTH)]),
        "conv_b": jnp.stack([fold_rows(gss[l]["conv_b"]) for l in range(DEPTH)]),
        "conv_ln_g": jnp.stack([fold_rows(gss[l]["ln_g"]) for l in range(DEPTH)]),
        "conv_ln_b": jnp.stack([fold_rows(gss[l]["ln_b"]) for l in range(DEPTH)]),
        "norm_ffn": jnp.stack([fold_rows(gss[l]["norm_ffn"]) for l in range(DEPTH)]),
    }
    (small_parts,) = _exchange([_pack_small(small_g)], scatter=False, name="gather_small_grads")
    shapes = {"norm_mix": norm_mix.shape, "q_norm": q_norm.shape, "k_norm": k_norm.shape, "sinks": sinks.shape,
              "conv_w": (DEPTH, CONV_WIDTH, CONV_CH), "conv_b": conv_b.shape, "conv_ln_g": conv_ln_g.shape,
              "conv_ln_b": conv_ln_b.shape, "norm_ffn": norm_ffn.shape}

    def widen(a):
        z = jnp.zeros((DEPTH, CONV_WIDTH, N_DEV, CONV_CH // N_DEV), F32)
        z = lax.dynamic_update_slice(z, a[:, :, None, :], (0, 0, me, 0))
        return z.reshape(DEPTH, CONV_WIDTH, CONV_CH)

    sw = _pack_small(dict(norm_mix=norm_mix, q_norm=q_norm, k_norm=k_norm, sinks=sinks, conv_w=widen(conv_w),
                          conv_b=conv_b, conv_ln_g=conv_ln_g, conv_ln_b=conv_ln_b, norm_ffn=norm_ffn))
    sm = _pack_small(dict(norm_mix=m_norm_mix, q_norm=m_q_norm, k_norm=m_k_norm, sinks=m_sinks,
                          conv_w=widen(m_conv_w), conv_b=m_conv_b, conv_ln_g=m_conv_ln_g, conv_ln_b=m_conv_ln_b,
                          norm_ffn=m_norm_ffn))
    sv_ = _pack_small(dict(norm_mix=v_norm_mix, q_norm=v_q_norm, k_norm=v_k_norm, sinks=v_sinks,
                           conv_w=widen(v_conv_w), conv_b=v_conv_b,
                           conv_ln_g=v_conv_ln_g, conv_ln_b=v_conv_ln_b, norm_ffn=v_norm_ffn))
    s_outs = [_unpack_small(o, shapes) for o in _adamw([small_parts], sw, sm, sv_, "adamw_small", SMALL_ROWS)]

    def narrow(a):
        a4 = a.reshape(DEPTH, CONV_WIDTH, N_DEV, CONV_CH // N_DEV)
        return lax.dynamic_slice(a4, (0, 0, me, 0), (DEPTH, CONV_WIDTH, 1, CONV_CH // N_DEV)).reshape(
            DEPTH, CONV_WIDTH, CONV_CH // N_DEV)

    big = {"w_in": u_in, "w_conv_out": u_co, "w_out": u_out, "w_gate_up": u_gu, "w_down": u_dn}
    order = ["norm_mix", "w_in", "q_norm", "k_norm", "sinks", "conv_w", "conv_b", "conv_ln_g", "conv_ln_b",
             "w_conv_out", "w_out", "norm_ffn", "w_gate_up", "w_down"]
    outs = [loss, grad_x]
    for kind in range(4):
        for name in order:
            if name in big:
                outs.append(big[name][kind])
            elif name == "conv_w":
                outs.append(narrow(s_outs[kind][name]))
            else:
                outs.append(s_outs[kind][name])
    return tuple(outs)
```

```python
import functools
import math

import jax
import jax.numpy as jnp
from jax import lax
from jax.experimental import pallas as pl
from jax.experimental.pallas import tpu as pltpu

F32 = jnp.float32
BF16 = jnp.bfloat16

D_MODEL = 1024
DEPTH = 4
N_HEADS = 16
N_KV_HEADS = 2
HEAD_DIM = 64
ROT_DIM = HEAD_DIM // 4
ROPE_THETA = 500000.0
BLOCK = 128
CONV_CH = D_MODEL // 2
CONV_WIDTH = 31
D_FF = 2816
EPS = 1e-6
Q_W = N_HEADS * HEAD_DIM
KV_W = N_KV_HEADS * HEAD_DIM
QKV_W = Q_W + 2 * KV_W
UUG_W = 2 * CONV_CH
GG_W = 2 * D_MODEL
IN_W = QKV_W + UUG_W + GG_W
N_DEV = 8

ADAM_LR = 0.001
ADAM_B1 = 0.9
ADAM_B2 = 0.999
ADAM_EPS = 1e-08
ADAM_WD = 0.01
ADAM_STEP = 10

LANES = 128
SUBLANES = 8
HALO = 32
VMEM_LIMIT = 56 * 1024 * 1024
NEG = -1e30


def _cp(sem=None):
    return pltpu.CompilerParams(dimension_semantics=sem, vmem_limit_bytes=VMEM_LIMIT)


def _sigmoid(z):
    return 1.0 / (1.0 + jnp.exp(-z))


def _rowgroup_sum(z):
    r, c = z.shape
    return jnp.sum(z.reshape(r // SUBLANES, SUBLANES, c), axis=0)


_DIMS = {"nn": (((1,), (0,)), ((), ())), "nt": (((1,), (1,)), ((), ())), "tn": (((0,), (0,)), ((), ()))}


def _mm(a, b, mode, *, out_dtype, name, tm, tn, tk, resid=None, after=None):
    if mode == "nn":
        (m, k), (k2, n) = a.shape, b.shape
        a_spec = pl.BlockSpec((tm, tk), lambda i, j, s: (i, s))
        b_spec = pl.BlockSpec((tk, tn), lambda i, j, s: (s, j))
    elif mode == "nt":
        (m, k), (n, k2) = a.shape, b.shape
        a_spec = pl.BlockSpec((tm, tk), lambda i, j, s: (i, s))
        b_spec = pl.BlockSpec((tn, tk), lambda i, j, s: (j, s))
    else:
        (k, m), (k2, n) = a.shape, b.shape
        a_spec = pl.BlockSpec((tk, tm), lambda i, j, s: (s, i))
        b_spec = pl.BlockSpec((tk, tn), lambda i, j, s: (s, j))
    assert k == k2 and m % tm == 0 and n % tn == 0 and k % tk == 0, (name, a.shape, b.shape, tm, tn, tk)
    nk = k // tk
    dims = _DIMS[mode]
    has_resid = resid is not None

    def body(*refs):
        a_ref, b_ref = refs[0], refs[1]
        r_ref = refs[2] if has_resid else None
        o_ref = refs[-1] if nk == 1 else refs[-2]
        part = lax.dot_general(a_ref[...].astype(BF16), b_ref[...].astype(BF16), dims, preferred_element_type=F32)

        def finish(acc):
            if has_resid:
                acc = acc + r_ref[...]
            o_ref[...] = acc.astype(out_dtype)

        if nk == 1:
            finish(part)
            return
        acc_ref = refs[-1]
        s = pl.program_id(2)

        @pl.when(s == 0)
        def _():
            acc_ref[...] = part

        @pl.when(s > 0)
        def _():
            acc_ref[...] += part

        @pl.when(s == nk - 1)
        def _():
            finish(acc_ref[...])

    in_specs = [a_spec, b_spec]
    args = [a, b]
    if has_resid:
        in_specs.append(pl.BlockSpec((tm, tn), lambda i, j, s: (i, j)))
        args.append(resid)
    if after is not None:
        in_specs.append(_ANY)
        args.append(after)
    return pl.pallas_call(
        body, name=name, grid=(m // tm, n // tn, nk),
        in_specs=in_specs, out_specs=pl.BlockSpec((tm, tn), lambda i, j, s: (i, j)),
        out_shape=jax.ShapeDtypeStruct((m, n), out_dtype),
        scratch_shapes=[] if nk == 1 else [pltpu.VMEM((tm, tn), F32)],
        compiler_params=_cp(("parallel", "parallel", "arbitrary")),
    )(*args)


def _mm_rows(pairs, mode, *, name, tm, resid=None, norm_fwd=None, norm_bwd=None, after=None):
    m = pairs[0][0].shape[0]
    n = pairs[0][1].shape[1] if mode == "nn" else pairs[0][1].shape[0]
    assert m % tm == 0 and not (norm_fwd is not None and norm_bwd is not None), name
    dims = _DIMS[mode]
    np_ = len(pairs)
    row = pl.BlockSpec((tm, n), lambda i: (i, 0))
    vec = pl.BlockSpec((1, n), lambda i: (0, 0))
    in_specs, args = [], []
    for a, b in pairs:
        k = a.shape[1]
        assert a.shape[0] == m and (b.shape == (k, n) if mode == "nn" else b.shape == (n, k)), (name, a.shape, b.shape)
        in_specs += [pl.BlockSpec((tm, k), lambda i: (i, 0)), pl.BlockSpec(b.shape, lambda i: (0, 0))]
        args += [a, b]
    if resid is not None:
        in_specs.append(row)
        args.append(resid)
    if norm_fwd is not None:
        in_specs.append(vec)
        args.append(norm_fwd)
    if norm_bwd is not None:
        in_specs += [row, vec, row]
        args += list(norm_bwd)
    if after is not None:
        in_specs.append(_ANY)
        args.append(after)
    n_out = 1 if (norm_fwd is None and norm_bwd is None) else 2

    def body(*refs):
        outs = refs[len(refs) - n_out:]
        pos = 2 * np_
        acc = None
        for p in range(np_):
            part = lax.dot_general(refs[2 * p][...].astype(BF16), refs[2 * p + 1][...].astype(BF16), dims,
                                   preferred_element_type=F32)
            acc = part if acc is None else acc + part
        if resid is not None:
            acc = acc + refs[pos][...]
            pos += 1
        if norm_fwd is not None:
            r = lax.rsqrt(jnp.mean(acc * acc, axis=-1, keepdims=True) + EPS)
            outs[1][...] = ((acc * r) * refs[pos][...]).astype(BF16)
        if norm_bwd is not None:
            @pl.when(pl.program_id(0) == 0)
            def _():
                outs[1][...] = jnp.zeros_like(outs[1])

            xv = refs[pos][...]
            r = lax.rsqrt(jnp.mean(xv * xv, axis=-1, keepdims=True) + EPS)
            y = xv * r
            outs[1][...] += _rowgroup_sum(acc * y)
            dy = acc * refs[pos + 1][...]
            acc = refs[pos + 2][...] + r * (dy - y * jnp.mean(dy * y, axis=-1, keepdims=True))
        outs[0][...] = acc

    out_specs = [row]
    out_shape = [jax.ShapeDtypeStruct((m, n), F32)]
    if norm_fwd is not None:
        out_specs.append(row)
        out_shape.append(jax.ShapeDtypeStruct((m, n), BF16))
    if norm_bwd is not None:
        out_specs.append(pl.BlockSpec((SUBLANES, n), lambda i: (0, 0)))
        out_shape.append(jax.ShapeDtypeStruct((SUBLANES, n), F32))
    res = pl.pallas_call(
        body, name=name, grid=(m // tm,), in_specs=in_specs, out_specs=out_specs, out_shape=out_shape,
        compiler_params=_cp(("arbitrary",) if norm_bwd is not None else ("parallel",)),
    )(*args)
    return res[0] if n_out == 1 else res


def _rmsnorm_fwd(x, g, name, tm=512):
    t, d = x.shape

    def body(x_ref, g_ref, h_ref):
        xv = x_ref[...]
        r = lax.rsqrt(jnp.mean(xv * xv, axis=-1, keepdims=True) + EPS)
        h_ref[...] = ((xv * r) * g_ref[...]).astype(BF16)

    return pl.pallas_call(
        body, name=name, grid=(t // tm,),
        in_specs=[pl.BlockSpec((tm, d), lambda i: (i, 0)), pl.BlockSpec((1, d), lambda i: (0, 0))],
        out_specs=pl.BlockSpec((tm, d), lambda i: (i, 0)),
        out_shape=jax.ShapeDtypeStruct((t, d), BF16),
        compiler_params=_cp(("parallel",)),
    )(x, g)


def _rmsnorm_bwd(dh, x, g, resid, name, tm=512):
    t, d = x.shape

    def body(dh_ref, x_ref, g_ref, r_ref, dx_ref, dg_ref):
        @pl.when(pl.program_id(0) == 0)
        def _():
            dg_ref[...] = jnp.zeros_like(dg_ref)

        xv = x_ref[...]
        dhv = dh_ref[...]
        r = lax.rsqrt(jnp.mean(xv * xv, axis=-1, keepdims=True) + EPS)
        y = xv * r
        dg_ref[...] += _rowgroup_sum(dhv * y)
        dy = dhv * g_ref[...]
        dx_ref[...] = r_ref[...] + r * (dy - y * jnp.mean(dy * y, axis=-1, keepdims=True))

    row = pl.BlockSpec((tm, d), lambda i: (i, 0))
    return pl.pallas_call(
        body, name=name, grid=(t // tm,),
        in_specs=[row, row, pl.BlockSpec((1, d), lambda i: (0, 0)), row],
        out_specs=[row, pl.BlockSpec((SUBLANES, d), lambda i: (0, 0))],
        out_shape=[jax.ShapeDtypeStruct((t, d), F32), jax.ShapeDtypeStruct((SUBLANES, d), F32)],
        compiler_params=_cp(("arbitrary",)),
    )(dh, x, g, resid)


def _seg_sum(z, bd):
    hi = z.astype(BF16)
    lo = (z - hi.astype(F32)).astype(BF16)
    return jnp.dot(hi, bd, preferred_element_type=F32) + jnp.dot(lo, bd, preferred_element_type=F32)


def _partner(z, lane64):
    return jnp.where(lane64 < ROT_DIM // 2, pltpu.roll(z, LANES - ROT_DIM // 2, 1), pltpu.roll(z, ROT_DIM // 2, 1))


def _rope_tables(t):
    inv_freq = ROPE_THETA ** (-jnp.arange(0, ROT_DIM, 2, dtype=F32) / ROT_DIM)
    ang = jnp.arange(t, dtype=F32)[:, None] * inv_freq[None, :]
    cos, sin = jnp.cos(ang), jnp.sin(ang)
    c64 = jnp.concatenate([cos, cos, jnp.ones((t, HEAD_DIM - ROT_DIM), F32)], axis=1)
    s64 = jnp.concatenate([-sin, sin, jnp.zeros((t, HEAD_DIM - ROT_DIM), F32)], axis=1)
    return jnp.tile(c64, (1, 2)), jnp.tile(s64, (1, 2))


def _block_diag_ones():
    r = lax.broadcasted_iota(jnp.int32, (LANES, LANES), 0) // HEAD_DIM
    c = lax.broadcasted_iota(jnp.int32, (LANES, LANES), 1) // HEAD_DIM
    return (r == c).astype(BF16)


def _qk_prep_fwd(qkv, gq, gk, ctab, stab, bd, name, tm=512):
    t = qkv.shape[0]
    scale = HEAD_DIM ** -0.5
    n_qg = Q_W // LANES

    def body(q_ref, kv_ref, gq_ref, gk_ref, c_ref, s_ref, bd_ref, qs_ref, kd_ref, vd_ref):
        lane = lax.broadcasted_iota(jnp.int32, (tm, LANES), 1)
        lane64 = lane % HEAD_DIM
        lo_half = lane < HEAD_DIM
        cv, sv, bdv = c_ref[...], s_ref[...], bd_ref[...]

        def norm_rope(xg, g):
            r = lax.rsqrt(_seg_sum(xg * xg, bdv) * (1.0 / HEAD_DIM) + EPS)
            yn = (xg * r) * g
            return yn * cv + _partner(yn, lane64) * sv

        zero = jnp.zeros((BLOCK, LANES), BF16)
        lo_blk = lo_half[0:BLOCK]
        for c in range(n_qg):
            xg = q_ref[:, c * LANES:(c + 1) * LANES]
            qn = (norm_rope(xg, gq_ref[...]) * scale).astype(BF16)
            for b in range(tm // BLOCK):
                rows = qn[b * BLOCK:(b + 1) * BLOCK]
                qs_ref[b, 2 * c] = jnp.where(lo_blk, rows, zero)
                qs_ref[b, 2 * c + 1] = jnp.where(lo_blk, zero, rows)
        kk = norm_rope(kv_ref[:, 0:LANES], gk_ref[...])
        kr = pltpu.roll(kk, HEAD_DIM, 1)
        kd_ref[:, 0:LANES] = jnp.where(lo_half, kk, kr).astype(BF16)
        kd_ref[:, LANES:2 * LANES] = jnp.where(lo_half, kr, kk).astype(BF16)
        vv = kv_ref[:, LANES:2 * LANES]
        vr = pltpu.roll(vv, HEAD_DIM, 1)
        vd_ref[:, 0:LANES] = jnp.where(lo_half, vv, vr).astype(BF16)
        vd_ref[:, LANES:2 * LANES] = jnp.where(lo_half, vr, vv).astype(BF16)

    vec = pl.BlockSpec((1, LANES), lambda i: (0, 0))
    tab = pl.BlockSpec((tm, LANES), lambda i: (i, 0))
    return pl.pallas_call(
        body, name=name, grid=(t // tm,),
        in_specs=[pl.BlockSpec((tm, Q_W), lambda i: (i, 0)),
                  pl.BlockSpec((tm, 2 * KV_W), lambda i: (i, Q_W // (2 * KV_W))),
                  vec, vec, tab, tab, pl.BlockSpec((LANES, LANES), lambda i: (0, 0))],
        out_specs=[pl.BlockSpec((tm // BLOCK, N_HEADS, BLOCK, LANES), lambda i: (i, 0, 0, 0)),
                   pl.BlockSpec((tm, 2 * LANES), lambda i: (i, 0)),
                   pl.BlockSpec((tm, 2 * LANES), lambda i: (i, 0))],
        out_shape=[jax.ShapeDtypeStruct((t // BLOCK, N_HEADS, BLOCK, LANES), BF16),
                   jax.ShapeDtypeStruct((t, 2 * LANES), BF16), jax.ShapeDtypeStruct((t, 2 * LANES), BF16)],
        compiler_params=_cp(("parallel",)),
    )(qkv, qkv, gq, gk, ctab, stab, bd)


def _qk_prep_bwd(dqs, dkd, dvd, qkv, gq, gk, ctab, stab, bd, name, tm=512):
    t = qkv.shape[0]
    scale = HEAD_DIM ** -0.5
    n_qg = Q_W // LANES

    def body(dqs_ref, dkd_ref, dvd_ref, q_ref, kv_ref, gq_ref, gk_ref, c_ref, s_ref, bd_ref,
             dqkv_ref, dgq_ref, dgk_ref):
        @pl.when(pl.program_id(0) == 0)
        def _():
            dgq_ref[...] = jnp.zeros_like(dgq_ref)
            dgk_ref[...] = jnp.zeros_like(dgk_ref)

        lane = lax.broadcasted_iota(jnp.int32, (tm, LANES), 1)
        lane64 = lane % HEAD_DIM
        lo_half = lane < HEAD_DIM
        cv, sv, bdv = c_ref[...], s_ref[...], bd_ref[...]

        def bwd(xg, g, dout):
            r = lax.rsqrt(_seg_sum(xg * xg, bdv) * (1.0 / HEAD_DIM) + EPS)
            y = xg * r
            dyn = dout * cv + jnp.where(lane64 < ROT_DIM, _partner(dout * sv, lane64), 0.0)
            dy = dyn * g
            dx = r * (dy - y * (_seg_sum(dy * y, bdv) * (1.0 / HEAD_DIM)))
            return dx, _rowgroup_sum(dyn * y)

        dgq = jnp.zeros((SUBLANES, LANES), F32)
        for c in range(n_qg):
            sl = slice(c * LANES, (c + 1) * LANES)
            dx, dg = bwd(q_ref[:, sl], gq_ref[...], dqs_ref[:, sl] * scale)
            dqkv_ref[:, sl] = dx.astype(BF16)
            dgq = dgq + dg
        dgq_ref[...] += dgq
        dk = jnp.where(lo_half, dkd_ref[:, 0:LANES], dkd_ref[:, LANES:2 * LANES])
        dx, dg = bwd(kv_ref[:, 0:LANES], gk_ref[...], dk)
        dqkv_ref[:, Q_W:Q_W + LANES] = dx.astype(BF16)
        dgk_ref[...] += dg
        dv = jnp.where(lo_half, dvd_ref[:, 0:LANES], dvd_ref[:, LANES:2 * LANES])
        dqkv_ref[:, Q_W + LANES:Q_W + 2 * LANES] = dv.astype(BF16)

    vec = pl.BlockSpec((1, LANES), lambda i: (0, 0))
    tab = pl.BlockSpec((tm, LANES), lambda i: (i, 0))
    wide = pl.BlockSpec((tm, 2 * LANES), lambda i: (i, 0))
    acc = pl.BlockSpec((SUBLANES, LANES), lambda i: (0, 0))
    return pl.pallas_call(
        body, name=name, grid=(t // tm,),
        in_specs=[pl.BlockSpec((tm, Q_W), lambda i: (i, 0)), wide, wide,
                  pl.BlockSpec((tm, Q_W), lambda i: (i, 0)),
                  pl.BlockSpec((tm, 2 * KV_W), lambda i: (i, Q_W // (2 * KV_W))),
                  vec, vec, tab, tab, pl.BlockSpec((LANES, LANES), lambda i: (0, 0))],
        out_specs=[pl.BlockSpec((tm, QKV_W), lambda i: (i, 0)), acc, acc],
        out_shape=[jax.ShapeDtypeStruct((t, QKV_W), BF16), jax.ShapeDtypeStruct((SUBLANES, LANES), F32),
                   jax.ShapeDtypeStruct((SUBLANES, LANES), F32)],
        compiler_params=_cp(("arbitrary",)),
    )(dqs, dkd, dvd, qkv, qkv, gq, gk, ctab, stab, bd)


GROUP = N_HEADS // N_KV_HEADS
GROUP_ROWS = GROUP * BLOCK


def _attn_masks():
    row = lax.broadcasted_iota(jnp.int32, (BLOCK, BLOCK), 0)
    col = lax.broadcasted_iota(jnp.int32, (BLOCK, BLOCK), 1)
    return col <= row, col < HEAD_DIM


def _window_softmax(s_c, s_p, is_cur, has_prev, sink):
    s = jnp.where(is_cur, s_c, jnp.where(has_prev, s_p, NEG))
    m = jnp.maximum(jnp.max(s, axis=-1, keepdims=True), sink)
    e = jnp.exp(s - m)
    e_s = jnp.exp(sink - m)
    inv = 1.0 / (jnp.sum(e, axis=-1, keepdims=True) + e_s)
    return e * inv, e_s * inv


def _attn_fwd(qs, kd, vd, sinks, name):
    nb = qs.shape[0]
    t = nb * BLOCK

    def body(sink_ref, q_ref, kc_ref, kp_ref, vc_ref, vp_ref, o_ref, pc_scr, pp_scr):
        has_prev = pl.program_id(0) > 0
        is_cur, lo_half = _attn_masks()
        for j in range(N_KV_HEADS):
            ks = slice(j * LANES, (j + 1) * LANES)
            qg = q_ref[0, j * GROUP:(j + 1) * GROUP].reshape(GROUP_ROWS, LANES)
            s_c = lax.dot_general(qg, kc_ref[:, ks], _DIMS["nt"], preferred_element_type=F32)
            s_p = lax.dot_general(qg, kp_ref[:, ks], _DIMS["nt"], preferred_element_type=F32)
            for g in range(GROUP):
                rs = slice(g * BLOCK, (g + 1) * BLOCK)
                p, _ = _window_softmax(s_c[rs], s_p[rs], is_cur, has_prev, sink_ref[j * GROUP + g])
                pc_scr[rs, :] = jnp.where(is_cur, p, 0.0).astype(BF16)
                pp_scr[rs, :] = jnp.where(is_cur, 0.0, p).astype(BF16)
            o2 = (jnp.dot(pc_scr[...], vc_ref[:, ks], preferred_element_type=F32)
                  + jnp.dot(pp_scr[...], vp_ref[:, ks], preferred_element_type=F32))
            for pp in range(GROUP // 2):
                c0 = (j * (GROUP // 2) + pp) * LANES
                o_ref[:, c0:c0 + LANES] = jnp.where(lo_half, o2[2 * pp * BLOCK:(2 * pp + 1) * BLOCK],
                                                    o2[(2 * pp + 1) * BLOCK:(2 * pp + 2) * BLOCK])

    cur = lambda i: (i, 0)
    prev = lambda i: (jnp.maximum(i - 1, 0), 0)
    kvs = (BLOCK, 2 * LANES)
    return pl.pallas_call(
        body, name=name, grid=(nb,),
        in_specs=[pl.BlockSpec(memory_space=pltpu.SMEM),
                  pl.BlockSpec((1, N_HEADS, BLOCK, LANES), lambda i: (i, 0, 0, 0)),
                  pl.BlockSpec(kvs, cur), pl.BlockSpec(kvs, prev), pl.BlockSpec(kvs, cur), pl.BlockSpec(kvs, prev)],
        out_specs=pl.BlockSpec((BLOCK, Q_W), cur),
        out_shape=jax.ShapeDtypeStruct((t, Q_W), F32),
        scratch_shapes=[pltpu.VMEM((GROUP_ROWS, LANES), BF16), pltpu.VMEM((GROUP_ROWS, LANES), BF16)],
        compiler_params=_cp(("parallel",)),
    )(sinks, qs, kd, kd, vd, vd)


def _attn_bwd(qs, kd, vd, sinks, do, name):
    nb = qs.shape[0]
    t = nb * BLOCK

    def body(sink_ref, q_ref, do_ref, kc_ref, kp_ref, vc_ref, vp_ref,
             dq_ref, dk_ref, dv_ref, dsink_ref,
             carry_k, carry_v, dsink_acc, do_scr, pc_scr, pp_scr, dsc_scr, dsp_scr):
        i = pl.program_id(0)

        @pl.when(i == 0)
        def _():
            carry_k[...] = jnp.zeros_like(carry_k)
            carry_v[...] = jnp.zeros_like(carry_v)
            dsink_acc[...] = jnp.zeros_like(dsink_acc)

        @pl.when(i < nb)
        def _():
            has_prev = i > 0
            is_cur, lo_half = _attn_masks()
            srow = lax.broadcasted_iota(jnp.int32, (SUBLANES, LANES), 0)
            scol = lax.broadcasted_iota(jnp.int32, (SUBLANES, LANES), 1)
            dsink = jnp.zeros((SUBLANES, LANES), F32)
            for j in range(N_KV_HEADS):
                ks = slice(j * LANES, (j + 1) * LANES)
                kc, kp, vc, vp = kc_ref[:, ks], kp_ref[:, ks], vc_ref[:, ks], vp_ref[:, ks]
                qg = q_ref[0, j * GROUP:(j + 1) * GROUP].reshape(GROUP_ROWS, LANES)
                for pp in range(GROUP // 2):
                    c0 = (j * (GROUP // 2) + pp) * LANES
                    dop = do_ref[:, c0:c0 + LANES]
                    do_scr[2 * pp * BLOCK:(2 * pp + 1) * BLOCK, :] = jnp.where(lo_half, dop, 0.0).astype(BF16)
                    do_scr[(2 * pp + 1) * BLOCK:(2 * pp + 2) * BLOCK, :] = jnp.where(lo_half, 0.0, dop).astype(BF16)
                dog = do_scr[...]
                s_c = lax.dot_general(qg, kc, _DIMS["nt"], preferred_element_type=F32)
                s_p = lax.dot_general(qg, kp, _DIMS["nt"], preferred_element_type=F32)
                dp_c = lax.dot_general(dog, vc, _DIMS["nt"], preferred_element_type=F32)
                dp_p = lax.dot_general(dog, vp, _DIMS["nt"], preferred_element_type=F32)
                for g in range(GROUP):
                    rs = slice(g * BLOCK, (g + 1) * BLOCK)
                    h = j * GROUP + g
                    p, p_s = _window_softmax(s_c[rs], s_p[rs], is_cur, has_prev, sink_ref[h])
                    dp = jnp.where(is_cur, dp_c[rs], dp_p[rs])
                    delta = jnp.sum(p * dp, axis=-1, keepdims=True)
                    ds = p * (dp - delta)
                    dsv = -jnp.sum(p_s * delta, axis=0, keepdims=True)
                    dsink = dsink + jnp.where(jnp.logical_and(srow == 0, scol == h), dsv, 0.0)
                    pc_scr[rs, :] = jnp.where(is_cur, p, 0.0).astype(BF16)
                    pp_scr[rs, :] = jnp.where(is_cur, 0.0, p).astype(BF16)
                    dsc_scr[rs, :] = jnp.where(is_cur, ds, 0.0).astype(BF16)
                    dsp_scr[rs, :] = jnp.where(is_cur, 0.0, ds).astype(BF16)
                dsc, dsp = dsc_scr[...], dsp_scr[...]
                dq2 = jnp.dot(dsc, kc, preferred_element_type=F32) + jnp.dot(dsp, kp, preferred_element_type=F32)
                for pp in range(GROUP // 2):
                    c0 = (j * (GROUP // 2) + pp) * LANES
                    dq_ref[:, c0:c0 + LANES] = jnp.where(lo_half, dq2[2 * pp * BLOCK:(2 * pp + 1) * BLOCK],
                                                         dq2[(2 * pp + 1) * BLOCK:(2 * pp + 2) * BLOCK])
                dk_c = lax.dot_general(dsc, qg, _DIMS["tn"], preferred_element_type=F32)
                dk_p = lax.dot_general(dsp, qg, _DIMS["tn"], preferred_element_type=F32)
                dv_c = lax.dot_general(pc_scr[...], dog, _DIMS["tn"], preferred_element_type=F32)
                dv_p = lax.dot_general(pp_scr[...], dog, _DIMS["tn"], preferred_element_type=F32)
                dk_ref[:, ks] = carry_k[:, ks] + dk_p + pltpu.roll(dk_p, HEAD_DIM, 1)
                dv_ref[:, ks] = carry_v[:, ks] + dv_p + pltpu.roll(dv_p, HEAD_DIM, 1)
                carry_k[:, ks] = dk_c + pltpu.roll(dk_c, HEAD_DIM, 1)
                carry_v[:, ks] = dv_c + pltpu.roll(dv_c, HEAD_DIM, 1)
            dsink_acc[...] += dsink

        @pl.when(i == nb)
        def _():
            dk_ref[...] = carry_k[...]
            dv_ref[...] = carry_v[...]
            dsink_ref[...] = dsink_acc[...]

    last = nb - 1
    cur = lambda i: (jnp.minimum(i, last), 0)
    prev = lambda i: (jnp.clip(i - 1, 0, last), 0)
    kvs = (BLOCK, 2 * LANES)
    stk = pltpu.VMEM((GROUP_ROWS, LANES), BF16)
    return pl.pallas_call(
        body, name=name, grid=(nb + 1,),
        in_specs=[pl.BlockSpec(memory_space=pltpu.SMEM),
                  pl.BlockSpec((1, N_HEADS, BLOCK, LANES), lambda i: (jnp.minimum(i, last), 0, 0, 0)),
                  pl.BlockSpec((BLOCK, Q_W), cur),
                  pl.BlockSpec(kvs, cur), pl.BlockSpec(kvs, prev), pl.BlockSpec(kvs, cur), pl.BlockSpec(kvs, prev)],
        out_specs=[pl.BlockSpec((BLOCK, Q_W), cur), pl.BlockSpec(kvs, prev), pl.BlockSpec(kvs, prev),
                   pl.BlockSpec((SUBLANES, LANES), lambda i: (0, 0))],
        out_shape=[jax.ShapeDtypeStruct((t, Q_W), F32), jax.ShapeDtypeStruct((t, 2 * LANES), F32),
                   jax.ShapeDtypeStruct((t, 2 * LANES), F32), jax.ShapeDtypeStruct((SUBLANES, LANES), F32)],
        scratch_shapes=[pltpu.VMEM(kvs, F32), pltpu.VMEM(kvs, F32), pltpu.VMEM((SUBLANES, LANES), F32),
                        stk, stk, stk, stk, stk],
        compiler_params=_cp(("arbitrary",)),
    )(sinks, qs, do, kd, kd, vd, vd)


CONV_CHUNK = 64


def _fill_row_shifts(sh):
    rows = sh.shape[1] - SUBLANES
    for r in range(1, SUBLANES):
        sh[r, 0:rows, :] = sh[0, r:r + rows, :]


def _shifted_rows(sh, start, size):
    q, r = divmod(start, SUBLANES)
    return sh[r, q * SUBLANES:q * SUBLANES + size, :]


def _conv_fwd(uug, w32, cb, lg, lb, name, tm=512):
    t = uug.shape[0]
    hb = tm // HALO

    def body(m_ref, h_ref, w_ref, cb_ref, lg_ref, lb_ref, y0_ref, y2_ref, a_sh):
        i = pl.program_id(0)
        a_sh[0, HALO:, :] = m_ref[:, 0:CONV_CH] * _sigmoid(m_ref[:, CONV_CH:])
        ah = h_ref[:, 0:CONV_CH] * _sigmoid(h_ref[:, CONV_CH:])
        a_sh[0, 0:HALO, :] = jnp.where(i > 0, ah, 0.0)
        _fill_row_shifts(a_sh)
        off = HALO - (CONV_WIDTH - 1)
        for c in range(tm // CONV_CHUNK):
            r0 = c * CONV_CHUNK
            acc = jnp.zeros((CONV_CHUNK, CONV_CH), F32)
            for k in range(CONV_WIDTH):
                acc = acc + w_ref[k:k + 1, :] * _shifted_rows(a_sh, r0 + off + k, CONV_CHUNK)
            y0 = acc + cb_ref[...]
            y0_ref[r0:r0 + CONV_CHUNK, :] = y0
            mu = jnp.mean(y0, axis=-1, keepdims=True)
            dlt = y0 - mu
            rstd = lax.rsqrt(jnp.mean(dlt * dlt, axis=-1, keepdims=True) + EPS)
            y1 = (dlt * rstd) * lg_ref[...] + lb_ref[...]
            y2_ref[r0:r0 + CONV_CHUNK, :] = (y1 * _sigmoid(y1)).astype(BF16)

    vec = pl.BlockSpec((1, CONV_CH), lambda i: (0, 0))
    return pl.pallas_call(
        body, name=name, grid=(t // tm,),
        in_specs=[pl.BlockSpec((tm, UUG_W), lambda i: (i, 0)),
                  pl.BlockSpec((HALO, UUG_W), lambda i: (jnp.maximum(i * hb - 1, 0), 0)),
                  pl.BlockSpec((HALO, CONV_CH), lambda i: (0, 0)), vec, vec, vec],
        out_specs=[pl.BlockSpec((tm, CONV_CH), lambda i: (i, 0)), pl.BlockSpec((tm, CONV_CH), lambda i: (i, 0))],
        out_shape=[jax.ShapeDtypeStruct((t, CONV_CH), F32), jax.ShapeDtypeStruct((t, CONV_CH), BF16)],
        scratch_shapes=[pltpu.VMEM((SUBLANES, tm + HALO, CONV_CH), F32)],
        compiler_params=_cp(("parallel",)),
    )(uug, uug, w32, cb, lg, lb)


def _conv_bwd_ln(dy2, y0, lg, lb, name, tm=512):
    t = y0.shape[0]

    def body(dy2_ref, y0_ref, lg_ref, lb_ref, dy0_ref, dlg_ref, dlb_ref, dcb_ref):
        @pl.when(pl.program_id(0) == 0)
        def _():
            dlg_ref[...] = jnp.zeros_like(dlg_ref)
            dlb_ref[...] = jnp.zeros_like(dlb_ref)
            dcb_ref[...] = jnp.zeros_like(dcb_ref)

        y0 = y0_ref[...]
        mu = jnp.mean(y0, axis=-1, keepdims=True)
        dlt = y0 - mu
        rstd = lax.rsqrt(jnp.mean(dlt * dlt, axis=-1, keepdims=True) + EPS)
        yh = dlt * rstd
        y1 = yh * lg_ref[...] + lb_ref[...]
        sg = _sigmoid(y1)
        dy1 = dy2_ref[...] * (sg * (1.0 + y1 * (1.0 - sg)))
        dlg_ref[...] += _rowgroup_sum(dy1 * yh)
        dlb_ref[...] += _rowgroup_sum(dy1)
        dyh = dy1 * lg_ref[...]
        dy0 = rstd * (dyh - jnp.mean(dyh, axis=-1, keepdims=True)
                      - yh * jnp.mean(dyh * yh, axis=-1, keepdims=True))
        dcb_ref[...] += _rowgroup_sum(dy0)
        dy0_ref[...] = dy0

    row = pl.BlockSpec((tm, CONV_CH), lambda i: (i, 0))
    vec = pl.BlockSpec((1, CONV_CH), lambda i: (0, 0))
    acc = pl.BlockSpec((SUBLANES, CONV_CH), lambda i: (0, 0))
    accs = jax.ShapeDtypeStruct((SUBLANES, CONV_CH), F32)
    return pl.pallas_call(
        body, name=name, grid=(t // tm,),
        in_specs=[row, row, vec, vec], out_specs=[row, acc, acc, acc],
        out_shape=[jax.ShapeDtypeStruct((t, CONV_CH), F32), accs, accs, accs],
        compiler_params=_cp(("arbitrary",)),
    )(dy2, y0, lg, lb)


def _conv_bwd_taps(dy0, uug, w32, name, tm=512):
    t = uug.shape[0]
    hb = tm // HALO
    n_halo_blocks = t // HALO
    nt = t // tm

    def body(dm_ref, dn_ref, m_ref, h_ref, w_ref, duug_ref, dw_ref, a_sh, d_sh):
        i = pl.program_id(0)

        @pl.when(i == 0)
        def _():
            dw_ref[...] = jnp.zeros_like(dw_ref)

        u = m_ref[:, 0:CONV_CH]
        sg = _sigmoid(m_ref[:, CONV_CH:])
        a_sh[0, HALO:, :] = u * sg
        ah = h_ref[:, 0:CONV_CH] * _sigmoid(h_ref[:, CONV_CH:])
        a_sh[0, 0:HALO, :] = jnp.where(i > 0, ah, 0.0)
        d_sh[0, 0:tm, :] = dm_ref[...]
        d_sh[0, tm:, :] = jnp.where(i < nt - 1, dn_ref[...], 0.0)
        _fill_row_shifts(a_sh)
        _fill_row_shifts(d_sh)
        off = HALO - (CONV_WIDTH - 1)
        for c in range(tm // CONV_CHUNK):
            r0 = c * CONV_CHUNK
            da = jnp.zeros((CONV_CHUNK, CONV_CH), F32)
            for k in range(CONV_WIDTH):
                sh = CONV_WIDTH - 1 - k
                da = da + w_ref[k:k + 1, :] * _shifted_rows(d_sh, r0 + sh, CONV_CHUNK)
            uc = u[r0:r0 + CONV_CHUNK, :]
            sc = sg[r0:r0 + CONV_CHUNK, :]
            duug_ref[r0:r0 + CONV_CHUNK, 0:CONV_CH] = (da * sc).astype(BF16)
            duug_ref[r0:r0 + CONV_CHUNK, CONV_CH:] = (da * uc * sc * (1.0 - sc)).astype(BF16)
            dch = d_sh[0, r0:r0 + CONV_CHUNK, :]
            for k in range(CONV_WIDTH):
                prod = dch * _shifted_rows(a_sh, r0 + off + k, CONV_CHUNK)
                dw_ref[k * SUBLANES:(k + 1) * SUBLANES, :] += _rowgroup_sum(prod)

    return pl.pallas_call(
        body, name=name, grid=(nt,),
        in_specs=[pl.BlockSpec((tm, CONV_CH), lambda i: (i, 0)),
                  pl.BlockSpec((HALO, CONV_CH), lambda i: (jnp.minimum((i + 1) * hb, n_halo_blocks - 1), 0)),
                  pl.BlockSpec((tm, UUG_W), lambda i: (i, 0)),
                  pl.BlockSpec((HALO, UUG_W), lambda i: (jnp.maximum(i * hb - 1, 0), 0)),
                  pl.BlockSpec((HALO, CONV_CH), lambda i: (0, 0))],
        out_specs=[pl.BlockSpec((tm, UUG_W), lambda i: (i, 0)),
                   pl.BlockSpec((CONV_WIDTH * SUBLANES, CONV_CH), lambda i: (0, 0))],
        out_shape=[jax.ShapeDtypeStruct((t, UUG_W), BF16),
                   jax.ShapeDtypeStruct((CONV_WIDTH * SUBLANES, CONV_CH), F32)],
        scratch_shapes=[pltpu.VMEM((SUBLANES, tm + HALO, CONV_CH), F32),
                        pltpu.VMEM((SUBLANES, tm + HALO, CONV_CH), F32)],
        compiler_params=_cp(("arbitrary",)),
    )(dy0, dy0, uug, uug, w32)


def _merge_fwd(y2, w_conv_out, a_out, gg, name, tm=512):
    t, d = a_out.shape
    k = y2.shape[1]

    def body(y_ref, w_ref, a_ref, g_ref, c_ref, o_ref):
        c = jnp.dot(y_ref[...], w_ref[...], preferred_element_type=F32)
        c_ref[...] = c
        o_ref[...] = (_sigmoid(g_ref[:, 0:d]) * a_ref[...] + _sigmoid(g_ref[:, d:]) * c).astype(BF16)

    row = pl.BlockSpec((tm, d), lambda i: (i, 0))
    return pl.pallas_call(
        body, name=name, grid=(t // tm,),
        in_specs=[pl.BlockSpec((tm, k), lambda i: (i, 0)), pl.BlockSpec((k, d), lambda i: (0, 0)), row,
                  pl.BlockSpec((tm, 2 * d), lambda i: (i, 0))],
        out_specs=[row, row],
        out_shape=[jax.ShapeDtypeStruct((t, d), F32), jax.ShapeDtypeStruct((t, d), BF16)],
        compiler_params=_cp(("parallel",)),
    )(y2, w_conv_out, a_out, gg)


def _merge_bwd(dx1, w_out, a_out, c_out, gg, name, tm=512):
    t, d = a_out.shape

    def body(dx_ref, w_ref, a_ref, c_ref, g_ref, da_ref, dc_ref, dg_ref):
        dmv = lax.dot_general(dx_ref[...].astype(BF16), w_ref[...], _DIMS["nt"], preferred_element_type=F32)
        sa = _sigmoid(g_ref[:, 0:d])
        sb = _sigmoid(g_ref[:, d:])
        da_ref[...] = dmv * sa
        dc_ref[...] = (dmv * sb).astype(BF16)
        dg_ref[:, 0:d] = (dmv * a_ref[...] * sa * (1.0 - sa)).astype(BF16)
        dg_ref[:, d:] = (dmv * c_ref[...] * sb * (1.0 - sb)).astype(BF16)

    row = pl.BlockSpec((tm, d), lambda i: (i, 0))
    wide = pl.BlockSpec((tm, 2 * d), lambda i: (i, 0))
    return pl.pallas_call(
        body, name=name, grid=(t // tm,),
        in_specs=[row, pl.BlockSpec((d, d), lambda i: (0, 0)), row, row, wide], out_specs=[row, row, wide],
        out_shape=[jax.ShapeDtypeStruct((t, d), F32), jax.ShapeDtypeStruct((t, d), BF16),
                   jax.ShapeDtypeStruct((t, 2 * d), BF16)],
        compiler_params=_cp(("parallel",)),
    )(dx1, w_out, a_out, c_out, gg)


FF_TN = 1408


def _ffn_up_fwd(h2, wgu, name, tm=512, after=None):
    t, d = h2.shape
    nj = D_FF // FF_TN

    def body(*refs):
        h_ref, wg_ref, wu_ref = refs[:3]
        o_ref, g_ref, u_ref = refs[-3:]
        hv = h_ref[...]
        g = jnp.dot(hv, wg_ref[...], preferred_element_type=F32)
        u = jnp.dot(hv, wu_ref[...], preferred_element_type=F32)
        o_ref[...] = ((g * _sigmoid(g)) * u).astype(BF16)
        g_ref[...] = g.astype(BF16)
        u_ref[...] = u.astype(BF16)

    tile = pl.BlockSpec((tm, FF_TN), lambda j, i: (i, j))
    o = jax.ShapeDtypeStruct((t, D_FF), BF16)
    return pl.pallas_call(
        body, name=name, grid=(nj, t // tm),
        in_specs=[pl.BlockSpec((tm, d), lambda j, i: (i, 0)),
                  pl.BlockSpec((d, FF_TN), lambda j, i: (0, j)),
                  pl.BlockSpec((d, FF_TN), lambda j, i: (0, j + nj))] + ([] if after is None else [_ANY]),
        out_specs=[tile, tile, tile], out_shape=[o, o, o],
        compiler_params=_cp(("parallel", "parallel")),
    )(h2, wgu, wgu, *([] if after is None else [after]))


def _ffn_bwd_mid(g, u, dx2, wd, name, tm=512, after=None):
    t, d = dx2.shape
    nj = D_FF // FF_TN

    def body(*refs):
        g_ref, u_ref, dx_ref, wd_ref = refs[:4]
        dg_ref, du_ref = refs[-2:]
        gv = g_ref[...].astype(F32)
        uv = u_ref[...].astype(F32)
        dact = lax.dot_general(dx_ref[...].astype(BF16), wd_ref[...], _DIMS["nt"], preferred_element_type=F32)
        sg = _sigmoid(gv)
        dg_ref[...] = (dact * uv * (sg * (1.0 + gv * (1.0 - sg)))).astype(BF16)
        du_ref[...] = (dact * (gv * sg)).astype(BF16)

    tile = pl.BlockSpec((tm, FF_TN), lambda j, i: (i, j))
    o = jax.ShapeDtypeStruct((t, D_FF), BF16)
    return pl.pallas_call(
        body, name=name, grid=(nj, t // tm),
        in_specs=[tile, tile, pl.BlockSpec((tm, d), lambda j, i: (i, 0)),
                  pl.BlockSpec((FF_TN, d), lambda j, i: (j, 0))] + ([] if after is None else [_ANY]),
        out_specs=[tile, tile], out_shape=[o, o],
        compiler_params=_cp(("parallel", "parallel")),
    )(g, u, dx2, wd, *([] if after is None else [after]))


def _loss_head(y, target, name, tm=512):
    t, d = y.shape

    def body(y_ref, t_ref, dy_ref, loss_ref):
        @pl.when(pl.program_id(0) == 0)
        def _():
            loss_ref[...] = jnp.zeros_like(loss_ref)

        e = y_ref[...] - t_ref[...]
        dy_ref[...] = e * (1.0 / d)
        s = _rowgroup_sum(e * e)
        acc = s[:, 0:LANES]
        for c in range(1, d // LANES):
            acc = acc + s[:, c * LANES:(c + 1) * LANES]
        loss_ref[...] += acc * (0.5 / d)

    row = pl.BlockSpec((tm, d), lambda i: (i, 0))
    return pl.pallas_call(
        body, name=name, grid=(t // tm,),
        in_specs=[row, row], out_specs=[row, pl.BlockSpec((SUBLANES, LANES), lambda i: (0, 0))],
        out_shape=[jax.ShapeDtypeStruct((t, d), F32), jax.ShapeDtypeStruct((SUBLANES, LANES), F32)],
        compiler_params=_cp(("arbitrary",)),
    )(y, target)


def _exchange(arrays, scatter, name):
    n = len(arrays)

    def body(*refs):
        ins, outs = refs[:n], refs[n:2 * n]
        send_sems, recv_sems, local_sems = refs[2 * n:]
        x, y, c = lax.axis_index("x"), lax.axis_index("y"), lax.axis_index("c")
        me = 4 * x + 2 * y + c

        def peer(k):
            px, py, pc = x ^ ((k >> 2) & 1), y ^ ((k >> 1) & 1), c ^ (k & 1)
            return (px, py, pc), 4 * px + 2 * py + pc

        def src(a, dst_id):
            return ins[a].at[dst_id] if scatter else ins[a]

        locals_ = [pltpu.make_async_copy(src(a, me), outs[a].at[me], local_sems.at[a]) for a in range(n)]
        for cp in locals_:
            cp.start()
        sends = []
        for k in range(1, N_DEV):
            dev, pid = peer(k)
            for a in range(n):
                sends.append(pltpu.make_async_remote_copy(
                    src_ref=src(a, pid), dst_ref=outs[a].at[me],
                    send_sem=send_sems.at[a, k], recv_sem=recv_sems.at[a, k],
                    device_id=dev, device_id_type=pl.DeviceIdType.MESH))
        for cp in sends:
            cp.start()
        for k in range(1, N_DEV):
            dev, pid = peer(k)
            for a in range(n):
                pltpu.make_async_remote_copy(
                    src_ref=src(a, pid), dst_ref=outs[a].at[pid],
                    send_sem=send_sems.at[a, k], recv_sem=recv_sems.at[a, k],
                    device_id=dev, device_id_type=pl.DeviceIdType.MESH).wait_recv()
        for cp in sends:
            cp.wait_send()
        for cp in locals_:
            cp.wait()

    def out_shape(a):
        return jax.ShapeDtypeStruct(a.shape if scatter else (N_DEV,) + a.shape, a.dtype)

    anyspec = pl.BlockSpec(memory_space=pl.ANY)
    return pl.pallas_call(
        body, name=name,
        in_specs=[anyspec] * n, out_specs=[anyspec] * n,
        out_shape=[out_shape(a) for a in arrays],
        scratch_shapes=[pltpu.SemaphoreType.DMA((n, N_DEV)), pltpu.SemaphoreType.DMA((n, N_DEV)),
                        pltpu.SemaphoreType.DMA((n,))],
    )(*arrays)


_HBM = pl.BlockSpec(memory_space=pltpu.HBM)
_SEM = pl.BlockSpec(memory_space=pltpu.SEMAPHORE)
_ANY = pl.BlockSpec(memory_space=pl.ANY)
_EFFECT = pltpu.SideEffectType.DATAFLOW_SIDE_EFFECTING


def _mesh_peer(k):
    x, y, c = lax.axis_index("x"), lax.axis_index("y"), lax.axis_index("c")
    px, py, pc = x ^ ((k >> 2) & 1), y ^ ((k >> 1) & 1), c ^ (k & 1)
    return (px, py, pc), 4 * px + 2 * py + pc


def _exchange_start(arrays, scatter, name, after):
    n = len(arrays)
    lands = [lax.empty(a.shape if scatter else (N_DEV,) + a.shape, a.dtype) for a in arrays]

    def body(*refs):
        ins, land_refs = refs[:n], refs[n:2 * n]
        send_sems, recv_sems = refs[2 * n + 1], refs[2 * n + 2]
        token = refs[-1]
        _, me = _mesh_peer(0)
        for k in range(1, N_DEV):
            dev, pid = _mesh_peer(k)
            for a in range(n):
                pltpu.make_async_remote_copy(
                    src_ref=ins[a].at[pid] if scatter else ins[a], dst_ref=land_refs[a].at[me],
                    send_sem=send_sems.at[a * N_DEV + k], recv_sem=recv_sems.at[a * N_DEV + k],
                    device_id=dev, device_id_type=pl.DeviceIdType.MESH).start()
        token[...] = jnp.zeros_like(token)

    hbm_in = [pltpu.with_memory_space_constraint(a, pltpu.HBM) for a in list(arrays) + lands]
    outs = pl.pallas_call(
        body, name=name,
        in_specs=[_HBM] * (2 * n) + [_ANY],
        out_specs=[_SEM, _SEM] + [_HBM] * (2 * n) + [pl.BlockSpec(memory_space=pltpu.VMEM)],
        out_shape=[pltpu.SemaphoreType.DMA((n * N_DEV,)), pltpu.SemaphoreType.DMA((n * N_DEV,))]
        + [pltpu.HBM(a.shape, a.dtype) for a in hbm_in]
        + [jax.ShapeDtypeStruct((SUBLANES, LANES), F32)],
        input_output_aliases={i: 2 + i for i in range(2 * n)},
        compiler_params=pltpu.CompilerParams(has_side_effects=_EFFECT),
    )(*hbm_in, after)
    return outs[0], outs[1], outs[2:2 + n], outs[2 + n:2 + 2 * n], outs[-1]


def _exchange_wait(started, scatter, name, after):
    send_sems, recv_sems, srcs, lands, _ = started
    n = len(srcs)

    def body(*refs):
        ins, land_refs = refs[:n], refs[n:2 * n]
        send_sems, recv_sems = refs[2 * n], refs[2 * n + 1]
        copies = []
        for k in range(1, N_DEV):
            dev, pid = _mesh_peer(k)
            for a in range(n):
                copies.append(pltpu.make_async_remote_copy(
                    src_ref=ins[a].at[pid] if scatter else ins[a], dst_ref=land_refs[a].at[pid],
                    send_sem=send_sems.at[a * N_DEV + k], recv_sem=recv_sems.at[a * N_DEV + k],
                    device_id=dev, device_id_type=pl.DeviceIdType.MESH))
        for cp in copies:
            cp.wait_recv()
        for cp in copies:
            cp.wait_send()

    outs = pl.pallas_call(
        body, name=name,
        in_specs=[_HBM] * (2 * n) + [_SEM, _SEM, _ANY],
        out_specs=[_HBM] * (2 * n),
        out_shape=[pltpu.HBM(a.shape, a.dtype) for a in list(srcs) + list(lands)],
        input_output_aliases={i: i for i in range(2 * n)},
        compiler_params=pltpu.CompilerParams(has_side_effects=_EFFECT),
    )(*srcs, *lands, send_sems, recv_sems, after)
    me = 4 * lax.axis_index("x") + 2 * lax.axis_index("y") + lax.axis_index("c")
    filled = []
    for src, land in zip(outs[:n], outs[n:]):
        own = lax.dynamic_index_in_dim(src, me, 0, keepdims=True) if scatter else src[None]
        filled.append(lax.dynamic_update_slice(land, own, (me,) + (0,) * (land.ndim - 1)))
    return filled


def _adamw(parts, w, m, v, name, tr):
    nl = len(parts)
    r, c = parts[0].shape[1:]
    assert w.shape == (nl * r, c) and r % tr == 0, (name, w.shape, r, tr)
    nt = r // tr
    c1 = 1.0 - ADAM_B1 ** ADAM_STEP
    c2 = 1.0 - ADAM_B2 ** ADAM_STEP

    def body(*refs):
        p_refs = refs[:nl]
        w_ref, m_ref, v_ref, g_ref, d_ref, nm_ref, nv_ref = refs[nl:]
        layer = pl.program_id(0)
        for k in range(nl):
            @pl.when(layer == k)
            def _(p_ref=p_refs[k]):
                g = p_ref[0].astype(F32)
                for s in range(1, N_DEV):
                    g = g + p_ref[s].astype(F32)
                nm = ADAM_B1 * m_ref[...] + (1.0 - ADAM_B1) * g
                nv = ADAM_B2 * v_ref[...] + (1.0 - ADAM_B2) * (g * g)
                g_ref[...] = g
                nm_ref[...] = nm
                nv_ref[...] = nv
                d_ref[...] = -ADAM_LR * ((nm / c1) / (jnp.sqrt(nv / c2) + ADAM_EPS) + ADAM_WD * w_ref[...])

    def part_spec(k):
        return pl.BlockSpec((N_DEV, tr, c), lambda l, i: (0, jnp.where(l == k, i, 0), 0))

    row = pl.BlockSpec((tr, c), lambda l, i: (l * nt + i, 0))
    o = jax.ShapeDtypeStruct((nl * r, c), F32)
    return pl.pallas_call(
        body, name=name, grid=(nl, nt),
        in_specs=[part_spec(k) for k in range(nl)] + [row, row, row],
        out_specs=[row, row, row, row], out_shape=[o, o, o, o],
        compiler_params=_cp(("arbitrary", "arbitrary")),
    )(*parts, w, m, v)


def _with_token(gain, token):
    return gain if token is None else gain + token[0:1, 0:1]


def _layer_fwd(x, wl, sl, tabs, l, rest_fn=None, rest2_fn=None, h=None, after=None, next_gain=None):
    ctab, stab, bd = tabs
    n = f"l{l}_"
    if h is None:
        h = _rmsnorm_fwd(x, sl["norm_mix"], n + "norm_mix")
    qkv = _mm(h, wl["w_qkv"], "nn", out_dtype=F32, name=n + "proj_qkv", tm=1024, tn=QKV_W, tk=D_MODEL, after=after)
    uug = _mm(h, wl["w_uug"], "nn", out_dtype=F32, name=n + "proj_uug", tm=1024, tn=UUG_W, tk=D_MODEL, after=after)
    gg = _mm(h, wl["w_gg"], "nn", out_dtype=F32, name=n + "proj_gg", tm=1024, tn=GG_W, tk=D_MODEL, after=after)
    qs, kd, vd = _qk_prep_fwd(qkv, sl["gq"], sl["gk"], ctab, stab, bd, n + "qk_prep")
    a_out = _attn_fwd(qs, kd, vd, sl["sinks"], n + "attn")
    y0, y2 = _conv_fwd(uug, sl["conv_w32"], sl["conv_b"], sl["ln_g"], sl["ln_b"], n + "conv")
    token = None
    if rest_fn is not None:
        rest, token = rest_fn(y2)
        wl = {**wl, **rest}
    c_out, merged = _merge_fwd(y2, wl["w_conv_out"], a_out, gg, n + "merge")
    x1, h2 = _mm_rows([(merged, wl["w_out"])], "nn", name=n + "out_proj", tm=512, resid=x,
                      norm_fwd=_with_token(sl["norm_ffn"], token))
    token2 = None
    if rest2_fn is not None:
        rest2, token2 = rest2_fn(x1)
        wl = {**wl, **rest2}
    act, g_pre, u_pre = _ffn_up_fwd(h2, wl["w_gate_up"], n + "ffn_up", after=token2)
    if next_gain is None:
        x2 = _mm_rows([(act, wl["w_down"])], "nn", name=n + "ffn_down", tm=512, resid=x1)
        h_next = None
    else:
        x2, h_next = _mm_rows([(act, wl["w_down"])], "nn", name=n + "ffn_down", tm=512, resid=x1, norm_fwd=next_gain)
    saved = dict(x=x, h=h, qkv=qkv, uug=uug, gg=gg, qs=qs, kd=kd, vd=vd, a_out=a_out, y0=y0, y2=y2,
                 c_out=c_out, merged=merged, x1=x1, h2=h2, act=act, g_pre=g_pre, u_pre=u_pre)
    return x2, saved, wl, h_next


def _layer_bwd(dx2, sv, wl, sl, tabs, l, after=None, ffn_hook=None, mix_hook=None):
    ctab, stab, bd = tabs
    n = f"l{l}_b_"
    tk = 2048
    gw, gs = {}, {}
    gw["w_down"] = _mm(sv["act"], dx2, "tn", out_dtype=BF16, name=n + "dw_down", tm=FF_TN, tn=D_MODEL, tk=tk,
                       after=after)
    dg, du = _ffn_bwd_mid(sv["g_pre"], sv["u_pre"], dx2, wl["w_down"], n + "ffn_mid", after=after)
    gw["w_gate"] = _mm(sv["h2"], dg, "tn", out_dtype=BF16, name=n + "dw_gate", tm=D_MODEL, tn=FF_TN, tk=tk)
    gw["w_up"] = _mm(sv["h2"], du, "tn", out_dtype=BF16, name=n + "dw_up", tm=D_MODEL, tn=FF_TN, tk=tk)
    token = None if ffn_hook is None else ffn_hook(gw)
    dx1, gs["norm_ffn"] = _mm_rows([(dg, wl["w_gate"]), (du, wl["w_up"])], "nt", name=n + "dh2", tm=256,
                                   norm_bwd=(sv["x1"], _with_token(sl["norm_ffn"], token), dx2))
    gw["w_out"] = _mm(sv["merged"], dx1, "tn", out_dtype=BF16, name=n + "dw_out", tm=D_MODEL, tn=D_MODEL, tk=tk)
    da_out, dc_out, dgg = _merge_bwd(dx1, wl["w_out"], sv["a_out"], sv["c_out"], sv["gg"], n + "merge")
    dy2 = _mm(dc_out, wl["w_conv_out"], "nt", out_dtype=F32, name=n + "dy2", tm=1024, tn=CONV_CH, tk=D_MODEL)
    gw["w_conv_out"] = _mm(sv["y2"], dc_out, "tn", out_dtype=BF16, name=n + "dw_conv_out", tm=CONV_CH, tn=D_MODEL,
                           tk=tk)
    dy0, gs["ln_g"], gs["ln_b"], gs["conv_b"] = _conv_bwd_ln(dy2, sv["y0"], sl["ln_g"], sl["ln_b"], n + "conv_ln")
    duug, gs["conv_w"] = _conv_bwd_taps(dy0, sv["uug"], sl["conv_w32"], n + "conv_taps")
    dqs, dkd, dvd, gs["sinks"] = _attn_bwd(sv["qs"], sv["kd"], sv["vd"], sl["sinks"], da_out, n + "attn")
    dqkv, gs["gq"], gs["gk"] = _qk_prep_bwd(dqs, dkd, dvd, sv["qkv"], sl["gq"], sl["gk"], ctab, stab, bd,
                                            n + "qk_prep")
    gw["w_qkv"] = _mm(sv["h"], dqkv, "tn", out_dtype=BF16, name=n + "dw_qkv", tm=D_MODEL, tn=QKV_W, tk=tk)
    gw["w_uug"] = _mm(sv["h"], duug, "tn", out_dtype=BF16, name=n + "dw_uug", tm=D_MODEL, tn=UUG_W, tk=tk)
    gw["w_gg"] = _mm(sv["h"], dgg, "tn", out_dtype=BF16, name=n + "dw_gg", tm=D_MODEL, tn=GG_W, tk=tk)
    token_mix = None if mix_hook is None else mix_hook(gw)
    dx, gs["norm_mix"] = _mm_rows([(dqkv, wl["w_qkv"]), (duug, wl["w_uug"]), (dgg, wl["w_gg"])], "nt",
                                  name=n + "dh", tm=512,
                                  norm_bwd=(sv["x"], _with_token(sl["norm_mix"], token_mix), dx1))
    return dx, gw, gs


def _cols_to_full(g):
    n, l, r, c = g.shape
    return jnp.transpose(g, (1, 2, 0, 3)).reshape(l, r, n * c)


def _rows_to_full(g):
    n, l, r, c = g.shape
    return jnp.transpose(g, (1, 0, 2, 3)).reshape(l, n * r, c)


def _full_to_cols(w):
    l, r, c = w.shape
    return jnp.transpose(w.reshape(l, r, N_DEV, c // N_DEV), (2, 0, 1, 3))


def _full_to_rows(w):
    l, r, c = w.shape
    return jnp.transpose(w.reshape(l, N_DEV, r // N_DEV, c), (1, 0, 2, 3))


SMALL = (("norm_mix", D_MODEL), ("q_norm", HEAD_DIM), ("k_norm", HEAD_DIM), ("sinks", N_HEADS),
         ("conv_w", CONV_WIDTH * CONV_CH), ("conv_b", CONV_CH), ("conv_ln_g", CONV_CH), ("conv_ln_b", CONV_CH),
         ("norm_ffn", D_MODEL))
SMALL_TOTAL = DEPTH * sum(s for _, s in SMALL)
SMALL_ROWS = -(-SMALL_TOTAL // (LANES * SUBLANES)) * SUBLANES


def _pack_small(d):
    flat = jnp.concatenate([d[k].reshape(-1).astype(F32) for k, _ in SMALL])
    flat = jnp.pad(flat, (0, SMALL_ROWS * LANES - SMALL_TOTAL))
    return flat.reshape(SMALL_ROWS, LANES)


def _unpack_small(buf, shapes):
    flat = buf.reshape(-1)
    out, o = {}, 0
    for k, s in SMALL:
        out[k] = flat[o:o + DEPTH * s].reshape(shapes[k])
        o += DEPTH * s
    return out


def kernel(x, norm_mix, w_in, q_norm, k_norm, sinks, conv_w, conv_b, conv_ln_g, conv_ln_b, w_conv_out, w_out, norm_ffn, w_gate_up, w_down, loss_target, m_norm_mix, m_w_in, m_q_norm, m_k_norm, m_sinks, m_conv_w, m_conv_b, m_conv_ln_g, m_conv_ln_b, m_w_conv_out, m_w_out, m_norm_ffn, m_w_gate_up, m_w_down, v_norm_mix, v_w_in, v_q_norm, v_k_norm, v_sinks, v_conv_w, v_conv_b, v_conv_ln_g, v_conv_ln_b, v_w_conv_out, v_w_out, v_norm_ffn, v_w_gate_up, v_w_down):
    t = x.shape[1]
    me = 4 * lax.axis_index("x") + 2 * lax.axis_index("y") + lax.axis_index("c")
    xs = x.reshape(t, D_MODEL)
    target = loss_target.reshape(t, D_MODEL)

    def shards_in(l):
        return [w_in[l].astype(BF16)]

    def shards_rest(l):
        return [w_conv_out[l].astype(BF16), w_out[l].astype(BF16), w_gate_up[l].astype(BF16), w_down[l].astype(BF16)]

    def weights_in(g_in):
        f_in = _cols_to_full(g_in[:, None])[0]
        return dict(w_qkv=f_in[:, :QKV_W], w_uug=f_in[:, QKV_W:QKV_W + UUG_W], w_gg=f_in[:, QKV_W + UUG_W:])

    def weights_merge(g):
        g_co, g_out = g
        return dict(w_conv_out=_cols_to_full(g_co[:, None])[0], w_out=_rows_to_full(g_out[:, None])[0])

    def weights_ffn(g):
        g_gu, g_dn = g
        f_gu = _cols_to_full(g_gu[:, None])[0]
        return dict(w_gate_up=f_gu, w_gate=f_gu[:, :D_FF], w_up=f_gu[:, D_FF:], w_down=_rows_to_full(g_dn[:, None])[0])

    def weights_rest(g):
        return {**weights_merge(g[:2]), **weights_ffn(g[2:])}

    g_in0, g_cw = _exchange(shards_in(0) + [conv_w], scatter=False, name="gather_w_in_0")
    f_cw = _cols_to_full(g_cw)
    tabs = _rope_tables(t) + (_block_diag_ones(),)

    def layer_small(l, token):
        return dict(norm_mix=_with_token(norm_mix[l][None], token), norm_ffn=norm_ffn[l][None],
                    gq=jnp.tile(q_norm[l], 2)[None], gk=jnp.tile(k_norm[l], 2)[None], sinks=sinks[l],
                    conv_w32=jnp.pad(f_cw[l], ((0, HALO - CONV_WIDTH), (0, 0))),
                    conv_b=conv_b[l][None], ln_g=conv_ln_g[l][None], ln_b=conv_ln_b[l][None])

    wls, sls, saved = [], [], []
    cur = xs
    flight = {"next": None}

    def start_next(l, after):
        flight["next"] = _exchange_start(shards_in(l + 1) + shards_rest(l + 1), False, f"gather_start_{l + 1}",
                                         after=after)
        return flight["next"][4]

    merge0 = _exchange_start(shards_rest(0)[:2], False, "gather_start_merge_0", after=g_in0)
    ffn0 = _exchange_start(shards_rest(0)[2:], False, "gather_start_ffn_0", after=merge0[4])

    def rest_fn0(after):
        g = _exchange_wait(merge0, False, "gather_wait_merge_0", after=after)
        return weights_merge(g), start_next(0, g[0])

    def rest2_fn0(after):
        return weights_ffn(_exchange_wait(ffn0, False, "gather_wait_ffn_0", after=after)), None

    gathered, h_next = None, None
    for l in range(DEPTH):
        if l == 0:
            w_first, token, rest_fn, rest2_fn = weights_in(g_in0), ffn0[4], rest_fn0, rest2_fn0
        else:
            w_first = {**weights_in(gathered[0]), **weights_rest(gathered[1:])}
            token = start_next(l, gathered[0]) if l + 1 < DEPTH else None
            rest_fn, rest2_fn = None, None
        sls.append(layer_small(l, token if l == 0 else None))
        cur, sv, wl, h_next = _layer_fwd(cur, w_first, sls[l], tabs, l, rest_fn=rest_fn, rest2_fn=rest2_fn, h=h_next,
                                         after=None if l == 0 else token,
                                         next_gain=norm_mix[l + 1][None] if l + 1 < DEPTH else None)
        wls.append(wl)
        saved.append(sv)
        if l + 1 < DEPTH:
            gathered = _exchange_wait(flight["next"], False, f"gather_wait_{l + 1}", after=cur)
    dy, loss_part = _loss_head(cur, target, "loss_head")
    loss = lax.psum(jnp.sum(loss_part), ("x", "y", "c"))

    def slabs_ffn(gw):
        d_gu = jnp.concatenate([gw["w_gate"], gw["w_up"]], axis=1)[None]
        return [_full_to_cols(d_gu)[:, 0].astype(BF16), _full_to_rows(gw["w_down"][None])[:, 0].astype(BF16)]

    def slabs_mix(gw):
        d_in = jnp.concatenate([gw["w_qkv"], gw["w_uug"], gw["w_gg"]], axis=1)[None]
        return [_full_to_cols(d_in)[:, 0].astype(BF16), _full_to_cols(gw["w_conv_out"][None])[:, 0].astype(BF16),
                _full_to_rows(gw["w_out"][None])[:, 0].astype(BF16)]

    gss = [None] * DEPTH
    parts_ffn, parts_mix = [None] * DEPTH, [None] * DEPTH
    dcur = dy
    state = {"mix": None, "ffn": None}

    def make_ffn_hook(l):
        def hook(gw):
            sends = slabs_ffn(gw)
            after = sends[0]
            if state["mix"] is not None:
                parts_mix[l + 1] = _exchange_wait(state["mix"], True, f"scatter_wait_mix_{l + 1}", after=sends[0])
                after = parts_mix[l + 1][0]
            state["ffn"] = _exchange_start(sends, True, f"scatter_start_ffn_{l}", after=after)
            return state["ffn"][4]
        return hook

    def last_mix_hook(gw):
        sends = slabs_mix(gw)
        parts_ffn[0] = _exchange_wait(state["ffn"], True, "scatter_wait_ffn_0", after=sends[0])
        state["mix"] = _exchange_start(sends, True, "scatter_start_mix_0", after=parts_ffn[0][0])
        return state["mix"][4]

    for l in reversed(range(DEPTH)):
        dcur, gw, gss[l] = _layer_bwd(dcur, saved[l], wls[l], sls[l], tabs, l,
                                      after=None if state["mix"] is None else state["mix"][4],
                                      ffn_hook=make_ffn_hook(l), mix_hook=last_mix_hook if l == 0 else None)
        if l > 0:
            parts_ffn[l] = _exchange_wait(state["ffn"], True, f"scatter_wait_ffn_{l}", after=dcur)
            state["mix"] = _exchange_start(slabs_mix(gw), True, f"scatter_start_mix_{l}", after=parts_ffn[l][0])
        else:
            parts_mix[0] = _exchange_wait(state["mix"], True, "scatter_wait_mix_0", after=dcur)
    grad_x = dcur.reshape(x.shape)
    parts = [[parts_mix[l][0] for l in range(DEPTH)], [parts_mix[l][1] for l in range(DEPTH)],
             [parts_mix[l][2] for l in range(DEPTH)], [parts_ffn[l][0] for l in range(DEPTH)],
             [parts_ffn[l][1] for l in range(DEPTH)]]

    def update(p, w, m, v, name, tr):
        shp = w.shape
        r = shp[0] * shp[1]
        flat = lambda a: a.reshape(r, shp[2])
        outs = _adamw(p, flat(w), flat(m), flat(v), name, tr)
        return [o.reshape(shp) for o in outs]

    u_in = update(parts[0], w_in, m_w_in, v_w_in, "adamw_w_in", 256)
    u_co = update(parts[1], w_conv_out, m_w_conv_out, v_w_conv_out, "adamw_w_conv_out", 512)
    u_out = update(parts[2], w_out, m_w_out, v_w_out, "adamw_w_out", 128)
    u_gu = update(parts[3], w_gate_up, m_w_gate_up, v_w_gate_up, "adamw_w_gate_up", 256)
    u_dn = update(parts[4], w_down, m_w_down, v_w_down, "adamw_w_down", 176)

    def fold_rows(a):
        return jnp.sum(a, axis=0)

    def fold_heads(a):
        return jnp.sum(a, axis=0).reshape(2, HEAD_DIM).sum(axis=0)

    small_g = {
        "norm_mix": jnp.stack([fold_rows(gss[l]["norm_mix"]) for l in range(DEPTH)]),
        "q_norm": jnp.stack([fold_heads(gss[l]["gq"]) for l in range(DEPTH)]),
        "k_norm": jnp.stack([fold_heads(gss[l]["gk"]) for l in range(DEPTH)]),
        "sinks": jnp.stack([gss[l]["sinks"][0, :N_HEADS] for l in range(DEPTH)]),
        "conv_w": jnp.stack([gss[l]["conv_w"].reshape(CONV_WIDTH, SUBLANES, CONV_CH).sum(axis=1)
                             for l in range(DEPTH)]),
        "conv_b": jnp.stack([fold_rows(gss[l]["conv_b"]) for l in range(DEPTH)]),
        "conv_ln_g": jnp.stack([fold_rows(gss[l]["ln_g"]) for l in range(DEPTH)]),
        "conv_ln_b": jnp.stack([fold_rows(gss[l]["ln_b"]) for l in range(DEPTH)]),
        "norm_ffn": jnp.stack([fold_rows(gss[l]["norm_ffn"]) for l in range(DEPTH)]),
    }
    (small_parts,) = _exchange([_pack_small(small_g)], scatter=False, name="gather_small_grads")
    shapes = {"norm_mix": norm_mix.shape, "q_norm": q_norm.shape, "k_norm": k_norm.shape, "sinks": sinks.shape,
              "conv_w": (DEPTH, CONV_WIDTH, CONV_CH), "conv_b": conv_b.shape, "conv_ln_g": conv_ln_g.shape,
              "conv_ln_b": conv_ln_b.shape, "norm_ffn": norm_ffn.shape}

    def widen(a):
        z = jnp.zeros((DEPTH, CONV_WIDTH, N_DEV, CONV_CH // N_DEV), F32)
        z = lax.dynamic_update_slice(z, a[:, :, None, :], (0, 0, me, 0))
        return z.reshape(DEPTH, CONV_WIDTH, CONV_CH)

    sw = _pack_small(dict(norm_mix=norm_mix, q_norm=q_norm, k_norm=k_norm, sinks=sinks, conv_w=widen(conv_w),
                          conv_b=conv_b, conv_ln_g=conv_ln_g, conv_ln_b=conv_ln_b, norm_ffn=norm_ffn))
    sm = _pack_small(dict(norm_mix=m_norm_mix, q_norm=m_q_norm, k_norm=m_k_norm, sinks=m_sinks,
                          conv_w=widen(m_conv_w), conv_b=m_conv_b, conv_ln_g=m_conv_ln_g, conv_ln_b=m_conv_ln_b,
                          norm_ffn=m_norm_ffn))
    sv_ = _pack_small(dict(norm_mix=v_norm_mix, q_norm=v_q_norm, k_norm=v_k_norm, sinks=v_sinks,
                           conv_w=widen(v_conv_w), conv_b=v_conv_b,
                           conv_ln_g=v_conv_ln_g, conv_ln_b=v_conv_ln_b, norm_ffn=v_norm_ffn))
    s_outs = [_unpack_small(o, shapes) for o in _adamw([small_parts], sw, sm, sv_, "adamw_small", SMALL_ROWS)]

    def narrow(a):
        a4 = a.reshape(DEPTH, CONV_WIDTH, N_DEV, CONV_CH // N_DEV)
        return lax.dynamic_slice(a4, (0, 0, me, 0), (DEPTH, CONV_WIDTH, 1, CONV_CH // N_DEV)).reshape(
            DEPTH, CONV_WIDTH, CONV_CH // N_DEV)

    big = {"w_in": u_in, "w_conv_out": u_co, "w_out": u_out, "w_gate_up": u_gu, "w_down": u_dn}
    order = ["norm_mix", "w_in", "q_norm", "k_norm", "sinks", "conv_w", "conv_b", "conv_ln_g", "conv_ln_b",
             "w_conv_out", "w_out", "norm_ffn", "w_gate_up", "w_down"]
    outs = [loss, grad_x]
    for kind in range(4):
        for name in order:
            if name in big:
                outs.append(big[name][kind])
            elif name == "conv_w":
                outs.append(narrow(s_outs[kind][name]))
            else:
                outs.append(s_outs[kind][name])
    return tuple(outs)
```

```python
import functools
import math

import jax
import jax.numpy as jnp
from jax import lax
from jax.experimental import pallas as pl
from jax.experimental.pallas import tpu as pltpu

F32 = jnp.float32
BF16 = jnp.bfloat16

D_MODEL = 1024
DEPTH = 4
N_HEADS = 16
N_KV_HEADS = 2
HEAD_DIM = 64
ROT_DIM = HEAD_DIM // 4
ROPE_THETA = 500000.0
BLOCK = 128
CONV_CH = D_MODEL // 2
CONV_WIDTH = 31
D_FF = 2816
EPS = 1e-6
Q_W = N_HEADS * HEAD_DIM
KV_W = N_KV_HEADS * HEAD_DIM
QKV_W = Q_W + 2 * KV_W
UUG_W = 2 * CONV_CH
GG_W = 2 * D_MODEL
IN_W = QKV_W + UUG_W + GG_W
N_DEV = 8

ADAM_LR = 0.001
ADAM_B1 = 0.9
ADAM_B2 = 0.999
ADAM_EPS = 1e-08
ADAM_WD = 0.01
ADAM_STEP = 10

LANES = 128
SUBLANES = 8
HALO = 32
VMEM_LIMIT = 56 * 1024 * 1024
NEG = -1e30


def _cp(sem=None):
    return pltpu.CompilerParams(dimension_semantics=sem, vmem_limit_bytes=VMEM_LIMIT)


def _sigmoid(z):
    return 1.0 / (1.0 + jnp.exp(-z))


def _rowgroup_sum(z):
    r, c = z.shape
    return jnp.sum(z.reshape(r // SUBLANES, SUBLANES, c), axis=0)


_DIMS = {"nn": (((1,), (0,)), ((), ())), "nt": (((1,), (1,)), ((), ())), "tn": (((0,), (0,)), ((), ()))}


def _mm(a, b, mode, *, out_dtype, name, tm, tn, tk, resid=None, after=None):
    if mode == "nn":
        (m, k), (k2, n) = a.shape, b.shape
        a_spec = pl.BlockSpec((tm, tk), lambda i, j, s: (i, s))
        b_spec = pl.BlockSpec((tk, tn), lambda i, j, s: (s, j))
    elif mode == "nt":
        (m, k), (n, k2) = a.shape, b.shape
        a_spec = pl.BlockSpec((tm, tk), lambda i, j, s: (i, s))
        b_spec = pl.BlockSpec((tn, tk), lambda i, j, s: (j, s))
    else:
        (k, m), (k2, n) = a.shape, b.shape
        a_spec = pl.BlockSpec((tk, tm), lambda i, j, s: (s, i))
        b_spec = pl.BlockSpec((tk, tn), lambda i, j, s: (s, j))
    assert k == k2 and m % tm == 0 and n % tn == 0 and k % tk == 0, (name, a.shape, b.shape, tm, tn, tk)
    nk = k // tk
    dims = _DIMS[mode]
    has_resid = resid is not None

    def body(*refs):
        a_ref, b_ref = refs[0], refs[1]
        r_ref = refs[2] if has_resid else None
        o_ref = refs[-1] if nk == 1 else refs[-2]
        part = lax.dot_general(a_ref[...].astype(BF16), b_ref[...].astype(BF16), dims, preferred_element_type=F32)

        def finish(acc):
            if has_resid:
                acc = acc + r_ref[...]
            o_ref[...] = acc.astype(out_dtype)

        if nk == 1:
            finish(part)
            return
        acc_ref = refs[-1]
        s = pl.program_id(2)

        @pl.when(s == 0)
        def _():
            acc_ref[...] = part

        @pl.when(s > 0)
        def _():
            acc_ref[...] += part

        @pl.when(s == nk - 1)
        def _():
            finish(acc_ref[...])

    in_specs = [a_spec, b_spec]
    args = [a, b]
    if has_resid:
        in_specs.append(pl.BlockSpec((tm, tn), lambda i, j, s: (i, j)))
        args.append(resid)
    if after is not None:
        in_specs.append(_ANY)
        args.append(after)
    return pl.pallas_call(
        body, name=name, grid=(m // tm, n // tn, nk),
        in_specs=in_specs, out_specs=pl.BlockSpec((tm, tn), lambda i, j, s: (i, j)),
        out_shape=jax.ShapeDtypeStruct((m, n), out_dtype),
        scratch_shapes=[] if nk == 1 else [pltpu.VMEM((tm, tn), F32)],
        compiler_params=_cp(("parallel", "parallel", "arbitrary")),
    )(*args)


def _mm_rows(pairs, mode, *, name, tm, resid=None, norm_fwd=None, norm_bwd=None, after=None):
    pairs = [tuple(p) for p in pairs]
    m = pairs[0][0].shape[0]
    n = pairs[0][1].shape[1] if mode == "nn" else pairs[0][1].shape[0]
    assert m % tm == 0 and not (norm_fwd is not None and norm_bwd is not None), name
    dims = _DIMS[mode]
    np_ = len(pairs)
    row = pl.BlockSpec((tm, n), lambda i: (i, 0))
    vec = pl.BlockSpec((1, n), lambda i: (0, 0))
    in_specs, args = [], []
    for pair in pairs:
        a, b = pair[0], pair[1]
        k = a.shape[1]
        if len(pair) == 3:
            assert mode == "nt" and a.shape[0] == m and b.shape[0] == n and b.shape[1] % k == 0, (name, a.shape, b.shape)
            b_spec = pl.BlockSpec((n, k), functools.partial(lambda i, j: (0, j), j=pair[2]))
        else:
            assert a.shape[0] == m and (b.shape == (k, n) if mode == "nn" else b.shape == (n, k)), (name, a.shape, b.shape)
            b_spec = pl.BlockSpec(b.shape, lambda i: (0, 0))
        in_specs += [pl.BlockSpec((tm, k), lambda i: (i, 0)), b_spec]
        args += [a, b]
    if resid is not None:
        in_specs.append(row)
        args.append(resid)
    if norm_fwd is not None:
        in_specs.append(vec)
        args.append(norm_fwd)
    if norm_bwd is not None:
        in_specs += [row, vec, row]
        args += list(norm_bwd)
    if after is not None:
        in_specs.append(_ANY)
        args.append(after)
    n_out = 1 if (norm_fwd is None and norm_bwd is None) else 2

    def body(*refs):
        outs = refs[len(refs) - n_out:]
        pos = 2 * np_
        acc = None
        for p in range(np_):
            part = lax.dot_general(refs[2 * p][...].astype(BF16), refs[2 * p + 1][...].astype(BF16), dims,
                                   preferred_element_type=F32)
            acc = part if acc is None else acc + part
        if resid is not None:
            acc = acc + refs[pos][...]
            pos += 1
        if norm_fwd is not None:
            r = lax.rsqrt(jnp.mean(acc * acc, axis=-1, keepdims=True) + EPS)
            outs[1][...] = ((acc * r) * refs[pos][...]).astype(BF16)
        if norm_bwd is not None:
            @pl.when(pl.program_id(0) == 0)
            def _():
                outs[1][...] = jnp.zeros_like(outs[1])

            xv = refs[pos][...]
            r = lax.rsqrt(jnp.mean(xv * xv, axis=-1, keepdims=True) + EPS)
            y = xv * r
            outs[1][...] += _rowgroup_sum(acc * y)
            dy = acc * refs[pos + 1][...]
            acc = refs[pos + 2][...] + r * (dy - y * jnp.mean(dy * y, axis=-1, keepdims=True))
        outs[0][...] = acc

    out_specs = [row]
    out_shape = [jax.ShapeDtypeStruct((m, n), F32)]
    if norm_fwd is not None:
        out_specs.append(row)
        out_shape.append(jax.ShapeDtypeStruct((m, n), BF16))
    if norm_bwd is not None:
        out_specs.append(pl.BlockSpec((SUBLANES, n), lambda i: (0, 0)))
        out_shape.append(jax.ShapeDtypeStruct((SUBLANES, n), F32))
    res = pl.pallas_call(
        body, name=name, grid=(m // tm,), in_specs=in_specs, out_specs=out_specs, out_shape=out_shape,
        compiler_params=_cp(("arbitrary",) if norm_bwd is not None else ("parallel",)),
    )(*args)
    return res[0] if n_out == 1 else res


def _rmsnorm_fwd(x, g, name, tm=512):
    t, d = x.shape

    def body(x_ref, g_ref, h_ref):
        xv = x_ref[...]
        r = lax.rsqrt(jnp.mean(xv * xv, axis=-1, keepdims=True) + EPS)
        h_ref[...] = ((xv * r) * g_ref[...]).astype(BF16)

    return pl.pallas_call(
        body, name=name, grid=(t // tm,),
        in_specs=[pl.BlockSpec((tm, d), lambda i: (i, 0)), pl.BlockSpec((1, d), lambda i: (0, 0))],
        out_specs=pl.BlockSpec((tm, d), lambda i: (i, 0)),
        out_shape=jax.ShapeDtypeStruct((t, d), BF16),
        compiler_params=_cp(("parallel",)),
    )(x, g)


def _rmsnorm_bwd(dh, x, g, resid, name, tm=512):
    t, d = x.shape

    def body(dh_ref, x_ref, g_ref, r_ref, dx_ref, dg_ref):
        @pl.when(pl.program_id(0) == 0)
        def _():
            dg_ref[...] = jnp.zeros_like(dg_ref)

        xv = x_ref[...]
        dhv = dh_ref[...]
        r = lax.rsqrt(jnp.mean(xv * xv, axis=-1, keepdims=True) + EPS)
        y = xv * r
        dg_ref[...] += _rowgroup_sum(dhv * y)
        dy = dhv * g_ref[...]
        dx_ref[...] = r_ref[...] + r * (dy - y * jnp.mean(dy * y, axis=-1, keepdims=True))

    row = pl.BlockSpec((tm, d), lambda i: (i, 0))
    return pl.pallas_call(
        body, name=name, grid=(t // tm,),
        in_specs=[row, row, pl.BlockSpec((1, d), lambda i: (0, 0)), row],
        out_specs=[row, pl.BlockSpec((SUBLANES, d), lambda i: (0, 0))],
        out_shape=[jax.ShapeDtypeStruct((t, d), F32), jax.ShapeDtypeStruct((SUBLANES, d), F32)],
        compiler_params=_cp(("arbitrary",)),
    )(dh, x, g, resid)


def _seg_sum(z, bd):
    hi = z.astype(BF16)
    lo = (z - hi.astype(F32)).astype(BF16)
    return jnp.dot(hi, bd, preferred_element_type=F32) + jnp.dot(lo, bd, preferred_element_type=F32)


def _partner(z, lane64):
    return jnp.where(lane64 < ROT_DIM // 2, pltpu.roll(z, LANES - ROT_DIM // 2, 1), pltpu.roll(z, ROT_DIM // 2, 1))


def _rope_tables(t):
    inv_freq = ROPE_THETA ** (-jnp.arange(0, ROT_DIM, 2, dtype=F32) / ROT_DIM)
    ang = jnp.arange(t, dtype=F32)[:, None] * inv_freq[None, :]
    cos, sin = jnp.cos(ang), jnp.sin(ang)
    c64 = jnp.concatenate([cos, cos, jnp.ones((t, HEAD_DIM - ROT_DIM), F32)], axis=1)
    s64 = jnp.concatenate([-sin, sin, jnp.zeros((t, HEAD_DIM - ROT_DIM), F32)], axis=1)
    return jnp.tile(c64, (1, 2)), jnp.tile(s64, (1, 2))


def _block_diag_ones():
    r = lax.broadcasted_iota(jnp.int32, (LANES, LANES), 0) // HEAD_DIM
    c = lax.broadcasted_iota(jnp.int32, (LANES, LANES), 1) // HEAD_DIM
    return (r == c).astype(BF16)


def _qk_prep_fwd(qkv, gq, gk, ctab, stab, bd, name, tm=512):
    t = qkv.shape[0]
    scale = HEAD_DIM ** -0.5
    n_qg = Q_W // LANES

    def body(q_ref, kv_ref, gq_ref, gk_ref, c_ref, s_ref, bd_ref, qs_ref, kd_ref, vd_ref):
        lane = lax.broadcasted_iota(jnp.int32, (tm, LANES), 1)
        lane64 = lane % HEAD_DIM
        lo_half = lane < HEAD_DIM
        cv, sv, bdv = c_ref[...], s_ref[...], bd_ref[...]

        def norm_rope(xg, g):
            r = lax.rsqrt(_seg_sum(xg * xg, bdv) * (1.0 / HEAD_DIM) + EPS)
            yn = (xg * r) * g
            return yn * cv + _partner(yn, lane64) * sv

        zero = jnp.zeros((BLOCK, LANES), BF16)
        lo_blk = lo_half[0:BLOCK]
        for c in range(n_qg):
            xg = q_ref[:, c * LANES:(c + 1) * LANES]
            qn = (norm_rope(xg, gq_ref[...]) * scale).astype(BF16)
            for b in range(tm // BLOCK):
                rows = qn[b * BLOCK:(b + 1) * BLOCK]
                qs_ref[b, 2 * c] = jnp.where(lo_blk, rows, zero)
                qs_ref[b, 2 * c + 1] = jnp.where(lo_blk, zero, rows)
        kk = norm_rope(kv_ref[:, 0:LANES], gk_ref[...])
        kr = pltpu.roll(kk, HEAD_DIM, 1)
        kd_ref[:, 0:LANES] = jnp.where(lo_half, kk, kr).astype(BF16)
        kd_ref[:, LANES:2 * LANES] = jnp.where(lo_half, kr, kk).astype(BF16)
        vv = kv_ref[:, LANES:2 * LANES]
        vr = pltpu.roll(vv, HEAD_DIM, 1)
        vd_ref[:, 0:LANES] = jnp.where(lo_half, vv, vr).astype(BF16)
        vd_ref[:, LANES:2 * LANES] = jnp.where(lo_half, vr, vv).astype(BF16)

    vec = pl.BlockSpec((1, LANES), lambda i: (0, 0))
    tab = pl.BlockSpec((tm, LANES), lambda i: (i, 0))
    return pl.pallas_call(
        body, name=name, grid=(t // tm,),
        in_specs=[pl.BlockSpec((tm, Q_W), lambda i: (i, 0)),
                  pl.BlockSpec((tm, 2 * KV_W), lambda i: (i, Q_W // (2 * KV_W))),
                  vec, vec, tab, tab, pl.BlockSpec((LANES, LANES), lambda i: (0, 0))],
        out_specs=[pl.BlockSpec((tm // BLOCK, N_HEADS, BLOCK, LANES), lambda i: (i, 0, 0, 0)),
                   pl.BlockSpec((tm, 2 * LANES), lambda i: (i, 0)),
                   pl.BlockSpec((tm, 2 * LANES), lambda i: (i, 0))],
        out_shape=[jax.ShapeDtypeStruct((t // BLOCK, N_HEADS, BLOCK, LANES), BF16),
                   jax.ShapeDtypeStruct((t, 2 * LANES), BF16), jax.ShapeDtypeStruct((t, 2 * LANES), BF16)],
        compiler_params=_cp(("parallel",)),
    )(qkv, qkv, gq, gk, ctab, stab, bd)


def _qk_prep_bwd(dqs, dkd, dvd, qkv, gq, gk, ctab, stab, bd, name, tm=512):
    t = qkv.shape[0]
    scale = HEAD_DIM ** -0.5
    n_qg = Q_W // LANES

    def body(dqs_ref, dkd_ref, dvd_ref, q_ref, kv_ref, gq_ref, gk_ref, c_ref, s_ref, bd_ref,
             dqkv_ref, dgq_ref, dgk_ref):
        @pl.when(pl.program_id(0) == 0)
        def _():
            dgq_ref[...] = jnp.zeros_like(dgq_ref)
            dgk_ref[...] = jnp.zeros_like(dgk_ref)

        lane = lax.broadcasted_iota(jnp.int32, (tm, LANES), 1)
        lane64 = lane % HEAD_DIM
        lo_half = lane < HEAD_DIM
        cv, sv, bdv = c_ref[...], s_ref[...], bd_ref[...]

        def bwd(xg, g, dout):
            r = lax.rsqrt(_seg_sum(xg * xg, bdv) * (1.0 / HEAD_DIM) + EPS)
            y = xg * r
            dyn = dout * cv + jnp.where(lane64 < ROT_DIM, _partner(dout * sv, lane64), 0.0)
            dy = dyn * g
            dx = r * (dy - y * (_seg_sum(dy * y, bdv) * (1.0 / HEAD_DIM)))
            return dx, _rowgroup_sum(dyn * y)

        dgq = jnp.zeros((SUBLANES, LANES), F32)
        for c in range(n_qg):
            sl = slice(c * LANES, (c + 1) * LANES)
            dx, dg = bwd(q_ref[:, sl], gq_ref[...], dqs_ref[:, sl] * scale)
            dqkv_ref[:, sl] = dx.astype(BF16)
            dgq = dgq + dg
        dgq_ref[...] += dgq
        dk = jnp.where(lo_half, dkd_ref[:, 0:LANES], dkd_ref[:, LANES:2 * LANES])
        dx, dg = bwd(kv_ref[:, 0:LANES], gk_ref[...], dk)
        dqkv_ref[:, Q_W:Q_W + LANES] = dx.astype(BF16)
        dgk_ref[...] += dg
        dv = jnp.where(lo_half, dvd_ref[:, 0:LANES], dvd_ref[:, LANES:2 * LANES])
        dqkv_ref[:, Q_W + LANES:Q_W + 2 * LANES] = dv.astype(BF16)

    vec = pl.BlockSpec((1, LANES), lambda i: (0, 0))
    tab = pl.BlockSpec((tm, LANES), lambda i: (i, 0))
    wide = pl.BlockSpec((tm, 2 * LANES), lambda i: (i, 0))
    acc = pl.BlockSpec((SUBLANES, LANES), lambda i: (0, 0))
    return pl.pallas_call(
        body, name=name, grid=(t // tm,),
        in_specs=[pl.BlockSpec((tm, Q_W), lambda i: (i, 0)), wide, wide,
                  pl.BlockSpec((tm, Q_W), lambda i: (i, 0)),
                  pl.BlockSpec((tm, 2 * KV_W), lambda i: (i, Q_W // (2 * KV_W))),
                  vec, vec, tab, tab, pl.BlockSpec((LANES, LANES), lambda i: (0, 0))],
        out_specs=[pl.BlockSpec((tm, QKV_W), lambda i: (i, 0)), acc, acc],
        out_shape=[jax.ShapeDtypeStruct((t, QKV_W), BF16), jax.ShapeDtypeStruct((SUBLANES, LANES), F32),
                   jax.ShapeDtypeStruct((SUBLANES, LANES), F32)],
        compiler_params=_cp(("arbitrary",)),
    )(dqs, dkd, dvd, qkv, qkv, gq, gk, ctab, stab, bd)


GROUP = N_HEADS // N_KV_HEADS
GROUP_ROWS = GROUP * BLOCK


def _attn_masks():
    row = lax.broadcasted_iota(jnp.int32, (BLOCK, BLOCK), 0)
    col = lax.broadcasted_iota(jnp.int32, (BLOCK, BLOCK), 1)
    return col <= row, col < HEAD_DIM


def _window_softmax(s_c, s_p, is_cur, has_prev, sink):
    s = jnp.where(is_cur, s_c, jnp.where(has_prev, s_p, NEG))
    m = jnp.maximum(jnp.max(s, axis=-1, keepdims=True), sink)
    e = jnp.exp(s - m)
    e_s = jnp.exp(sink - m)
    inv = 1.0 / (jnp.sum(e, axis=-1, keepdims=True) + e_s)
    return e * inv, e_s * inv


def _attn_fwd(qs, kd, vd, sinks, name):
    nb = qs.shape[0]
    t = nb * BLOCK

    def body(sink_ref, q_ref, kc_ref, kp_ref, vc_ref, vp_ref, o_ref, pc_scr, pp_scr):
        has_prev = pl.program_id(0) > 0
        is_cur, lo_half = _attn_masks()
        for j in range(N_KV_HEADS):
            ks = slice(j * LANES, (j + 1) * LANES)
            qg = q_ref[0, j * GROUP:(j + 1) * GROUP].reshape(GROUP_ROWS, LANES)
            s_c = lax.dot_general(qg, kc_ref[:, ks], _DIMS["nt"], preferred_element_type=F32)
            s_p = lax.dot_general(qg, kp_ref[:, ks], _DIMS["nt"], preferred_element_type=F32)
            for g in range(GROUP):
                rs = slice(g * BLOCK, (g + 1) * BLOCK)
                p, _ = _window_softmax(s_c[rs], s_p[rs], is_cur, has_prev, sink_ref[j * GROUP + g])
                rg = slice(j * GROUP_ROWS + g * BLOCK, j * GROUP_ROWS + (g + 1) * BLOCK)
                pc_scr[rg, :] = jnp.where(is_cur, p, 0.0).astype(BF16)
                pp_scr[rg, :] = jnp.where(is_cur, 0.0, p).astype(BF16)
            grp = slice(j * GROUP_ROWS, (j + 1) * GROUP_ROWS)
            o2 = (jnp.dot(pc_scr[grp, :], vc_ref[:, ks], preferred_element_type=F32)
                  + jnp.dot(pp_scr[grp, :], vp_ref[:, ks], preferred_element_type=F32))
            for pp in range(GROUP // 2):
                c0 = (j * (GROUP // 2) + pp) * LANES
                o_ref[:, c0:c0 + LANES] = jnp.where(lo_half, o2[2 * pp * BLOCK:(2 * pp + 1) * BLOCK],
                                                    o2[(2 * pp + 1) * BLOCK:(2 * pp + 2) * BLOCK])

    cur = lambda i: (i, 0)
    prev = lambda i: (jnp.maximum(i - 1, 0), 0)
    kvs = (BLOCK, 2 * LANES)
    return pl.pallas_call(
        body, name=name, grid=(nb,),
        in_specs=[pl.BlockSpec(memory_space=pltpu.SMEM),
                  pl.BlockSpec((1, N_HEADS, BLOCK, LANES), lambda i: (i, 0, 0, 0)),
                  pl.BlockSpec(kvs, cur), pl.BlockSpec(kvs, prev), pl.BlockSpec(kvs, cur), pl.BlockSpec(kvs, prev)],
        out_specs=pl.BlockSpec((BLOCK, Q_W), cur),
        out_shape=jax.ShapeDtypeStruct((t, Q_W), F32),
        scratch_shapes=[pltpu.VMEM((N_KV_HEADS * GROUP_ROWS, LANES), BF16)] * 2,
        compiler_params=_cp(("parallel",)),
    )(sinks, qs, kd, kd, vd, vd)


def _attn_bwd(qs, kd, vd, sinks, do, name):
    nb = qs.shape[0]
    t = nb * BLOCK

    def body(sink_ref, q_ref, do_ref, kc_ref, kp_ref, vc_ref, vp_ref,
             dq_ref, dk_ref, dv_ref, dsink_ref,
             carry_k, carry_v, dsink_acc, do_scr, pc_scr, pp_scr, dsc_scr, dsp_scr):
        i = pl.program_id(0)

        @pl.when(i == 0)
        def _():
            carry_k[...] = jnp.zeros_like(carry_k)
            carry_v[...] = jnp.zeros_like(carry_v)
            dsink_acc[...] = jnp.zeros_like(dsink_acc)

        @pl.when(i < nb)
        def _():
            has_prev = i > 0
            is_cur, lo_half = _attn_masks()
            srow = lax.broadcasted_iota(jnp.int32, (SUBLANES, LANES), 0)
            scol = lax.broadcasted_iota(jnp.int32, (SUBLANES, LANES), 1)
            dsink = jnp.zeros((SUBLANES, LANES), F32)
            for j in range(N_KV_HEADS):
                ks = slice(j * LANES, (j + 1) * LANES)
                kc, kp, vc, vp = kc_ref[:, ks], kp_ref[:, ks], vc_ref[:, ks], vp_ref[:, ks]
                qg = q_ref[0, j * GROUP:(j + 1) * GROUP].reshape(GROUP_ROWS, LANES)
                for pp in range(GROUP // 2):
                    c0 = (j * (GROUP // 2) + pp) * LANES
                    dop = do_ref[:, c0:c0 + LANES]
                    r0 = j * GROUP_ROWS + 2 * pp * BLOCK
                    do_scr[r0:r0 + BLOCK, :] = jnp.where(lo_half, dop, 0.0).astype(BF16)
                    do_scr[r0 + BLOCK:r0 + 2 * BLOCK, :] = jnp.where(lo_half, 0.0, dop).astype(BF16)
                grp = slice(j * GROUP_ROWS, (j + 1) * GROUP_ROWS)
                dog = do_scr[grp, :]
                s_c = lax.dot_general(qg, kc, _DIMS["nt"], preferred_element_type=F32)
                s_p = lax.dot_general(qg, kp, _DIMS["nt"], preferred_element_type=F32)
                dp_c = lax.dot_general(dog, vc, _DIMS["nt"], preferred_element_type=F32)
                dp_p = lax.dot_general(dog, vp, _DIMS["nt"], preferred_element_type=F32)
                for g in range(GROUP):
                    rs = slice(g * BLOCK, (g + 1) * BLOCK)
                    h = j * GROUP + g
                    p, p_s = _window_softmax(s_c[rs], s_p[rs], is_cur, has_prev, sink_ref[h])
                    dp = jnp.where(is_cur, dp_c[rs], dp_p[rs])
                    delta = jnp.sum(p * dp, axis=-1, keepdims=True)
                    ds = p * (dp - delta)
                    dsv = -jnp.sum(p_s * delta, axis=0, keepdims=True)
                    dsink = dsink + jnp.where(jnp.logical_and(srow == 0, scol == h), dsv, 0.0)
                    rg = slice(j * GROUP_ROWS + g * BLOCK, j * GROUP_ROWS + (g + 1) * BLOCK)
                    pc_scr[rg, :] = jnp.where(is_cur, p, 0.0).astype(BF16)
                    pp_scr[rg, :] = jnp.where(is_cur, 0.0, p).astype(BF16)
                    dsc_scr[rg, :] = jnp.where(is_cur, ds, 0.0).astype(BF16)
                    dsp_scr[rg, :] = jnp.where(is_cur, 0.0, ds).astype(BF16)
                dsc, dsp = dsc_scr[grp, :], dsp_scr[grp, :]
                dq2 = jnp.dot(dsc, kc, preferred_element_type=F32) + jnp.dot(dsp, kp, preferred_element_type=F32)
                for pp in range(GROUP // 2):
                    c0 = (j * (GROUP // 2) + pp) * LANES
                    dq_ref[:, c0:c0 + LANES] = jnp.where(lo_half, dq2[2 * pp * BLOCK:(2 * pp + 1) * BLOCK],
                                                         dq2[(2 * pp + 1) * BLOCK:(2 * pp + 2) * BLOCK])
                dk_c = lax.dot_general(dsc, qg, _DIMS["tn"], preferred_element_type=F32)
                dk_p = lax.dot_general(dsp, qg, _DIMS["tn"], preferred_element_type=F32)
                dv_c = lax.dot_general(pc_scr[grp, :], dog, _DIMS["tn"], preferred_element_type=F32)
                dv_p = lax.dot_general(pp_scr[grp, :], dog, _DIMS["tn"], preferred_element_type=F32)
                dk_ref[:, ks] = carry_k[:, ks] + dk_p + pltpu.roll(dk_p, HEAD_DIM, 1)
                dv_ref[:, ks] = carry_v[:, ks] + dv_p + pltpu.roll(dv_p, HEAD_DIM, 1)
                carry_k[:, ks] = dk_c + pltpu.roll(dk_c, HEAD_DIM, 1)
                carry_v[:, ks] = dv_c + pltpu.roll(dv_c, HEAD_DIM, 1)
            dsink_acc[...] += dsink

        @pl.when(i == nb)
        def _():
            dk_ref[...] = carry_k[...]
            dv_ref[...] = carry_v[...]
            dsink_ref[...] = dsink_acc[...]

    last = nb - 1
    cur = lambda i: (jnp.minimum(i, last), 0)
    prev = lambda i: (jnp.clip(i - 1, 0, last), 0)
    kvs = (BLOCK, 2 * LANES)
    stk = pltpu.VMEM((N_KV_HEADS * GROUP_ROWS, LANES), BF16)
    return pl.pallas_call(
        body, name=name, grid=(nb + 1,),
        in_specs=[pl.BlockSpec(memory_space=pltpu.SMEM),
                  pl.BlockSpec((1, N_HEADS, BLOCK, LANES), lambda i: (jnp.minimum(i, last), 0, 0, 0)),
                  pl.BlockSpec((BLOCK, Q_W), cur),
                  pl.BlockSpec(kvs, cur), pl.BlockSpec(kvs, prev), pl.BlockSpec(kvs, cur), pl.BlockSpec(kvs, prev)],
        out_specs=[pl.BlockSpec((BLOCK, Q_W), cur), pl.BlockSpec(kvs, prev), pl.BlockSpec(kvs, prev),
                   pl.BlockSpec((SUBLANES, LANES), lambda i: (0, 0))],
        out_shape=[jax.ShapeDtypeStruct((t, Q_W), F32), jax.ShapeDtypeStruct((t, 2 * LANES), F32),
                   jax.ShapeDtypeStruct((t, 2 * LANES), F32), jax.ShapeDtypeStruct((SUBLANES, LANES), F32)],
        scratch_shapes=[pltpu.VMEM(kvs, F32), pltpu.VMEM(kvs, F32), pltpu.VMEM((SUBLANES, LANES), F32),
                        stk, stk, stk, stk, stk],
        compiler_params=_cp(("arbitrary",)),
    )(sinks, qs, do, kd, kd, vd, vd)


CONV_CHUNK = 64


def _fill_row_shifts(sh):
    rows = sh.shape[1] - SUBLANES
    for r in range(1, SUBLANES):
        sh[r, 0:rows, :] = sh[0, r:r + rows, :]


def _shifted_rows(sh, start, size):
    q, r = divmod(start, SUBLANES)
    return sh[r, q * SUBLANES:q * SUBLANES + size, :]


def _conv_fwd(uug, w32, cb, lg, lb, name, tm=512):
    t = uug.shape[0]
    hb = tm // HALO

    def body(m_ref, h_ref, w_ref, cb_ref, lg_ref, lb_ref, y0_ref, y2_ref, a_sh):
        i = pl.program_id(0)
        a_sh[0, HALO:, :] = m_ref[:, 0:CONV_CH] * _sigmoid(m_ref[:, CONV_CH:])
        ah = h_ref[:, 0:CONV_CH] * _sigmoid(h_ref[:, CONV_CH:])
        a_sh[0, 0:HALO, :] = jnp.where(i > 0, ah, 0.0)
        _fill_row_shifts(a_sh)
        off = HALO - (CONV_WIDTH - 1)
        for c in range(tm // CONV_CHUNK):
            r0 = c * CONV_CHUNK
            acc = jnp.zeros((CONV_CHUNK, CONV_CH), F32)
            for k in range(CONV_WIDTH):
                acc = acc + w_ref[k:k + 1, :] * _shifted_rows(a_sh, r0 + off + k, CONV_CHUNK)
            y0 = acc + cb_ref[...]
            y0_ref[r0:r0 + CONV_CHUNK, :] = y0
            mu = jnp.mean(y0, axis=-1, keepdims=True)
            dlt = y0 - mu
            rstd = lax.rsqrt(jnp.mean(dlt * dlt, axis=-1, keepdims=True) + EPS)
            y1 = (dlt * rstd) * lg_ref[...] + lb_ref[...]
            y2_ref[r0:r0 + CONV_CHUNK, :] = (y1 * _sigmoid(y1)).astype(BF16)

    vec = pl.BlockSpec((1, CONV_CH), lambda i: (0, 0))
    return pl.pallas_call(
        body, name=name, grid=(t // tm,),
        in_specs=[pl.BlockSpec((tm, UUG_W), lambda i: (i, 0)),
                  pl.BlockSpec((HALO, UUG_W), lambda i: (jnp.maximum(i * hb - 1, 0), 0)),
                  pl.BlockSpec((HALO, CONV_CH), lambda i: (0, 0)), vec, vec, vec],
        out_specs=[pl.BlockSpec((tm, CONV_CH), lambda i: (i, 0)), pl.BlockSpec((tm, CONV_CH), lambda i: (i, 0))],
        out_shape=[jax.ShapeDtypeStruct((t, CONV_CH), F32), jax.ShapeDtypeStruct((t, CONV_CH), BF16)],
        scratch_shapes=[pltpu.VMEM((SUBLANES, tm + HALO, CONV_CH), F32)],
        compiler_params=_cp(("parallel",)),
    )(uug, uug, w32, cb, lg, lb)


def _conv_bwd_ln(dc_out, w_conv_out, y0, lg, lb, name, tm=512):
    t = y0.shape[0]
    d = dc_out.shape[1]

    def body(dc_ref, w_ref, y0_ref, lg_ref, lb_ref, dy0_ref, dlg_ref, dlb_ref, dcb_ref):
        @pl.when(pl.program_id(0) == 0)
        def _():
            dlg_ref[...] = jnp.zeros_like(dlg_ref)
            dlb_ref[...] = jnp.zeros_like(dlb_ref)
            dcb_ref[...] = jnp.zeros_like(dcb_ref)

        y0 = y0_ref[...]
        mu = jnp.mean(y0, axis=-1, keepdims=True)
        dlt = y0 - mu
        rstd = lax.rsqrt(jnp.mean(dlt * dlt, axis=-1, keepdims=True) + EPS)
        yh = dlt * rstd
        y1 = yh * lg_ref[...] + lb_ref[...]
        sg = _sigmoid(y1)
        dy2 = lax.dot_general(dc_ref[...], w_ref[...], _DIMS["nt"], preferred_element_type=F32)
        dy1 = dy2 * (sg * (1.0 + y1 * (1.0 - sg)))
        dlg_ref[...] += _rowgroup_sum(dy1 * yh)
        dlb_ref[...] += _rowgroup_sum(dy1)
        dyh = dy1 * lg_ref[...]
        dy0 = rstd * (dyh - jnp.mean(dyh, axis=-1, keepdims=True)
                      - yh * jnp.mean(dyh * yh, axis=-1, keepdims=True))
        dcb_ref[...] += _rowgroup_sum(dy0)
        dy0_ref[...] = dy0

    row = pl.BlockSpec((tm, CONV_CH), lambda i: (i, 0))
    vec = pl.BlockSpec((1, CONV_CH), lambda i: (0, 0))
    acc = pl.BlockSpec((SUBLANES, CONV_CH), lambda i: (0, 0))
    accs = jax.ShapeDtypeStruct((SUBLANES, CONV_CH), F32)
    return pl.pallas_call(
        body, name=name, grid=(t // tm,),
        in_specs=[pl.BlockSpec((tm, d), lambda i: (i, 0)), pl.BlockSpec((CONV_CH, d), lambda i: (0, 0)), row, vec, vec],
        out_specs=[row, acc, acc, acc],
        out_shape=[jax.ShapeDtypeStruct((t, CONV_CH), F32), accs, accs, accs],
        compiler_params=_cp(("arbitrary",)),
    )(dc_out, w_conv_out, y0, lg, lb)


def _conv_bwd_taps(dy0, uug, w32, name, tm=512):
    t = uug.shape[0]
    hb = tm // HALO
    n_halo_blocks = t // HALO
    nt = t // tm

    def body(dm_ref, dn_ref, m_ref, h_ref, w_ref, duug_ref, dw_ref, a_sh, d_sh):
        i = pl.program_id(0)

        @pl.when(i == 0)
        def _():
            dw_ref[...] = jnp.zeros_like(dw_ref)

        u = m_ref[:, 0:CONV_CH]
        sg = _sigmoid(m_ref[:, CONV_CH:])
        a_sh[0, HALO:, :] = u * sg
        ah = h_ref[:, 0:CONV_CH] * _sigmoid(h_ref[:, CONV_CH:])
        a_sh[0, 0:HALO, :] = jnp.where(i > 0, ah, 0.0)
        d_sh[0, 0:tm, :] = dm_ref[...]
        d_sh[0, tm:, :] = jnp.where(i < nt - 1, dn_ref[...], 0.0)
        _fill_row_shifts(a_sh)
        _fill_row_shifts(d_sh)
        off = HALO - (CONV_WIDTH - 1)
        for c in range(tm // CONV_CHUNK):
            r0 = c * CONV_CHUNK
            da = jnp.zeros((CONV_CHUNK, CONV_CH), F32)
            for k in range(CONV_WIDTH):
                sh = CONV_WIDTH - 1 - k
                da = da + w_ref[k:k + 1, :] * _shifted_rows(d_sh, r0 + sh, CONV_CHUNK)
            uc = u[r0:r0 + CONV_CHUNK, :]
            sc = sg[r0:r0 + CONV_CHUNK, :]
            duug_ref[r0:r0 + CONV_CHUNK, 0:CONV_CH] = (da * sc).astype(BF16)
            duug_ref[r0:r0 + CONV_CHUNK, CONV_CH:] = (da * uc * sc * (1.0 - sc)).astype(BF16)
            dch = d_sh[0, r0:r0 + CONV_CHUNK, :]
            for k in range(CONV_WIDTH):
                prod = dch * _shifted_rows(a_sh, r0 + off + k, CONV_CHUNK)
                dw_ref[k * SUBLANES:(k + 1) * SUBLANES, :] += _rowgroup_sum(prod)

    return pl.pallas_call(
        body, name=name, grid=(nt,),
        in_specs=[pl.BlockSpec((tm, CONV_CH), lambda i: (i, 0)),
                  pl.BlockSpec((HALO, CONV_CH), lambda i: (jnp.minimum((i + 1) * hb, n_halo_blocks - 1), 0)),
                  pl.BlockSpec((tm, UUG_W), lambda i: (i, 0)),
                  pl.BlockSpec((HALO, UUG_W), lambda i: (jnp.maximum(i * hb - 1, 0), 0)),
                  pl.BlockSpec((HALO, CONV_CH), lambda i: (0, 0))],
        out_specs=[pl.BlockSpec((tm, UUG_W), lambda i: (i, 0)),
                   pl.BlockSpec((CONV_WIDTH * SUBLANES, CONV_CH), lambda i: (0, 0))],
        out_shape=[jax.ShapeDtypeStruct((t, UUG_W), BF16),
                   jax.ShapeDtypeStruct((CONV_WIDTH * SUBLANES, CONV_CH), F32)],
        scratch_shapes=[pltpu.VMEM((SUBLANES, tm + HALO, CONV_CH), F32),
                        pltpu.VMEM((SUBLANES, tm + HALO, CONV_CH), F32)],
        compiler_params=_cp(("arbitrary",)),
    )(dy0, dy0, uug, uug, w32)


def _merge_fwd(y2, w_conv_out, a_out, gg, name, tm=512):
    t, d = a_out.shape
    k = y2.shape[1]

    def body(y_ref, w_ref, a_ref, g_ref, c_ref, o_ref):
        c = jnp.dot(y_ref[...], w_ref[...], preferred_element_type=F32)
        c_ref[...] = c
        o_ref[...] = (_sigmoid(g_ref[:, 0:d]) * a_ref[...] + _sigmoid(g_ref[:, d:]) * c).astype(BF16)

    row = pl.BlockSpec((tm, d), lambda i: (i, 0))
    return pl.pallas_call(
        body, name=name, grid=(t // tm,),
        in_specs=[pl.BlockSpec((tm, k), lambda i: (i, 0)), pl.BlockSpec((k, d), lambda i: (0, 0)), row,
                  pl.BlockSpec((tm, 2 * d), lambda i: (i, 0))],
        out_specs=[row, row],
        out_shape=[jax.ShapeDtypeStruct((t, d), F32), jax.ShapeDtypeStruct((t, d), BF16)],
        compiler_params=_cp(("parallel",)),
    )(y2, w_conv_out, a_out, gg)


def _merge_bwd(dx1, w_out, a_out, c_out, gg, name, tm=512):
    t, d = a_out.shape

    def body(dx_ref, w_ref, a_ref, c_ref, g_ref, da_ref, dc_ref, dg_ref):
        dmv = lax.dot_general(dx_ref[...].astype(BF16), w_ref[...], _DIMS["nt"], preferred_element_type=F32)
        sa = _sigmoid(g_ref[:, 0:d])
        sb = _sigmoid(g_ref[:, d:])
        da_ref[...] = dmv * sa
        dc_ref[...] = (dmv * sb).astype(BF16)
        dg_ref[:, 0:d] = (dmv * a_ref[...] * sa * (1.0 - sa)).astype(BF16)
        dg_ref[:, d:] = (dmv * c_ref[...] * sb * (1.0 - sb)).astype(BF16)

    row = pl.BlockSpec((tm, d), lambda i: (i, 0))
    wide = pl.BlockSpec((tm, 2 * d), lambda i: (i, 0))
    return pl.pallas_call(
        body, name=name, grid=(t // tm,),
        in_specs=[row, pl.BlockSpec((d, d), lambda i: (0, 0)), row, row, wide], out_specs=[row, row, wide],
        out_shape=[jax.ShapeDtypeStruct((t, d), F32), jax.ShapeDtypeStruct((t, d), BF16),
                   jax.ShapeDtypeStruct((t, 2 * d), BF16)],
        compiler_params=_cp(("parallel",)),
    )(dx1, w_out, a_out, c_out, gg)


FF_TN = 1408


def _ffn_up_fwd(h2, wgu, name, tm=512, after=None):
    t, d = h2.shape
    nj = D_FF // FF_TN

    def body(*refs):
        h_ref, wg_ref, wu_ref = refs[:3]
        o_ref, g_ref, u_ref = refs[-3:]
        hv = h_ref[...]
        g = jnp.dot(hv, wg_ref[...], preferred_element_type=F32)
        u = jnp.dot(hv, wu_ref[...], preferred_element_type=F32)
        o_ref[...] = ((g * _sigmoid(g)) * u).astype(BF16)
        g_ref[...] = g.astype(BF16)
        u_ref[...] = u.astype(BF16)

    tile = pl.BlockSpec((tm, FF_TN), lambda j, i: (i, j))
    o = jax.ShapeDtypeStruct((t, D_FF), BF16)
    return pl.pallas_call(
        body, name=name, grid=(nj, t // tm),
        in_specs=[pl.BlockSpec((tm, d), lambda j, i: (i, 0)),
                  pl.BlockSpec((d, FF_TN), lambda j, i: (0, j)),
                  pl.BlockSpec((d, FF_TN), lambda j, i: (0, j + nj))] + ([] if after is None else [_ANY]),
        out_specs=[tile, tile, tile], out_shape=[o, o, o],
        compiler_params=_cp(("parallel", "parallel")),
    )(h2, wgu, wgu, *([] if after is None else [after]))


def _ffn_bwd_mid(g, u, dx2, wd, name, tm=512, after=None):
    t, d = dx2.shape
    nj = D_FF // FF_TN

    def body(*refs):
        g_ref, u_ref, dx_ref, wd_ref = refs[:4]
        dg_ref, du_ref = refs[-2:]
        gv = g_ref[...].astype(F32)
        uv = u_ref[...].astype(F32)
        dact = lax.dot_general(dx_ref[...].astype(BF16), wd_ref[...], _DIMS["nt"], preferred_element_type=F32)
        sg = _sigmoid(gv)
        dg_ref[...] = (dact * uv * (sg * (1.0 + gv * (1.0 - sg)))).astype(BF16)
        du_ref[...] = (dact * (gv * sg)).astype(BF16)

    tile = pl.BlockSpec((tm, FF_TN), lambda j, i: (i, j))
    o = jax.ShapeDtypeStruct((t, D_FF), BF16)
    return pl.pallas_call(
        body, name=name, grid=(nj, t // tm),
        in_specs=[tile, tile, pl.BlockSpec((tm, d), lambda j, i: (i, 0)),
                  pl.BlockSpec((FF_TN, d), lambda j, i: (j, 0))] + ([] if after is None else [_ANY]),
        out_specs=[tile, tile], out_shape=[o, o],
        compiler_params=_cp(("parallel", "parallel")),
    )(g, u, dx2, wd, *([] if after is None else [after]))


def _loss_head(y, target, name, tm=512):
    t, d = y.shape

    def body(y_ref, t_ref, dy_ref, loss_ref):
        @pl.when(pl.program_id(0) == 0)
        def _():
            loss_ref[...] = jnp.zeros_like(loss_ref)

        e = y_ref[...] - t_ref[...]
        dy_ref[...] = e * (1.0 / d)
        s = _rowgroup_sum(e * e)
        acc = s[:, 0:LANES]
        for c in range(1, d // LANES):
            acc = acc + s[:, c * LANES:(c + 1) * LANES]
        loss_ref[...] += acc * (0.5 / d)

    row = pl.BlockSpec((tm, d), lambda i: (i, 0))
    return pl.pallas_call(
        body, name=name, grid=(t // tm,),
        in_specs=[row, row], out_specs=[row, pl.BlockSpec((SUBLANES, LANES), lambda i: (0, 0))],
        out_shape=[jax.ShapeDtypeStruct((t, d), F32), jax.ShapeDtypeStruct((SUBLANES, LANES), F32)],
        compiler_params=_cp(("arbitrary",)),
    )(y, target)


def _exchange(arrays, scatter, name):
    n = len(arrays)

    def body(*refs):
        ins, outs = refs[:n], refs[n:2 * n]
        send_sems, recv_sems, local_sems = refs[2 * n:]
        x, y, c = lax.axis_index("x"), lax.axis_index("y"), lax.axis_index("c")
        me = 4 * x + 2 * y + c

        def peer(k):
            px, py, pc = x ^ ((k >> 2) & 1), y ^ ((k >> 1) & 1), c ^ (k & 1)
            return (px, py, pc), 4 * px + 2 * py + pc

        def src(a, dst_id):
            return ins[a].at[dst_id] if scatter else ins[a]

        locals_ = [pltpu.make_async_copy(src(a, me), outs[a].at[me], local_sems.at[a]) for a in range(n)]
        for cp in locals_:
            cp.start()
        sends = []
        for k in range(1, N_DEV):
            dev, pid = peer(k)
            for a in range(n):
                sends.append(pltpu.make_async_remote_copy(
                    src_ref=src(a, pid), dst_ref=outs[a].at[me],
                    send_sem=send_sems.at[a, k], recv_sem=recv_sems.at[a, k],
                    device_id=dev, device_id_type=pl.DeviceIdType.MESH))
        for cp in sends:
            cp.start()
        for k in range(1, N_DEV):
            dev, pid = peer(k)
            for a in range(n):
                pltpu.make_async_remote_copy(
                    src_ref=src(a, pid), dst_ref=outs[a].at[pid],
                    send_sem=send_sems.at[a, k], recv_sem=recv_sems.at[a, k],
                    device_id=dev, device_id_type=pl.DeviceIdType.MESH).wait_recv()
        for cp in sends:
            cp.wait_send()
        for cp in locals_:
            cp.wait()

    def out_shape(a):
        return jax.ShapeDtypeStruct(a.shape if scatter else (N_DEV,) + a.shape, a.dtype)

    anyspec = pl.BlockSpec(memory_space=pl.ANY)
    return pl.pallas_call(
        body, name=name,
        in_specs=[anyspec] * n, out_specs=[anyspec] * n,
        out_shape=[out_shape(a) for a in arrays],
        scratch_shapes=[pltpu.SemaphoreType.DMA((n, N_DEV)), pltpu.SemaphoreType.DMA((n, N_DEV)),
                        pltpu.SemaphoreType.DMA((n,))],
    )(*arrays)


def _gather_two_level(arrays, name):
    n = len(arrays)
    nk = N_DEV - 1

    def body(*refs):
        ins, outs = refs[:n], refs[n:2 * n]
        send_sems, recv_sems, local_sems = refs[2 * n:]
        x, y, c = lax.axis_index("x"), lax.axis_index("y"), lax.axis_index("c")
        me, sibling = (x, y, c), (x, y, 1 - c)
        chips = [(1 - x, y), (x, 1 - y), (1 - x, 1 - y)]

        def slot(a, dev):
            return outs[a].at[4 * dev[0] + 2 * dev[1] + dev[2]]

        def copy(a, k, block, to, src=None):
            return pltpu.make_async_remote_copy(
                src_ref=slot(a, block) if src is None else src, dst_ref=slot(a, block),
                send_sem=send_sems.at[a * nk + k], recv_sem=recv_sems.at[a * nk + k],
                device_id=to, device_id_type=pl.DeviceIdType.MESH)

        mine = [pltpu.make_async_copy(ins[a], slot(a, me), local_sems.at[a]) for a in range(n)]
        for cp in mine:
            cp.start()
        first = []
        for a in range(n):
            first.append(copy(a, 0, me, sibling, src=ins[a]))
            first += [copy(a, 1 + j, me, (*chip, c), src=ins[a]) for j, chip in enumerate(chips)]
        for cp in first:
            cp.start()
        passed = []
        for j, chip in enumerate(chips):
            for a in range(n):
                copy(a, 1 + j, (*chip, c), me).wait_recv()
                fwd = copy(a, 4 + j, (*chip, c), sibling)
                fwd.start()
                passed.append(fwd)
        for a in range(n):
            copy(a, 0, sibling, me).wait_recv()
            for j, chip in enumerate(chips):
                copy(a, 4 + j, (*chip, 1 - c), me).wait_recv()
        for cp in first + passed:
            cp.wait_send()
        for cp in mine:
            cp.wait()

    anyspec = pl.BlockSpec(memory_space=pl.ANY)
    return pl.pallas_call(
        body, name=name,
        in_specs=[anyspec] * n, out_specs=[anyspec] * n,
        out_shape=[jax.ShapeDtypeStruct((N_DEV,) + a.shape, a.dtype) for a in arrays],
        scratch_shapes=[pltpu.SemaphoreType.DMA((n * nk,)), pltpu.SemaphoreType.DMA((n * nk,)),
                        pltpu.SemaphoreType.DMA((n,))],
    )(*arrays)


_HBM = pl.BlockSpec(memory_space=pltpu.HBM)
_SEM = pl.BlockSpec(memory_space=pltpu.SEMAPHORE)
_ANY = pl.BlockSpec(memory_space=pl.ANY)
_EFFECT = pltpu.SideEffectType.DATAFLOW_SIDE_EFFECTING


def _mesh_peer(k):
    x, y, c = lax.axis_index("x"), lax.axis_index("y"), lax.axis_index("c")
    px, py, pc = x ^ ((k >> 2) & 1), y ^ ((k >> 1) & 1), c ^ (k & 1)
    return (px, py, pc), 4 * px + 2 * py + pc


def _exchange_start(arrays, scatter, name, after):
    n = len(arrays)
    lands = [lax.empty(a.shape if scatter else (N_DEV,) + a.shape, a.dtype) for a in arrays]

    def body(*refs):
        ins, land_refs = refs[:n], refs[n:2 * n]
        send_sems, recv_sems = refs[2 * n + 1], refs[2 * n + 2]
        token = refs[-1]
        _, me = _mesh_peer(0)
        for k in range(1, N_DEV):
            dev, pid = _mesh_peer(k)
            for a in range(n):
                pltpu.make_async_remote_copy(
                    src_ref=ins[a].at[pid] if scatter else ins[a], dst_ref=land_refs[a].at[me],
                    send_sem=send_sems.at[a * N_DEV + k], recv_sem=recv_sems.at[a * N_DEV + k],
                    device_id=dev, device_id_type=pl.DeviceIdType.MESH).start()
        token[...] = jnp.zeros_like(token)

    hbm_in = [pltpu.with_memory_space_constraint(a, pltpu.HBM) for a in list(arrays) + lands]
    outs = pl.pallas_call(
        body, name=name,
        in_specs=[_HBM] * (2 * n) + [_ANY],
        out_specs=[_SEM, _SEM] + [_HBM] * (2 * n) + [pl.BlockSpec(memory_space=pltpu.VMEM)],
        out_shape=[pltpu.SemaphoreType.DMA((n * N_DEV,)), pltpu.SemaphoreType.DMA((n * N_DEV,))]
        + [pltpu.HBM(a.shape, a.dtype) for a in hbm_in]
        + [jax.ShapeDtypeStruct((SUBLANES, LANES), F32)],
        input_output_aliases={i: 2 + i for i in range(2 * n)},
        compiler_params=pltpu.CompilerParams(has_side_effects=_EFFECT),
    )(*hbm_in, after)
    return outs[0], outs[1], outs[2:2 + n], outs[2 + n:2 + 2 * n], outs[-1]


def _exchange_wait(started, scatter, name, after):
    send_sems, recv_sems, srcs, lands, _ = started
    n = len(srcs)

    def body(*refs):
        ins, land_refs = refs[:n], refs[n:2 * n]
        send_sems, recv_sems = refs[2 * n], refs[2 * n + 1]
        copies = []
        for k in range(1, N_DEV):
            dev, pid = _mesh_peer(k)
            for a in range(n):
                copies.append(pltpu.make_async_remote_copy(
                    src_ref=ins[a].at[pid] if scatter else ins[a], dst_ref=land_refs[a].at[pid],
                    send_sem=send_sems.at[a * N_DEV + k], recv_sem=recv_sems.at[a * N_DEV + k],
                    device_id=dev, device_id_type=pl.DeviceIdType.MESH))
        for cp in copies:
            cp.wait_recv()
        for cp in copies:
            cp.wait_send()

    outs = pl.pallas_call(
        body, name=name,
        in_specs=[_HBM] * (2 * n) + [_SEM, _SEM, _ANY],
        out_specs=[_HBM] * (2 * n),
        out_shape=[pltpu.HBM(a.shape, a.dtype) for a in list(srcs) + list(lands)],
        input_output_aliases={i: i for i in range(2 * n)},
        compiler_params=pltpu.CompilerParams(has_side_effects=_EFFECT),
    )(*srcs, *lands, send_sems, recv_sems, after)
    me = 4 * lax.axis_index("x") + 2 * lax.axis_index("y") + lax.axis_index("c")
    filled = []
    for src, land in zip(outs[:n], outs[n:]):
        own = lax.dynamic_index_in_dim(src, me, 0, keepdims=True) if scatter else src[None]
        filled.append(lax.dynamic_update_slice(land, own, (me,) + (0,) * (land.ndim - 1)))
    return filled


def _adamw(parts, w, m, v, name, tr):
    nl = len(parts)
    r, c = parts[0].shape[1:]
    assert w.shape == (nl * r, c) and r % tr == 0, (name, w.shape, r, tr)
    nt = r // tr
    c1 = 1.0 - ADAM_B1 ** ADAM_STEP
    c2 = 1.0 - ADAM_B2 ** ADAM_STEP

    def body(*refs):
        p_refs = refs[:nl]
        w_ref, m_ref, v_ref, g_ref, d_ref, nm_ref, nv_ref = refs[nl:]
        layer = pl.program_id(0)
        for k in range(nl):
            @pl.when(layer == k)
            def _(p_ref=p_refs[k]):
                g = p_ref[0].astype(F32)
                for s in range(1, N_DEV):
                    g = g + p_ref[s].astype(F32)
                nm = ADAM_B1 * m_ref[...] + (1.0 - ADAM_B1) * g
                nv = ADAM_B2 * v_ref[...] + (1.0 - ADAM_B2) * (g * g)
                g_ref[...] = g
                nm_ref[...] = nm
                nv_ref[...] = nv
                d_ref[...] = -ADAM_LR * ((nm / c1) / (jnp.sqrt(nv / c2) + ADAM_EPS) + ADAM_WD * w_ref[...])

    def part_spec(k):
        return pl.BlockSpec((N_DEV, tr, c), lambda l, i: (0, jnp.where(l == k, i, 0), 0))

    row = pl.BlockSpec((tr, c), lambda l, i: (l * nt + i, 0))
    o = jax.ShapeDtypeStruct((nl * r, c), F32)
    return pl.pallas_call(
        body, name=name, grid=(nl, nt),
        in_specs=[part_spec(k) for k in range(nl)] + [row, row, row],
        out_specs=[row, row, row, row], out_shape=[o, o, o, o],
        compiler_params=_cp(("arbitrary", "arbitrary")),
    )(*parts, w, m, v)


def _with_token(gain, token):
    return gain if token is None else gain + token[0:1, 0:1]


def _layer_fwd(x, wl, sl, tabs, l, rest_fn=None, rest2_fn=None, h=None, after=None, next_gain=None):
    ctab, stab, bd = tabs
    n = f"l{l}_"
    if h is None:
        h = _rmsnorm_fwd(x, sl["norm_mix"], n + "norm_mix")
    qkv = _mm(h, wl["w_qkv"], "nn", out_dtype=F32, name=n + "proj_qkv", tm=1024, tn=QKV_W, tk=D_MODEL, after=after)
    uug = _mm(h, wl["w_uug"], "nn", out_dtype=F32, name=n + "proj_uug", tm=1024, tn=UUG_W, tk=D_MODEL, after=after)
    gg = _mm(h, wl["w_gg"], "nn", out_dtype=F32, name=n + "proj_gg", tm=1024, tn=GG_W, tk=D_MODEL, after=after)
    qs, kd, vd = _qk_prep_fwd(qkv, sl["gq"], sl["gk"], ctab, stab, bd, n + "qk_prep")
    a_out = _attn_fwd(qs, kd, vd, sl["sinks"], n + "attn")
    y0, y2 = _conv_fwd(uug, sl["conv_w32"], sl["conv_b"], sl["ln_g"], sl["ln_b"], n + "conv")
    token = None
    if rest_fn is not None:
        rest, token = rest_fn(y2)
        wl = {**wl, **rest}
    c_out, merged = _merge_fwd(y2, wl["w_conv_out"], a_out, gg, n + "merge")
    x1, h2 = _mm_rows([(merged, wl["w_out"])], "nn", name=n + "out_proj", tm=512, resid=x,
                      norm_fwd=_with_token(sl["norm_ffn"], token))
    token2 = None
    if rest2_fn is not None:
        rest2, token2 = rest2_fn(x1)
        wl = {**wl, **rest2}
    act, g_pre, u_pre = _ffn_up_fwd(h2, wl["w_gate_up"], n + "ffn_up", after=token2)
    if next_gain is None:
        x2 = _mm_rows([(act, wl["w_down"])], "nn", name=n + "ffn_down", tm=512, resid=x1)
        h_next = None
    else:
        x2, h_next = _mm_rows([(act, wl["w_down"])], "nn", name=n + "ffn_down", tm=512, resid=x1, norm_fwd=next_gain)
    saved = dict(x=x, h=h, qkv=qkv, uug=uug, gg=gg, qs=qs, kd=kd, vd=vd, a_out=a_out, y0=y0, y2=y2,
                 c_out=c_out, merged=merged, x1=x1, h2=h2, act=act, g_pre=g_pre, u_pre=u_pre)
    return x2, saved, wl, h_next


def _layer_bwd(dx2, sv, wl, sl, tabs, l, after=None, ffn_hook=None, mix_hook=None):
    ctab, stab, bd = tabs
    n = f"l{l}_b_"
    tk = 2048
    gw, gs = {}, {}
    gw["w_down"] = _mm(sv["act"], dx2, "tn", out_dtype=BF16, name=n + "dw_down", tm=FF_TN, tn=D_MODEL, tk=tk,
                       after=after)
    dg, du = _ffn_bwd_mid(sv["g_pre"], sv["u_pre"], dx2, wl["w_down"], n + "ffn_mid", after=after)
    gw["w_gate"] = _mm(sv["h2"], dg, "tn", out_dtype=BF16, name=n + "dw_gate", tm=D_MODEL, tn=FF_TN, tk=tk)
    gw["w_up"] = _mm(sv["h2"], du, "tn", out_dtype=BF16, name=n + "dw_up", tm=D_MODEL, tn=FF_TN, tk=tk)
    token = None if ffn_hook is None else ffn_hook(gw)
    dx1, gs["norm_ffn"] = _mm_rows([(dg, wl["w_gate_up"], 0), (du, wl["w_gate_up"], 1)], "nt", name=n + "dh2", tm=512,
                                   norm_bwd=(sv["x1"], _with_token(sl["norm_ffn"], token), dx2))
    gw["w_out"] = _mm(sv["merged"], dx1, "tn", out_dtype=BF16, name=n + "dw_out", tm=D_MODEL, tn=D_MODEL, tk=tk)
    da_out, dc_out, dgg = _merge_bwd(dx1, wl["w_out"], sv["a_out"], sv["c_out"], sv["gg"], n + "merge")
    gw["w_conv_out"] = _mm(sv["y2"], dc_out, "tn", out_dtype=BF16, name=n + "dw_conv_out", tm=CONV_CH, tn=D_MODEL,
                           tk=tk)
    dy0, gs["ln_g"], gs["ln_b"], gs["conv_b"] = _conv_bwd_ln(dc_out, wl["w_conv_out"], sv["y0"], sl["ln_g"],
                                                                 sl["ln_b"], n + "conv_ln")
    duug, gs["conv_w"] = _conv_bwd_taps(dy0, sv["uug"], sl["conv_w32"], n + "conv_taps")
    dqs, dkd, dvd, gs["sinks"] = _attn_bwd(sv["qs"], sv["kd"], sv["vd"], sl["sinks"], da_out, n + "attn")
    dqkv, gs["gq"], gs["gk"] = _qk_prep_bwd(dqs, dkd, dvd, sv["qkv"], sl["gq"], sl["gk"], ctab, stab, bd,
                                            n + "qk_prep")
    gw["w_qkv"] = _mm(sv["h"], dqkv, "tn", out_dtype=BF16, name=n + "dw_qkv", tm=D_MODEL, tn=QKV_W, tk=tk)
    gw["w_uug"] = _mm(sv["h"], duug, "tn", out_dtype=BF16, name=n + "dw_uug", tm=D_MODEL, tn=UUG_W, tk=tk)
    gw["w_gg"] = _mm(sv["h"], dgg, "tn", out_dtype=BF16, name=n + "dw_gg", tm=D_MODEL, tn=GG_W, tk=tk)
    token_mix = None if mix_hook is None else mix_hook(gw)
    dx, gs["norm_mix"] = _mm_rows([(dqkv, wl["w_qkv"]), (duug, wl["w_uug"]), (dgg, wl["w_gg"])], "nt",
                                  name=n + "dh", tm=512,
                                  norm_bwd=(sv["x"], _with_token(sl["norm_mix"], token_mix), dx1))
    return dx, gw, gs


def _cols_to_full(g):
    n, l, r, c = g.shape
    return jnp.transpose(g, (1, 2, 0, 3)).reshape(l, r, n * c)


def _rows_to_full(g):
    n, l, r, c = g.shape
    return jnp.transpose(g, (1, 0, 2, 3)).reshape(l, n * r, c)


def _full_to_cols(w):
    l, r, c = w.shape
    return jnp.transpose(w.reshape(l, r, N_DEV, c // N_DEV), (2, 0, 1, 3))


def _full_to_rows(w):
    l, r, c = w.shape
    return jnp.transpose(w.reshape(l, N_DEV, r // N_DEV, c), (1, 0, 2, 3))


SMALL = (("norm_mix", D_MODEL), ("q_norm", HEAD_DIM), ("k_norm", HEAD_DIM), ("sinks", N_HEADS),
         ("conv_w", CONV_WIDTH * CONV_CH), ("conv_b", CONV_CH), ("conv_ln_g", CONV_CH), ("conv_ln_b", CONV_CH),
         ("norm_ffn", D_MODEL))
SMALL_TOTAL = DEPTH * sum(s for _, s in SMALL)
SMALL_ROWS = -(-SMALL_TOTAL // (LANES * SUBLANES)) * SUBLANES


def _pack_small(d):
    flat = jnp.concatenate([d[k].reshape(-1).astype(F32) for k, _ in SMALL])
    flat = jnp.pad(flat, (0, SMALL_ROWS * LANES - SMALL_TOTAL))
    return flat.reshape(SMALL_ROWS, LANES)


def _unpack_small(buf, shapes):
    flat = buf.reshape(-1)
    out, o = {}, 0
    for k, s in SMALL:
        out[k] = flat[o:o + DEPTH * s].reshape(shapes[k])
        o += DEPTH * s
    return out


def kernel(x, norm_mix, w_in, q_norm, k_norm, sinks, conv_w, conv_b, conv_ln_g, conv_ln_b, w_conv_out, w_out, norm_ffn, w_gate_up, w_down, loss_target, m_norm_mix, m_w_in, m_q_norm, m_k_norm, m_sinks, m_conv_w, m_conv_b, m_conv_ln_g, m_conv_ln_b, m_w_conv_out, m_w_out, m_norm_ffn, m_w_gate_up, m_w_down, v_norm_mix, v_w_in, v_q_norm, v_k_norm, v_sinks, v_conv_w, v_conv_b, v_conv_ln_g, v_conv_ln_b, v_w_conv_out, v_w_out, v_norm_ffn, v_w_gate_up, v_w_down):
    t = x.shape[1]
    me = 4 * lax.axis_index("x") + 2 * lax.axis_index("y") + lax.axis_index("c")
    xs = x.reshape(t, D_MODEL)
    target = loss_target.reshape(t, D_MODEL)

    def shards_in(l):
        return [w_in[l].astype(BF16)]

    def shards_rest(l):
        return [w_conv_out[l].astype(BF16), w_out[l].astype(BF16), w_gate_up[l].astype(BF16), w_down[l].astype(BF16)]

    def weights_in(g_in):
        f_in = _cols_to_full(g_in[:, None])[0]
        return dict(w_qkv=f_in[:, :QKV_W], w_uug=f_in[:, QKV_W:QKV_W + UUG_W], w_gg=f_in[:, QKV_W + UUG_W:])

    def weights_merge(g):
        g_co, g_out = g
        return dict(w_conv_out=_cols_to_full(g_co[:, None])[0], w_out=_rows_to_full(g_out[:, None])[0])

    def weights_ffn(g):
        g_gu, g_dn = g
        f_gu = _cols_to_full(g_gu[:, None])[0]
        return dict(w_gate_up=f_gu, w_down=_rows_to_full(g_dn[:, None])[0])

    def weights_rest(g):
        return {**weights_merge(g[:2]), **weights_ffn(g[2:])}

    g_in0, g_cw = _gather_two_level(shards_in(0) + [conv_w], name="gather_w_in_0")
    f_cw = _cols_to_full(g_cw)
    tabs = _rope_tables(t) + (_block_diag_ones(),)

    def layer_small(l, token):
        return dict(norm_mix=_with_token(norm_mix[l][None], token), norm_ffn=norm_ffn[l][None],
                    gq=jnp.tile(q_norm[l], 2)[None], gk=jnp.tile(k_norm[l], 2)[None], sinks=sinks[l],
                    conv_w32=jnp.pad(f_cw[l], ((0, HALO - CONV_WIDTH), (0, 0))),
                    conv_b=conv_b[l][None], ln_g=conv_ln_g[l][None], ln_b=conv_ln_b[l][None])

    wls, sls, saved = [], [], []
    cur = xs
    flight = {"next": None}

    def start_next(l, after):
        flight["next"] = _exchange_start(shards_in(l + 1) + shards_rest(l + 1), False, f"gather_start_{l + 1}",
                                         after=after)
        return flight["next"][4]

    merge0 = _exchange_start(shards_rest(0)[:2], False, "gather_start_merge_0", after=g_in0)
    ffn0 = _exchange_start(shards_rest(0)[2:], False, "gather_start_ffn_0", after=merge0[4])

    def rest_fn0(after):
        g = _exchange_wait(merge0, False, "gather_wait_merge_0", after=after)
        return weights_merge(g), start_next(0, g[0])

    def rest2_fn0(after):
        return weights_ffn(_exchange_wait(ffn0, False, "gather_wait_ffn_0", after=after)), None

    gathered, h_next = None, None
    for l in range(DEPTH):
        if l == 0:
            w_first, token, rest_fn, rest2_fn = weights_in(g_in0), ffn0[4], rest_fn0, rest2_fn0
        else:
            w_first = {**weights_in(gathered[0]), **weights_rest(gathered[1:])}
            token = start_next(l, gathered[0]) if l + 1 < DEPTH else None
            rest_fn, rest2_fn = None, None
        sls.append(layer_small(l, token if l == 0 else None))
        cur, sv, wl, h_next = _layer_fwd(cur, w_first, sls[l], tabs, l, rest_fn=rest_fn, rest2_fn=rest2_fn, h=h_next,
                                         after=None if l == 0 else token,
                                         next_gain=norm_mix[l + 1][None] if l + 1 < DEPTH else None)
        wls.append(wl)
        saved.append(sv)
        if l + 1 < DEPTH:
            gathered = _exchange_wait(flight["next"], False, f"gather_wait_{l + 1}", after=cur)
    dy, loss_part = _loss_head(cur, target, "loss_head")
    loss = lax.psum(jnp.sum(loss_part), ("x", "y", "c"))

    def slabs_ffn(gw):
        d_gu = jnp.concatenate([gw["w_gate"], gw["w_up"]], axis=1)[None]
        return [_full_to_cols(d_gu)[:, 0].astype(BF16), _full_to_rows(gw["w_down"][None])[:, 0].astype(BF16)]

    def slabs_mix(gw):
        d_in = jnp.concatenate([gw["w_qkv"], gw["w_uug"], gw["w_gg"]], axis=1)[None]
        return [_full_to_cols(d_in)[:, 0].astype(BF16), _full_to_cols(gw["w_conv_out"][None])[:, 0].astype(BF16),
                _full_to_rows(gw["w_out"][None])[:, 0].astype(BF16)]

    gss = [None] * DEPTH
    parts_ffn, parts_mix = [None] * DEPTH, [None] * DEPTH
    dcur = dy
    state = {"mix": None, "ffn": None}

    def make_ffn_hook(l):
        def hook(gw):
            sends = slabs_ffn(gw)
            after = sends[0]
            if state["mix"] is not None:
                parts_mix[l + 1] = _exchange_wait(state["mix"], True, f"scatter_wait_mix_{l + 1}", after=sends[0])
                after = parts_mix[l + 1][0]
            state["ffn"] = _exchange_start(sends, True, f"scatter_start_ffn_{l}", after=after)
            return state["ffn"][4]
        return hook

    def last_mix_hook(gw):
        sends = slabs_mix(gw)
        parts_ffn[0] = _exchange_wait(state["ffn"], True, "scatter_wait_ffn_0", after=sends[0])
        state["mix"] = _exchange_start(sends, True, "scatter_start_mix_0", after=parts_ffn[0][0])
        return state["mix"][4]

    for l in reversed(range(DEPTH)):
        dcur, gw, gss[l] = _layer_bwd(dcur, saved[l], wls[l], sls[l], tabs, l,
                                      after=None if state["mix"] is None else state["mix"][4],
                                      ffn_hook=make_ffn_hook(l), mix_hook=last_mix_hook if l == 0 else None)
        if l > 0:
            parts_ffn[l] = _exchange_wait(state["ffn"], True, f"scatter_wait_ffn_{l}", after=dcur)
            state["mix"] = _exchange_start(slabs_mix(gw), True, f"scatter_start_mix_{l}", after=parts_ffn[l][0])
        else:
            parts_mix[0] = _exchange_wait(state["mix"], True, "scatter_wait_mix_0", after=dcur)
    grad_x = dcur.reshape(x.shape)
    parts = [[parts_mix[l][0] for l in range(DEPTH)], [parts_mix[l][1] for l in range(DEPTH)],
             [parts_mix[l][2] for l in range(DEPTH)], [parts_ffn[l][0] for l in range(DEPTH)],
             [parts_ffn[l][1] for l in range(DEPTH)]]

    def update(p, w, m, v, name, tr):
        shp = w.shape
        r = shp[0] * shp[1]
        flat = lambda a: a.reshape(r, shp[2])
        outs = _adamw(p, flat(w), flat(m), flat(v), name, tr)
        return [o.reshape(shp) for o in outs]

    u_in = update(parts[0], w_in, m_w_in, v_w_in, "adamw_w_in", 256)
    u_co = update(parts[1], w_conv_out, m_w_conv_out, v_w_conv_out, "adamw_w_conv_out", 512)
    u_out = update(parts[2], w_out, m_w_out, v_w_out, "adamw_w_out", 128)
    u_gu = update(parts[3], w_gate_up, m_w_gate_up, v_w_gate_up, "adamw_w_gate_up", 256)
    u_dn = update(parts[4], w_down, m_w_down, v_w_down, "adamw_w_down", 176)

    def fold_rows(a):
        return jnp.sum(a, axis=0)

    def fold_heads(a):
        return jnp.sum(a, axis=0).reshape(2, HEAD_DIM).sum(axis=0)

    small_g = {
        "norm_mix": jnp.stack([fold_rows(gss[l]["norm_mix"]) for l in range(DEPTH)]),
        "q_norm": jnp.stack([fold_heads(gss[l]["gq"]) for l in range(DEPTH)]),
        "k_norm": jnp.stack([fold_heads(gss[l]["gk"]) for l in range(DEPTH)]),
        "sinks": jnp.stack([gss[l]["sinks"][0, :N_HEADS] for l in range(DEPTH)]),
        "conv_w": jnp.stack([gss[l]["conv_w"].reshape(CONV_WIDTH, SUBLANES, CONV_CH).sum(axis=1)
                             for l in range(DEPTH)]),
        "conv_b": jnp.stack([fold_rows(gss[l]["conv_b"]) for l in range(DEPTH)]),
        "conv_ln_g": jnp.stack([fold_rows(gss[l]["ln_g"]) for l in range(DEPTH)]),
        "conv_ln_b": jnp.stack([fold_rows(gss[l]["ln_b"]) for l in range(DEPTH)]),
        "norm_ffn": jnp.stack([fold_rows(gss[l]["norm_ffn"]) for l in range(DEPTH)]),
    }
    (small_parts,) = _exchange([_pack_small(small_g)], scatter=False, name="gather_small_grads")
    shapes = {"norm_mix": norm_mix.shape, "q_norm": q_norm.shape, "k_norm": k_norm.shape, "sinks": sinks.shape,
              "conv_w": (DEPTH, CONV_WIDTH, CONV_CH), "conv_b": conv_b.shape, "conv_ln_g": conv_ln_g.shape,
              "conv_ln_b": conv_ln_b.shape, "norm_ffn": norm_ffn.shape}

    def widen(a):
        z = jnp.zeros((DEPTH, CONV_WIDTH, N_DEV, CONV_CH // N_DEV), F32)
        z = lax.dynamic_update_slice(z, a[:, :, None, :], (0, 0, me, 0))
        return z.reshape(DEPTH, CONV_WIDTH, CONV_CH)

    sw = _pack_small(dict(norm_mix=norm_mix, q_norm=q_norm, k_norm=k_norm, sinks=sinks, conv_w=widen(conv_w),
                          conv_b=conv_b, conv_ln_g=conv_ln_g, conv_ln_b=conv_ln_b, norm_ffn=norm_ffn))
    sm = _pack_small(dict(norm_mix=m_norm_mix, q_norm=m_q_norm, k_norm=m_k_norm, sinks=m_sinks,
                          conv_w=widen(m_conv_w), conv_b=m_conv_b, conv_ln_g=m_conv_ln_g, conv_ln_b=m_conv_ln_b,
                          norm_ffn=m_norm_ffn))
    sv_ = _pack_small(dict(norm_mix=v_norm_mix, q_norm=v_q_norm, k_norm=v_k_norm, sinks=v_sinks,
                           conv_w=widen(v_conv_w), conv_b=v_conv_b,
                           conv_ln_g=v_conv_ln_g, conv_ln_b=v_conv_ln_b, norm_ffn=v_norm_ffn))
    s_outs = [_unpack_small(o, shapes) for o in _adamw([small_parts], sw, sm, sv_, "adamw_small", SMALL_ROWS)]

    def narrow(a):
        a4 = a.reshape(DEPTH, CONV_WIDTH, N_DEV, CONV_CH // N_DEV)
        return lax.dynamic_slice(a4, (0, 0, me, 0), (DEPTH, CONV_WIDTH, 1, CONV_CH // N_DEV)).reshape(
            DEPTH, CONV_WIDTH, CONV_CH // N_DEV)

    big = {"w_in": u_in, "w_conv_out": u_co, "w_out": u_out, "w_gate_up": u_gu, "w_down": u_dn}
    order = ["norm_mix", "w_in", "q_norm", "k_norm", "sinks", "conv_w", "conv_b", "conv_ln_g", "conv_ln_b",
             "w_conv_out", "w_out", "norm_ffn", "w_gate_up", "w_down"]
    outs = [loss, grad_x]
    for kind in range(4):
        for name in order:
            if name in big:
                outs.append(big[name][kind])
            elif name == "conv_w":
                outs.append(narrow(s_outs[kind][name]))
            else:
                outs.append(s_outs[kind][name])
    return tuple(outs)
```

```python
import functools
import math

import jax
import jax.numpy as jnp
from jax import lax
from jax.experimental import pallas as pl
from jax.experimental.pallas import tpu as pltpu

F32 = jnp.float32
BF16 = jnp.bfloat16

D_MODEL = 1024
DEPTH = 4
N_HEADS = 16
N_KV_HEADS = 2
HEAD_DIM = 64
ROT_DIM = HEAD_DIM // 4
ROPE_THETA = 500000.0
BLOCK = 128
CONV_CH = D_MODEL // 2
CONV_WIDTH = 31
D_FF = 2816
EPS = 1e-6
Q_W = N_HEADS * HEAD_DIM
KV_W = N_KV_HEADS * HEAD_DIM
QKV_W = Q_W + 2 * KV_W
UUG_W = 2 * CONV_CH
GG_W = 2 * D_MODEL
IN_W = QKV_W + UUG_W + GG_W
N_DEV = 8

ADAM_LR = 0.001
ADAM_B1 = 0.9
ADAM_B2 = 0.999
ADAM_EPS = 1e-08
ADAM_WD = 0.01
ADAM_STEP = 10

LANES = 128
SUBLANES = 8
HALO = 32
VMEM_LIMIT = 56 * 1024 * 1024
NEG = -1e30


def _cp(sem=None):
    return pltpu.CompilerParams(dimension_semantics=sem, vmem_limit_bytes=VMEM_LIMIT)


def _sigmoid(z):
    return 1.0 / (1.0 + jnp.exp(-z))


def _rowgroup_sum(z):
    r, c = z.shape
    return jnp.sum(z.reshape(r // SUBLANES, SUBLANES, c), axis=0)


_DIMS = {"nn": (((1,), (0,)), ((), ())), "nt": (((1,), (1,)), ((), ())), "tn": (((0,), (0,)), ((), ()))}


def _mm(a, b, mode, *, out_dtype, name, tm, tn, tk, resid=None, after=None):
    if mode == "nn":
        (m, k), (k2, n) = a.shape, b.shape
        a_spec = pl.BlockSpec((tm, tk), lambda i, j, s: (i, s))
        b_spec = pl.BlockSpec((tk, tn), lambda i, j, s: (s, j))
    elif mode == "nt":
        (m, k), (n, k2) = a.shape, b.shape
        a_spec = pl.BlockSpec((tm, tk), lambda i, j, s: (i, s))
        b_spec = pl.BlockSpec((tn, tk), lambda i, j, s: (j, s))
    elif b.ndim == 3:
        (k, m), (parts, k2, n_part) = a.shape, b.shape
        n, per = parts * n_part, n_part // tn
        a_spec = pl.BlockSpec((tk, tm), lambda i, j, s: (s, i))
        b_spec = pl.BlockSpec((None, tk, tn), lambda i, j, s: (j // per, s, j % per))
    else:
        (k, m), (k2, n) = a.shape, b.shape
        a_spec = pl.BlockSpec((tk, tm), lambda i, j, s: (s, i))
        b_spec = pl.BlockSpec((tk, tn), lambda i, j, s: (s, j))
    assert k == k2 and m % tm == 0 and n % tn == 0 and k % tk == 0, (name, a.shape, b.shape, tm, tn, tk)
    nk = k // tk
    dims = _DIMS[mode]
    has_resid = resid is not None

    def body(*refs):
        a_ref, b_ref = refs[0], refs[1]
        r_ref = refs[2] if has_resid else None
        o_ref = refs[-1] if nk == 1 else refs[-2]
        part = lax.dot_general(a_ref[...].astype(BF16), b_ref[...].astype(BF16), dims, preferred_element_type=F32)

        def finish(acc):
            if has_resid:
                acc = acc + r_ref[...]
            o_ref[...] = acc.astype(out_dtype)

        if nk == 1:
            finish(part)
            return
        acc_ref = refs[-1]
        s = pl.program_id(2)

        @pl.when(s == 0)
        def _():
            acc_ref[...] = part

        @pl.when(s > 0)
        def _():
            acc_ref[...] += part

        @pl.when(s == nk - 1)
        def _():
            finish(acc_ref[...])

    in_specs = [a_spec, b_spec]
    args = [a, b]
    if has_resid:
        in_specs.append(pl.BlockSpec((tm, tn), lambda i, j, s: (i, j)))
        args.append(resid)
    if after is not None:
        in_specs.append(_ANY)
        args.append(after)
    return pl.pallas_call(
        body, name=name, grid=(m // tm, n // tn, nk),
        in_specs=in_specs, out_specs=pl.BlockSpec((tm, tn), lambda i, j, s: (i, j)),
        out_shape=jax.ShapeDtypeStruct((m, n), out_dtype),
        scratch_shapes=[] if nk == 1 else [pltpu.VMEM((tm, tn), F32)],
        compiler_params=_cp(("parallel", "parallel", "arbitrary")),
    )(*args)


def _mm_rows(pairs, mode, *, name, tm, resid=None, norm_fwd=None, norm_bwd=None, after=None):
    pairs = [tuple(p) for p in pairs]
    m = pairs[0][0].shape[-2]
    n = pairs[0][1].shape[1] if mode == "nn" else pairs[0][1].shape[0]
    assert m % tm == 0 and not (norm_fwd is not None and norm_bwd is not None), name
    dims = _DIMS[mode]
    np_ = len(pairs)
    row = pl.BlockSpec((tm, n), lambda i: (i, 0))
    vec = pl.BlockSpec((1, n), lambda i: (0, 0))
    in_specs, args = [], []
    for pair in pairs:
        a, b = pair[0], pair[1]
        k = a.shape[-1]
        assert a.shape[-2] == m, (name, a.shape)
        if len(pair) >= 3:
            assert mode == "nt" and b.shape[0] == n and b.shape[1] % k == 0, (name, a.shape, b.shape)
            b_spec = pl.BlockSpec((n, k), functools.partial(lambda i, j: (0, j), j=pair[2]))
        else:
            assert b.shape == (k, n) if mode == "nn" else b.shape == (n, k), (name, a.shape, b.shape)
            b_spec = pl.BlockSpec(b.shape, lambda i: (0, 0))
        if len(pair) == 4:
            a_spec = pl.BlockSpec((None, tm, k), functools.partial(lambda i, p: (p, i, 0), p=pair[3]))
        else:
            a_spec = pl.BlockSpec((tm, k), lambda i: (i, 0))
        in_specs += [a_spec, b_spec]
        args += [a, b]
    if resid is not None:
        in_specs.append(row)
        args.append(resid)
    if norm_fwd is not None:
        in_specs.append(vec)
        args.append(norm_fwd)
    if norm_bwd is not None:
        in_specs += [row, vec, row]
        args += list(norm_bwd)
    if after is not None:
        in_specs.append(_ANY)
        args.append(after)
    n_out = 1 if (norm_fwd is None and norm_bwd is None) else 2

    def body(*refs):
        outs = refs[len(refs) - n_out:]
        pos = 2 * np_
        acc = None
        for p in range(np_):
            part = lax.dot_general(refs[2 * p][...].astype(BF16), refs[2 * p + 1][...].astype(BF16), dims,
                                   preferred_element_type=F32)
            acc = part if acc is None else acc + part
        if resid is not None:
            acc = acc + refs[pos][...]
            pos += 1
        if norm_fwd is not None:
            r = lax.rsqrt(jnp.mean(acc * acc, axis=-1, keepdims=True) + EPS)
            outs[1][...] = ((acc * r) * refs[pos][...]).astype(BF16)
        if norm_bwd is not None:
            @pl.when(pl.program_id(0) == 0)
            def _():
                outs[1][...] = jnp.zeros_like(outs[1])

            xv = refs[pos][...]
            r = lax.rsqrt(jnp.mean(xv * xv, axis=-1, keepdims=True) + EPS)
            y = xv * r
            outs[1][...] += _rowgroup_sum(acc * y)
            dy = acc * refs[pos + 1][...]
            acc = refs[pos + 2][...] + r * (dy - y * jnp.mean(dy * y, axis=-1, keepdims=True))
        outs[0][...] = acc

    out_specs = [row]
    out_shape = [jax.ShapeDtypeStruct((m, n), F32)]
    if norm_fwd is not None:
        out_specs.append(row)
        out_shape.append(jax.ShapeDtypeStruct((m, n), BF16))
    if norm_bwd is not None:
        out_specs.append(pl.BlockSpec((SUBLANES, n), lambda i: (0, 0)))
        out_shape.append(jax.ShapeDtypeStruct((SUBLANES, n), F32))
    res = pl.pallas_call(
        body, name=name, grid=(m // tm,), in_specs=in_specs, out_specs=out_specs, out_shape=out_shape,
        compiler_params=_cp(("arbitrary",) if norm_bwd is not None else ("parallel",)),
    )(*args)
    return res[0] if n_out == 1 else res


def _rmsnorm_fwd(x, g, name, tm=512):
    t, d = x.shape

    def body(x_ref, g_ref, h_ref):
        xv = x_ref[...]
        r = lax.rsqrt(jnp.mean(xv * xv, axis=-1, keepdims=True) + EPS)
        h_ref[...] = ((xv * r) * g_ref[...]).astype(BF16)

    return pl.pallas_call(
        body, name=name, grid=(t // tm,),
        in_specs=[pl.BlockSpec((tm, d), lambda i: (i, 0)), pl.BlockSpec((1, d), lambda i: (0, 0))],
        out_specs=pl.BlockSpec((tm, d), lambda i: (i, 0)),
        out_shape=jax.ShapeDtypeStruct((t, d), BF16),
        compiler_params=_cp(("parallel",)),
    )(x, g)


def _rmsnorm_bwd(dh, x, g, resid, name, tm=512):
    t, d = x.shape

    def body(dh_ref, x_ref, g_ref, r_ref, dx_ref, dg_ref):
        @pl.when(pl.program_id(0) == 0)
        def _():
            dg_ref[...] = jnp.zeros_like(dg_ref)

        xv = x_ref[...]
        dhv = dh_ref[...]
        r = lax.rsqrt(jnp.mean(xv * xv, axis=-1, keepdims=True) + EPS)
        y = xv * r
        dg_ref[...] += _rowgroup_sum(dhv * y)
        dy = dhv * g_ref[...]
        dx_ref[...] = r_ref[...] + r * (dy - y * jnp.mean(dy * y, axis=-1, keepdims=True))

    row = pl.BlockSpec((tm, d), lambda i: (i, 0))
    return pl.pallas_call(
        body, name=name, grid=(t // tm,),
        in_specs=[row, row, pl.BlockSpec((1, d), lambda i: (0, 0)), row],
        out_specs=[row, pl.BlockSpec((SUBLANES, d), lambda i: (0, 0))],
        out_shape=[jax.ShapeDtypeStruct((t, d), F32), jax.ShapeDtypeStruct((SUBLANES, d), F32)],
        compiler_params=_cp(("arbitrary",)),
    )(dh, x, g, resid)


def _seg_sum(z, bd):
    hi = z.astype(BF16)
    lo = (z - hi.astype(F32)).astype(BF16)
    return jnp.dot(hi, bd, preferred_element_type=F32) + jnp.dot(lo, bd, preferred_element_type=F32)


def _partner(z, lane64):
    return jnp.where(lane64 < ROT_DIM // 2, pltpu.roll(z, LANES - ROT_DIM // 2, 1), pltpu.roll(z, ROT_DIM // 2, 1))


def _rope_tables(t):
    inv_freq = ROPE_THETA ** (-jnp.arange(0, ROT_DIM, 2, dtype=F32) / ROT_DIM)
    ang = jnp.arange(t, dtype=F32)[:, None] * inv_freq[None, :]
    cos, sin = jnp.cos(ang), jnp.sin(ang)
    c64 = jnp.concatenate([cos, cos, jnp.ones((t, HEAD_DIM - ROT_DIM), F32)], axis=1)
    s64 = jnp.concatenate([-sin, sin, jnp.zeros((t, HEAD_DIM - ROT_DIM), F32)], axis=1)
    return jnp.tile(c64, (1, 2)), jnp.tile(s64, (1, 2))


def _block_diag_ones():
    r = lax.broadcasted_iota(jnp.int32, (LANES, LANES), 0) // HEAD_DIM
    c = lax.broadcasted_iota(jnp.int32, (LANES, LANES), 1) // HEAD_DIM
    return (r == c).astype(BF16)


def _qk_prep_fwd(qkv, gq, gk, ctab, stab, bd, name, tm=512):
    t = qkv.shape[0]
    scale = HEAD_DIM ** -0.5
    n_qg = Q_W // LANES

    def body(q_ref, kv_ref, gq_ref, gk_ref, c_ref, s_ref, bd_ref, qs_ref, kd_ref, vd_ref):
        lane = lax.broadcasted_iota(jnp.int32, (tm, LANES), 1)
        lane64 = lane % HEAD_DIM
        lo_half = lane < HEAD_DIM
        cv, sv, bdv = c_ref[...], s_ref[...], bd_ref[...]

        def norm_rope(xg, g):
            r = lax.rsqrt(_seg_sum(xg * xg, bdv) * (1.0 / HEAD_DIM) + EPS)
            yn = (xg * r) * g
            return yn * cv + _partner(yn, lane64) * sv

        zero = jnp.zeros((BLOCK, LANES), BF16)
        lo_blk = lo_half[0:BLOCK]
        for c in range(n_qg):
            xg = q_ref[:, c * LANES:(c + 1) * LANES]
            qn = (norm_rope(xg, gq_ref[...]) * scale).astype(BF16)
            for b in range(tm // BLOCK):
                rows = qn[b * BLOCK:(b + 1) * BLOCK]
                qs_ref[b, 2 * c] = jnp.where(lo_blk, rows, zero)
                qs_ref[b, 2 * c + 1] = jnp.where(lo_blk, zero, rows)
        kk = norm_rope(kv_ref[:, 0:LANES], gk_ref[...])
        kr = pltpu.roll(kk, HEAD_DIM, 1)
        kd_ref[:, 0:LANES] = jnp.where(lo_half, kk, kr).astype(BF16)
        kd_ref[:, LANES:2 * LANES] = jnp.where(lo_half, kr, kk).astype(BF16)
        vv = kv_ref[:, LANES:2 * LANES]
        vr = pltpu.roll(vv, HEAD_DIM, 1)
        vd_ref[:, 0:LANES] = jnp.where(lo_half, vv, vr).astype(BF16)
        vd_ref[:, LANES:2 * LANES] = jnp.where(lo_half, vr, vv).astype(BF16)

    vec = pl.BlockSpec((1, LANES), lambda i: (0, 0))
    tab = pl.BlockSpec((tm, LANES), lambda i: (i, 0))
    return pl.pallas_call(
        body, name=name, grid=(t // tm,),
        in_specs=[pl.BlockSpec((tm, Q_W), lambda i: (i, 0)),
                  pl.BlockSpec((tm, 2 * KV_W), lambda i: (i, Q_W // (2 * KV_W))),
                  vec, vec, tab, tab, pl.BlockSpec((LANES, LANES), lambda i: (0, 0))],
        out_specs=[pl.BlockSpec((tm // BLOCK, N_HEADS, BLOCK, LANES), lambda i: (i, 0, 0, 0)),
                   pl.BlockSpec((tm, 2 * LANES), lambda i: (i, 0)),
                   pl.BlockSpec((tm, 2 * LANES), lambda i: (i, 0))],
        out_shape=[jax.ShapeDtypeStruct((t // BLOCK, N_HEADS, BLOCK, LANES), BF16),
                   jax.ShapeDtypeStruct((t, 2 * LANES), BF16), jax.ShapeDtypeStruct((t, 2 * LANES), BF16)],
        compiler_params=_cp(("parallel",)),
    )(qkv, qkv, gq, gk, ctab, stab, bd)


def _qk_prep_bwd(dqs, dkd, dvd, qkv, gq, gk, ctab, stab, bd, name, tm=512):
    t = qkv.shape[0]
    scale = HEAD_DIM ** -0.5
    n_qg = Q_W // LANES

    def body(dqs_ref, dkd_ref, dvd_ref, q_ref, kv_ref, gq_ref, gk_ref, c_ref, s_ref, bd_ref,
             dqkv_ref, dgq_ref, dgk_ref):
        @pl.when(pl.program_id(0) == 0)
        def _():
            dgq_ref[...] = jnp.zeros_like(dgq_ref)
            dgk_ref[...] = jnp.zeros_like(dgk_ref)

        lane = lax.broadcasted_iota(jnp.int32, (tm, LANES), 1)
        lane64 = lane % HEAD_DIM
        lo_half = lane < HEAD_DIM
        cv, sv, bdv = c_ref[...], s_ref[...], bd_ref[...]

        def bwd(xg, g, dout):
            r = lax.rsqrt(_seg_sum(xg * xg, bdv) * (1.0 / HEAD_DIM) + EPS)
            y = xg * r
            dyn = dout * cv + jnp.where(lane64 < ROT_DIM, _partner(dout * sv, lane64), 0.0)
            dy = dyn * g
            dx = r * (dy - y * (_seg_sum(dy * y, bdv) * (1.0 / HEAD_DIM)))
            return dx, _rowgroup_sum(dyn * y)

        dgq = jnp.zeros((SUBLANES, LANES), F32)
        for c in range(n_qg):
            sl = slice(c * LANES, (c + 1) * LANES)
            dx, dg = bwd(q_ref[:, sl], gq_ref[...], dqs_ref[:, sl] * scale)
            dqkv_ref[:, sl] = dx.astype(BF16)
            dgq = dgq + dg
        dgq_ref[...] += dgq
        dk = jnp.where(lo_half, dkd_ref[:, 0:LANES], dkd_ref[:, LANES:2 * LANES])
        dx, dg = bwd(kv_ref[:, 0:LANES], gk_ref[...], dk)
        dqkv_ref[:, Q_W:Q_W + LANES] = dx.astype(BF16)
        dgk_ref[...] += dg
        dv = jnp.where(lo_half, dvd_ref[:, 0:LANES], dvd_ref[:, LANES:2 * LANES])
        dqkv_ref[:, Q_W + LANES:Q_W + 2 * LANES] = dv.astype(BF16)

    vec = pl.BlockSpec((1, LANES), lambda i: (0, 0))
    tab = pl.BlockSpec((tm, LANES), lambda i: (i, 0))
    wide = pl.BlockSpec((tm, 2 * LANES), lambda i: (i, 0))
    acc = pl.BlockSpec((SUBLANES, LANES), lambda i: (0, 0))
    return pl.pallas_call(
        body, name=name, grid=(t // tm,),
        in_specs=[pl.BlockSpec((tm, Q_W), lambda i: (i, 0)), wide, wide,
                  pl.BlockSpec((tm, Q_W), lambda i: (i, 0)),
                  pl.BlockSpec((tm, 2 * KV_W), lambda i: (i, Q_W // (2 * KV_W))),
                  vec, vec, tab, tab, pl.BlockSpec((LANES, LANES), lambda i: (0, 0))],
        out_specs=[pl.BlockSpec((tm, QKV_W), lambda i: (i, 0)), acc, acc],
        out_shape=[jax.ShapeDtypeStruct((t, QKV_W), BF16), jax.ShapeDtypeStruct((SUBLANES, LANES), F32),
                   jax.ShapeDtypeStruct((SUBLANES, LANES), F32)],
        compiler_params=_cp(("arbitrary",)),
    )(dqs, dkd, dvd, qkv, qkv, gq, gk, ctab, stab, bd)


GROUP = N_HEADS // N_KV_HEADS
GROUP_ROWS = GROUP * BLOCK


def _attn_masks():
    row = lax.broadcasted_iota(jnp.int32, (BLOCK, BLOCK), 0)
    col = lax.broadcasted_iota(jnp.int32, (BLOCK, BLOCK), 1)
    return col <= row, col < HEAD_DIM


def _window_softmax(s_c, s_p, is_cur, has_prev, sink):
    s = jnp.where(is_cur, s_c, jnp.where(has_prev, s_p, NEG))
    m = jnp.maximum(jnp.max(s, axis=-1, keepdims=True), sink)
    e = jnp.exp(s - m)
    e_s = jnp.exp(sink - m)
    inv = 1.0 / (jnp.sum(e, axis=-1, keepdims=True) + e_s)
    return e * inv, e_s * inv


def _attn_fwd(qs, kd, vd, sinks, name):
    nb = qs.shape[0]
    t = nb * BLOCK

    def body(sink_ref, q_ref, kc_ref, kp_ref, vc_ref, vp_ref, o_ref, pc_scr, pp_scr):
        has_prev = pl.program_id(0) > 0
        is_cur, lo_half = _attn_masks()
        for j in range(N_KV_HEADS):
            ks = slice(j * LANES, (j + 1) * LANES)
            qg = q_ref[0, j * GROUP:(j + 1) * GROUP].reshape(GROUP_ROWS, LANES)
            s_c = lax.dot_general(qg, kc_ref[:, ks], _DIMS["nt"], preferred_element_type=F32)
            s_p = lax.dot_general(qg, kp_ref[:, ks], _DIMS["nt"], preferred_element_type=F32)
            for g in range(GROUP):
                rs = slice(g * BLOCK, (g + 1) * BLOCK)
                p, _ = _window_softmax(s_c[rs], s_p[rs], is_cur, has_prev, sink_ref[j * GROUP + g])
                rg = slice(j * GROUP_ROWS + g * BLOCK, j * GROUP_ROWS + (g + 1) * BLOCK)
                pc_scr[rg, :] = jnp.where(is_cur, p, 0.0).astype(BF16)
                pp_scr[rg, :] = jnp.where(is_cur, 0.0, p).astype(BF16)
            grp = slice(j * GROUP_ROWS, (j + 1) * GROUP_ROWS)
            o2 = (jnp.dot(pc_scr[grp, :], vc_ref[:, ks], preferred_element_type=F32)
                  + jnp.dot(pp_scr[grp, :], vp_ref[:, ks], preferred_element_type=F32))
            for pp in range(GROUP // 2):
                c0 = (j * (GROUP // 2) + pp) * LANES
                o_ref[:, c0:c0 + LANES] = jnp.where(lo_half, o2[2 * pp * BLOCK:(2 * pp + 1) * BLOCK],
                                                    o2[(2 * pp + 1) * BLOCK:(2 * pp + 2) * BLOCK])

    cur = lambda i: (i, 0)
    prev = lambda i: (jnp.maximum(i - 1, 0), 0)
    kvs = (BLOCK, 2 * LANES)
    return pl.pallas_call(
        body, name=name, grid=(nb,),
        in_specs=[pl.BlockSpec(memory_space=pltpu.SMEM),
                  pl.BlockSpec((1, N_HEADS, BLOCK, LANES), lambda i: (i, 0, 0, 0)),
                  pl.BlockSpec(kvs, cur), pl.BlockSpec(kvs, prev), pl.BlockSpec(kvs, cur), pl.BlockSpec(kvs, prev)],
        out_specs=pl.BlockSpec((BLOCK, Q_W), cur),
        out_shape=jax.ShapeDtypeStruct((t, Q_W), F32),
        scratch_shapes=[pltpu.VMEM((N_KV_HEADS * GROUP_ROWS, LANES), BF16)] * 2,
        compiler_params=_cp(("parallel",)),
    )(sinks, qs, kd, kd, vd, vd)


def _attn_bwd(qs, kd, vd, sinks, do, name):
    nb = qs.shape[0]
    t = nb * BLOCK

    def body(sink_ref, q_ref, do_ref, kc_ref, kp_ref, vc_ref, vp_ref,
             dq_ref, dk_ref, dv_ref, dsink_ref,
             carry_k, carry_v, dsink_acc, do_scr, pc_scr, pp_scr, dsc_scr, dsp_scr):
        i = pl.program_id(0)

        @pl.when(i == 0)
        def _():
            carry_k[...] = jnp.zeros_like(carry_k)
            carry_v[...] = jnp.zeros_like(carry_v)
            dsink_acc[...] = jnp.zeros_like(dsink_acc)

        @pl.when(i < nb)
        def _():
            has_prev = i > 0
            is_cur, lo_half = _attn_masks()
            srow = lax.broadcasted_iota(jnp.int32, (SUBLANES, LANES), 0)
            scol = lax.broadcasted_iota(jnp.int32, (SUBLANES, LANES), 1)
            dsink = jnp.zeros((SUBLANES, LANES), F32)
            for j in range(N_KV_HEADS):
                ks = slice(j * LANES, (j + 1) * LANES)
                kc, kp, vc, vp = kc_ref[:, ks], kp_ref[:, ks], vc_ref[:, ks], vp_ref[:, ks]
                qg = q_ref[0, j * GROUP:(j + 1) * GROUP].reshape(GROUP_ROWS, LANES)
                for pp in range(GROUP // 2):
                    c0 = (j * (GROUP // 2) + pp) * LANES
                    dop = do_ref[:, c0:c0 + LANES]
                    r0 = j * GROUP_ROWS + 2 * pp * BLOCK
                    do_scr[r0:r0 + BLOCK, :] = jnp.where(lo_half, dop, 0.0).astype(BF16)
                    do_scr[r0 + BLOCK:r0 + 2 * BLOCK, :] = jnp.where(lo_half, 0.0, dop).astype(BF16)
                grp = slice(j * GROUP_ROWS, (j + 1) * GROUP_ROWS)
                dog = do_scr[grp, :]
                s_c = lax.dot_general(qg, kc, _DIMS["nt"], preferred_element_type=F32)
                s_p = lax.dot_general(qg, kp, _DIMS["nt"], preferred_element_type=F32)
                dp_c = lax.dot_general(dog, vc, _DIMS["nt"], preferred_element_type=F32)
                dp_p = lax.dot_general(dog, vp, _DIMS["nt"], preferred_element_type=F32)
                for g in range(GROUP):
                    rs = slice(g * BLOCK, (g + 1) * BLOCK)
                    h = j * GROUP + g
                    p, p_s = _window_softmax(s_c[rs], s_p[rs], is_cur, has_prev, sink_ref[h])
                    dp = jnp.where(is_cur, dp_c[rs], dp_p[rs])
                    delta = jnp.sum(p * dp, axis=-1, keepdims=True)
                    ds = p * (dp - delta)
                    dsv = -jnp.sum(p_s * delta, axis=0, keepdims=True)
                    dsink = dsink + jnp.where(jnp.logical_and(srow == 0, scol == h), dsv, 0.0)
                    rg = slice(j * GROUP_ROWS + g * BLOCK, j * GROUP_ROWS + (g + 1) * BLOCK)
                    pc_scr[rg, :] = jnp.where(is_cur, p, 0.0).astype(BF16)
                    pp_scr[rg, :] = jnp.where(is_cur, 0.0, p).astype(BF16)
                    dsc_scr[rg, :] = jnp.where(is_cur, ds, 0.0).astype(BF16)
                    dsp_scr[rg, :] = jnp.where(is_cur, 0.0, ds).astype(BF16)
                dsc, dsp = dsc_scr[grp, :], dsp_scr[grp, :]
                dq2 = jnp.dot(dsc, kc, preferred_element_type=F32) + jnp.dot(dsp, kp, preferred_element_type=F32)
                for pp in range(GROUP // 2):
                    c0 = (j * (GROUP // 2) + pp) * LANES
                    dq_ref[:, c0:c0 + LANES] = jnp.where(lo_half, dq2[2 * pp * BLOCK:(2 * pp + 1) * BLOCK],
                                                         dq2[(2 * pp + 1) * BLOCK:(2 * pp + 2) * BLOCK])
                dk_c = lax.dot_general(dsc, qg, _DIMS["tn"], preferred_element_type=F32)
                dk_p = lax.dot_general(dsp, qg, _DIMS["tn"], preferred_element_type=F32)
                dv_c = lax.dot_general(pc_scr[grp, :], dog, _DIMS["tn"], preferred_element_type=F32)
                dv_p = lax.dot_general(pp_scr[grp, :], dog, _DIMS["tn"], preferred_element_type=F32)
                dk_ref[:, ks] = carry_k[:, ks] + dk_p + pltpu.roll(dk_p, HEAD_DIM, 1)
                dv_ref[:, ks] = carry_v[:, ks] + dv_p + pltpu.roll(dv_p, HEAD_DIM, 1)
                carry_k[:, ks] = dk_c + pltpu.roll(dk_c, HEAD_DIM, 1)
                carry_v[:, ks] = dv_c + pltpu.roll(dv_c, HEAD_DIM, 1)
            dsink_acc[...] += dsink

        @pl.when(i == nb)
        def _():
            dk_ref[...] = carry_k[...]
            dv_ref[...] = carry_v[...]
            dsink_ref[...] = dsink_acc[...]

    last = nb - 1
    cur = lambda i: (jnp.minimum(i, last), 0)
    prev = lambda i: (jnp.clip(i - 1, 0, last), 0)
    kvs = (BLOCK, 2 * LANES)
    stk = pltpu.VMEM((N_KV_HEADS * GROUP_ROWS, LANES), BF16)
    return pl.pallas_call(
        body, name=name, grid=(nb + 1,),
        in_specs=[pl.BlockSpec(memory_space=pltpu.SMEM),
                  pl.BlockSpec((1, N_HEADS, BLOCK, LANES), lambda i: (jnp.minimum(i, last), 0, 0, 0)),
                  pl.BlockSpec((BLOCK, Q_W), cur),
                  pl.BlockSpec(kvs, cur), pl.BlockSpec(kvs, prev), pl.BlockSpec(kvs, cur), pl.BlockSpec(kvs, prev)],
        out_specs=[pl.BlockSpec((BLOCK, Q_W), cur), pl.BlockSpec(kvs, prev), pl.BlockSpec(kvs, prev),
                   pl.BlockSpec((SUBLANES, LANES), lambda i: (0, 0))],
        out_shape=[jax.ShapeDtypeStruct((t, Q_W), F32), jax.ShapeDtypeStruct((t, 2 * LANES), F32),
                   jax.ShapeDtypeStruct((t, 2 * LANES), F32), jax.ShapeDtypeStruct((SUBLANES, LANES), F32)],
        scratch_shapes=[pltpu.VMEM(kvs, F32), pltpu.VMEM(kvs, F32), pltpu.VMEM((SUBLANES, LANES), F32),
                        stk, stk, stk, stk, stk],
        compiler_params=_cp(("arbitrary",)),
    )(sinks, qs, do, kd, kd, vd, vd)


CONV_CHUNK = 64


def _fill_row_shifts(sh):
    rows = sh.shape[1] - SUBLANES
    for r in range(1, SUBLANES):
        sh[r, 0:rows, :] = sh[0, r:r + rows, :]


def _shifted_rows(sh, start, size):
    q, r = divmod(start, SUBLANES)
    return sh[r, q * SUBLANES:q * SUBLANES + size, :]


def _conv_fwd(uug, w32, cb, lg, lb, name, tm=512):
    t = uug.shape[0]
    hb = tm // HALO

    def body(m_ref, h_ref, w_ref, cb_ref, lg_ref, lb_ref, y0_ref, y2_ref, a_sh):
        i = pl.program_id(0)
        a_sh[0, HALO:, :] = m_ref[:, 0:CONV_CH] * _sigmoid(m_ref[:, CONV_CH:])
        ah = h_ref[:, 0:CONV_CH] * _sigmoid(h_ref[:, CONV_CH:])
        a_sh[0, 0:HALO, :] = jnp.where(i > 0, ah, 0.0)
        _fill_row_shifts(a_sh)
        off = HALO - (CONV_WIDTH - 1)
        for c in range(tm // CONV_CHUNK):
            r0 = c * CONV_CHUNK
            acc = jnp.zeros((CONV_CHUNK, CONV_CH), F32)
            for k in range(CONV_WIDTH):
                acc = acc + w_ref[k:k + 1, :] * _shifted_rows(a_sh, r0 + off + k, CONV_CHUNK)
            y0 = acc + cb_ref[...]
            y0_ref[r0:r0 + CONV_CHUNK, :] = y0
            mu = jnp.mean(y0, axis=-1, keepdims=True)
            dlt = y0 - mu
            rstd = lax.rsqrt(jnp.mean(dlt * dlt, axis=-1, keepdims=True) + EPS)
            y1 = (dlt * rstd) * lg_ref[...] + lb_ref[...]
            y2_ref[r0:r0 + CONV_CHUNK, :] = (y1 * _sigmoid(y1)).astype(BF16)

    vec = pl.BlockSpec((1, CONV_CH), lambda i: (0, 0))
    return pl.pallas_call(
        body, name=name, grid=(t // tm,),
        in_specs=[pl.BlockSpec((tm, UUG_W), lambda i: (i, 0)),
                  pl.BlockSpec((HALO, UUG_W), lambda i: (jnp.maximum(i * hb - 1, 0), 0)),
                  pl.BlockSpec((HALO, CONV_CH), lambda i: (0, 0)), vec, vec, vec],
        out_specs=[pl.BlockSpec((tm, CONV_CH), lambda i: (i, 0)), pl.BlockSpec((tm, CONV_CH), lambda i: (i, 0))],
        out_shape=[jax.ShapeDtypeStruct((t, CONV_CH), F32), jax.ShapeDtypeStruct((t, CONV_CH), BF16)],
        scratch_shapes=[pltpu.VMEM((SUBLANES, tm + HALO, CONV_CH), F32)],
        compiler_params=_cp(("parallel",)),
    )(uug, uug, w32, cb, lg, lb)


def _conv_bwd_ln(dc_out, w_conv_out, y0, lg, lb, name, tm=512):
    t = y0.shape[0]
    d = dc_out.shape[1]

    def body(dc_ref, w_ref, y0_ref, lg_ref, lb_ref, dy0_ref, dlg_ref, dlb_ref, dcb_ref):
        @pl.when(pl.program_id(0) == 0)
        def _():
            dlg_ref[...] = jnp.zeros_like(dlg_ref)
            dlb_ref[...] = jnp.zeros_like(dlb_ref)
            dcb_ref[...] = jnp.zeros_like(dcb_ref)

        y0 = y0_ref[...]
        mu = jnp.mean(y0, axis=-1, keepdims=True)
        dlt = y0 - mu
        rstd = lax.rsqrt(jnp.mean(dlt * dlt, axis=-1, keepdims=True) + EPS)
        yh = dlt * rstd
        y1 = yh * lg_ref[...] + lb_ref[...]
        sg = _sigmoid(y1)
        dy2 = lax.dot_general(dc_ref[...], w_ref[...], _DIMS["nt"], preferred_element_type=F32)
        dy1 = dy2 * (sg * (1.0 + y1 * (1.0 - sg)))
        dlg_ref[...] += _rowgroup_sum(dy1 * yh)
        dlb_ref[...] += _rowgroup_sum(dy1)
        dyh = dy1 * lg_ref[...]
        dy0 = rstd * (dyh - jnp.mean(dyh, axis=-1, keepdims=True)
                      - yh * jnp.mean(dyh * yh, axis=-1, keepdims=True))
        dcb_ref[...] += _rowgroup_sum(dy0)
        dy0_ref[...] = dy0

    row = pl.BlockSpec((tm, CONV_CH), lambda i: (i, 0))
    vec = pl.BlockSpec((1, CONV_CH), lambda i: (0, 0))
    acc = pl.BlockSpec((SUBLANES, CONV_CH), lambda i: (0, 0))
    accs = jax.ShapeDtypeStruct((SUBLANES, CONV_CH), F32)
    return pl.pallas_call(
        body, name=name, grid=(t // tm,),
        in_specs=[pl.BlockSpec((tm, d), lambda i: (i, 0)), pl.BlockSpec((CONV_CH, d), lambda i: (0, 0)), row, vec, vec],
        out_specs=[row, acc, acc, acc],
        out_shape=[jax.ShapeDtypeStruct((t, CONV_CH), F32), accs, accs, accs],
        compiler_params=_cp(("arbitrary",)),
    )(dc_out, w_conv_out, y0, lg, lb)


def _conv_bwd_taps(dy0, uug, w32, name, tm=512):
    t = uug.shape[0]
    hb = tm // HALO
    n_halo_blocks = t // HALO
    nt = t // tm

    def body(dm_ref, dn_ref, m_ref, h_ref, w_ref, duug_ref, dw_ref, a_sh, d_sh):
        i = pl.program_id(0)

        @pl.when(i == 0)
        def _():
            dw_ref[...] = jnp.zeros_like(dw_ref)

        u = m_ref[:, 0:CONV_CH]
        sg = _sigmoid(m_ref[:, CONV_CH:])
        a_sh[0, HALO:, :] = u * sg
        ah = h_ref[:, 0:CONV_CH] * _sigmoid(h_ref[:, CONV_CH:])
        a_sh[0, 0:HALO, :] = jnp.where(i > 0, ah, 0.0)
        d_sh[0, 0:tm, :] = dm_ref[...]
        d_sh[0, tm:, :] = jnp.where(i < nt - 1, dn_ref[...], 0.0)
        _fill_row_shifts(a_sh)
        _fill_row_shifts(d_sh)
        off = HALO - (CONV_WIDTH - 1)
        for c in range(tm // CONV_CHUNK):
            r0 = c * CONV_CHUNK
            da = jnp.zeros((CONV_CHUNK, CONV_CH), F32)
            for k in range(CONV_WIDTH):
                sh = CONV_WIDTH - 1 - k
                da = da + w_ref[k:k + 1, :] * _shifted_rows(d_sh, r0 + sh, CONV_CHUNK)
            uc = u[r0:r0 + CONV_CHUNK, :]
            sc = sg[r0:r0 + CONV_CHUNK, :]
            duug_ref[r0:r0 + CONV_CHUNK, 0:CONV_CH] = (da * sc).astype(BF16)
            duug_ref[r0:r0 + CONV_CHUNK, CONV_CH:] = (da * uc * sc * (1.0 - sc)).astype(BF16)
            dch = d_sh[0, r0:r0 + CONV_CHUNK, :]
            for k in range(CONV_WIDTH):
                prod = dch * _shifted_rows(a_sh, r0 + off + k, CONV_CHUNK)
                dw_ref[k * SUBLANES:(k + 1) * SUBLANES, :] += _rowgroup_sum(prod)

    return pl.pallas_call(
        body, name=name, grid=(nt,),
        in_specs=[pl.BlockSpec((tm, CONV_CH), lambda i: (i, 0)),
                  pl.BlockSpec((HALO, CONV_CH), lambda i: (jnp.minimum((i + 1) * hb, n_halo_blocks - 1), 0)),
                  pl.BlockSpec((tm, UUG_W), lambda i: (i, 0)),
                  pl.BlockSpec((HALO, UUG_W), lambda i: (jnp.maximum(i * hb - 1, 0), 0)),
                  pl.BlockSpec((HALO, CONV_CH), lambda i: (0, 0))],
        out_specs=[pl.BlockSpec((tm, UUG_W), lambda i: (i, 0)),
                   pl.BlockSpec((CONV_WIDTH * SUBLANES, CONV_CH), lambda i: (0, 0))],
        out_shape=[jax.ShapeDtypeStruct((t, UUG_W), BF16),
                   jax.ShapeDtypeStruct((CONV_WIDTH * SUBLANES, CONV_CH), F32)],
        scratch_shapes=[pltpu.VMEM((SUBLANES, tm + HALO, CONV_CH), F32),
                        pltpu.VMEM((SUBLANES, tm + HALO, CONV_CH), F32)],
        compiler_params=_cp(("arbitrary",)),
    )(dy0, dy0, uug, uug, w32)


def _merge_fwd(y2, w_conv_out, a_out, gg, name, tm=512):
    t, d = a_out.shape
    k = y2.shape[1]

    def body(y_ref, w_ref, a_ref, g_ref, c_ref, o_ref):
        c = jnp.dot(y_ref[...], w_ref[...], preferred_element_type=F32)
        c_ref[...] = c
        o_ref[...] = (_sigmoid(g_ref[:, 0:d]) * a_ref[...] + _sigmoid(g_ref[:, d:]) * c).astype(BF16)

    row = pl.BlockSpec((tm, d), lambda i: (i, 0))
    return pl.pallas_call(
        body, name=name, grid=(t // tm,),
        in_specs=[pl.BlockSpec((tm, k), lambda i: (i, 0)), pl.BlockSpec((k, d), lambda i: (0, 0)), row,
                  pl.BlockSpec((tm, 2 * d), lambda i: (i, 0))],
        out_specs=[row, row],
        out_shape=[jax.ShapeDtypeStruct((t, d), F32), jax.ShapeDtypeStruct((t, d), BF16)],
        compiler_params=_cp(("parallel",)),
    )(y2, w_conv_out, a_out, gg)


def _merge_bwd(dx1, w_out, a_out, c_out, gg, name, tm=512):
    t, d = a_out.shape

    def body(dx_ref, w_ref, a_ref, c_ref, g_ref, da_ref, dc_ref, dg_ref):
        dmv = lax.dot_general(dx_ref[...].astype(BF16), w_ref[...], _DIMS["nt"], preferred_element_type=F32)
        sa = _sigmoid(g_ref[:, 0:d])
        sb = _sigmoid(g_ref[:, d:])
        da_ref[...] = dmv * sa
        dc_ref[...] = (dmv * sb).astype(BF16)
        dg_ref[:, 0:d] = (dmv * a_ref[...] * sa * (1.0 - sa)).astype(BF16)
        dg_ref[:, d:] = (dmv * c_ref[...] * sb * (1.0 - sb)).astype(BF16)

    row = pl.BlockSpec((tm, d), lambda i: (i, 0))
    wide = pl.BlockSpec((tm, 2 * d), lambda i: (i, 0))
    return pl.pallas_call(
        body, name=name, grid=(t // tm,),
        in_specs=[row, pl.BlockSpec((d, d), lambda i: (0, 0)), row, row, wide], out_specs=[row, row, wide],
        out_shape=[jax.ShapeDtypeStruct((t, d), F32), jax.ShapeDtypeStruct((t, d), BF16),
                   jax.ShapeDtypeStruct((t, 2 * d), BF16)],
        compiler_params=_cp(("parallel",)),
    )(dx1, w_out, a_out, c_out, gg)


FF_TN = 1408


def _ffn_up_fwd(h2, wgu, name, tm=512, after=None):
    t, d = h2.shape
    nj = D_FF // FF_TN

    def body(*refs):
        h_ref, wg_ref, wu_ref = refs[:3]
        o_ref, g_ref, u_ref = refs[-3:]
        hv = h_ref[...]
        g = jnp.dot(hv, wg_ref[...], preferred_element_type=F32)
        u = jnp.dot(hv, wu_ref[...], preferred_element_type=F32)
        o_ref[...] = ((g * _sigmoid(g)) * u).astype(BF16)
        g_ref[...] = g.astype(BF16)
        u_ref[...] = u.astype(BF16)

    tile = pl.BlockSpec((tm, FF_TN), lambda j, i: (i, j))
    o = jax.ShapeDtypeStruct((t, D_FF), BF16)
    return pl.pallas_call(
        body, name=name, grid=(nj, t // tm),
        in_specs=[pl.BlockSpec((tm, d), lambda j, i: (i, 0)),
                  pl.BlockSpec((d, FF_TN), lambda j, i: (0, j)),
                  pl.BlockSpec((d, FF_TN), lambda j, i: (0, j + nj))] + ([] if after is None else [_ANY]),
        out_specs=[tile, tile, tile], out_shape=[o, o, o],
        compiler_params=_cp(("parallel", "parallel")),
    )(h2, wgu, wgu, *([] if after is None else [after]))


def _ffn_bwd_mid(g, u, dx2, wd, name, tm=512, after=None):
    t, d = dx2.shape
    nj = D_FF // FF_TN

    def body(*refs):
        g_ref, u_ref, dx_ref, wd_ref = refs[:4]
        dgu_ref = refs[-1]
        gv = g_ref[...].astype(F32)
        uv = u_ref[...].astype(F32)
        dact = lax.dot_general(dx_ref[...].astype(BF16), wd_ref[...], _DIMS["nt"], preferred_element_type=F32)
        sg = _sigmoid(gv)
        dgu_ref[0] = (dact * uv * (sg * (1.0 + gv * (1.0 - sg)))).astype(BF16)
        dgu_ref[1] = (dact * (gv * sg)).astype(BF16)

    tile = pl.BlockSpec((tm, FF_TN), lambda j, i: (i, j))
    o = jax.ShapeDtypeStruct((t, D_FF), BF16)
    return pl.pallas_call(
        body, name=name, grid=(nj, t // tm),
        in_specs=[tile, tile, pl.BlockSpec((tm, d), lambda j, i: (i, 0)),
                  pl.BlockSpec((FF_TN, d), lambda j, i: (j, 0))] + ([] if after is None else [_ANY]),
        out_specs=pl.BlockSpec((2, tm, FF_TN), lambda j, i: (0, i, j)),
        out_shape=jax.ShapeDtypeStruct((2, t, D_FF), BF16),
        compiler_params=_cp(("parallel", "parallel")),
    )(g, u, dx2, wd, *([] if after is None else [after]))


def _loss_head(y, target, name, tm=512):
    t, d = y.shape

    def body(y_ref, t_ref, dy_ref, loss_ref):
        @pl.when(pl.program_id(0) == 0)
        def _():
            loss_ref[...] = jnp.zeros_like(loss_ref)

        e = y_ref[...] - t_ref[...]
        dy_ref[...] = e * (1.0 / d)
        s = _rowgroup_sum(e * e)
        acc = s[:, 0:LANES]
        for c in range(1, d // LANES):
            acc = acc + s[:, c * LANES:(c + 1) * LANES]
        loss_ref[...] += acc * (0.5 / d)

    row = pl.BlockSpec((tm, d), lambda i: (i, 0))
    return pl.pallas_call(
        body, name=name, grid=(t // tm,),
        in_specs=[row, row], out_specs=[row, pl.BlockSpec((SUBLANES, LANES), lambda i: (0, 0))],
        out_shape=[jax.ShapeDtypeStruct((t, d), F32), jax.ShapeDtypeStruct((SUBLANES, LANES), F32)],
        compiler_params=_cp(("arbitrary",)),
    )(y, target)


def _exchange(arrays, scatter, name):
    n = len(arrays)

    def body(*refs):
        ins, outs = refs[:n], refs[n:2 * n]
        send_sems, recv_sems, local_sems = refs[2 * n:]
        x, y, c = lax.axis_index("x"), lax.axis_index("y"), lax.axis_index("c")
        me = 4 * x + 2 * y + c

        def peer(k):
            px, py, pc = x ^ ((k >> 2) & 1), y ^ ((k >> 1) & 1), c ^ (k & 1)
            return (px, py, pc), 4 * px + 2 * py + pc

        def src(a, dst_id):
            return ins[a].at[dst_id] if scatter else ins[a]

        locals_ = [pltpu.make_async_copy(src(a, me), outs[a].at[me], local_sems.at[a]) for a in range(n)]
        for cp in locals_:
            cp.start()
        sends = []
        for k in range(1, N_DEV):
            dev, pid = peer(k)
            for a in range(n):
                sends.append(pltpu.make_async_remote_copy(
                    src_ref=src(a, pid), dst_ref=outs[a].at[me],
                    send_sem=send_sems.at[a, k], recv_sem=recv_sems.at[a, k],
                    device_id=dev, device_id_type=pl.DeviceIdType.MESH))
        for cp in sends:
            cp.start()
        for k in range(1, N_DEV):
            dev, pid = peer(k)
            for a in range(n):
                pltpu.make_async_remote_copy(
                    src_ref=src(a, pid), dst_ref=outs[a].at[pid],
                    send_sem=send_sems.at[a, k], recv_sem=recv_sems.at[a, k],
                    device_id=dev, device_id_type=pl.DeviceIdType.MESH).wait_recv()
        for cp in sends:
            cp.wait_send()
        for cp in locals_:
            cp.wait()

    def out_shape(a):
        return jax.ShapeDtypeStruct(a.shape if scatter else (N_DEV,) + a.shape, a.dtype)

    anyspec = pl.BlockSpec(memory_space=pl.ANY)
    return pl.pallas_call(
        body, name=name,
        in_specs=[anyspec] * n, out_specs=[anyspec] * n,
        out_shape=[out_shape(a) for a in arrays],
        scratch_shapes=[pltpu.SemaphoreType.DMA((n, N_DEV)), pltpu.SemaphoreType.DMA((n, N_DEV)),
                        pltpu.SemaphoreType.DMA((n,))],
    )(*arrays)


def _gather_two_level(arrays, name):
    n = len(arrays)
    nk = N_DEV - 1

    def body(*refs):
        ins, outs = refs[:n], refs[n:2 * n]
        send_sems, recv_sems, local_sems = refs[2 * n:]
        x, y, c = lax.axis_index("x"), lax.axis_index("y"), lax.axis_index("c")
        me, sibling = (x, y, c), (x, y, 1 - c)
        chips = [(1 - x, y), (x, 1 - y), (1 - x, 1 - y)]

        def slot(a, dev):
            return outs[a].at[4 * dev[0] + 2 * dev[1] + dev[2]]

        def copy(a, k, block, to, src=None):
            return pltpu.make_async_remote_copy(
                src_ref=slot(a, block) if src is None else src, dst_ref=slot(a, block),
                send_sem=send_sems.at[a * nk + k], recv_sem=recv_sems.at[a * nk + k],
                device_id=to, device_id_type=pl.DeviceIdType.MESH)

        mine = [pltpu.make_async_copy(ins[a], slot(a, me), local_sems.at[a]) for a in range(n)]
        for cp in mine:
            cp.start()
        first = []
        for a in range(n):
            first.append(copy(a, 0, me, sibling, src=ins[a]))
            first += [copy(a, 1 + j, me, (*chip, c), src=ins[a]) for j, chip in enumerate(chips)]
        for cp in first:
            cp.start()
        passed = []
        for j, chip in enumerate(chips):
            for a in range(n):
                copy(a, 1 + j, (*chip, c), me).wait_recv()
                fwd = copy(a, 4 + j, (*chip, c), sibling)
                fwd.start()
                passed.append(fwd)
        for a in range(n):
            copy(a, 0, sibling, me).wait_recv()
            for j, chip in enumerate(chips):
                copy(a, 4 + j, (*chip, 1 - c), me).wait_recv()
        for cp in first + passed:
            cp.wait_send()
        for cp in mine:
            cp.wait()

    anyspec = pl.BlockSpec(memory_space=pl.ANY)
    return pl.pallas_call(
        body, name=name,
        in_specs=[anyspec] * n, out_specs=[anyspec] * n,
        out_shape=[jax.ShapeDtypeStruct((N_DEV,) + a.shape, a.dtype) for a in arrays],
        scratch_shapes=[pltpu.SemaphoreType.DMA((n * nk,)), pltpu.SemaphoreType.DMA((n * nk,)),
                        pltpu.SemaphoreType.DMA((n,))],
    )(*arrays)


_HBM = pl.BlockSpec(memory_space=pltpu.HBM)
_SEM = pl.BlockSpec(memory_space=pltpu.SEMAPHORE)
_ANY = pl.BlockSpec(memory_space=pl.ANY)
_EFFECT = pltpu.SideEffectType.DATAFLOW_SIDE_EFFECTING


def _mesh_peer(k):
    x, y, c = lax.axis_index("x"), lax.axis_index("y"), lax.axis_index("c")
    px, py, pc = x ^ ((k >> 2) & 1), y ^ ((k >> 1) & 1), c ^ (k & 1)
    return (px, py, pc), 4 * px + 2 * py + pc


def _exchange_start(arrays, scatter, name, after):
    n = len(arrays)
    lands = [lax.empty(a.shape if scatter else (N_DEV,) + a.shape, a.dtype) for a in arrays]

    def body(*refs):
        ins, land_refs = refs[:n], refs[n:2 * n]
        send_sems, recv_sems = refs[2 * n + 1], refs[2 * n + 2]
        token = refs[-1]
        _, me = _mesh_peer(0)
        for k in range(1, N_DEV):
            dev, pid = _mesh_peer(k)
            for a in range(n):
                pltpu.make_async_remote_copy(
                    src_ref=ins[a].at[pid] if scatter else ins[a], dst_ref=land_refs[a].at[me],
                    send_sem=send_sems.at[a * N_DEV + k], recv_sem=recv_sems.at[a * N_DEV + k],
                    device_id=dev, device_id_type=pl.DeviceIdType.MESH).start()
        token[...] = jnp.zeros_like(token)

    hbm_in = [pltpu.with_memory_space_constraint(a, pltpu.HBM) for a in list(arrays) + lands]
    outs = pl.pallas_call(
        body, name=name,
        in_specs=[_HBM] * (2 * n) + [_ANY],
        out_specs=[_SEM, _SEM] + [_HBM] * (2 * n) + [pl.BlockSpec(memory_space=pltpu.VMEM)],
        out_shape=[pltpu.SemaphoreType.DMA((n * N_DEV,)), pltpu.SemaphoreType.DMA((n * N_DEV,))]
        + [pltpu.HBM(a.shape, a.dtype) for a in hbm_in]
        + [jax.ShapeDtypeStruct((SUBLANES, LANES), F32)],
        input_output_aliases={i: 2 + i for i in range(2 * n)},
        compiler_params=pltpu.CompilerParams(has_side_effects=_EFFECT),
    )(*hbm_in, after)
    return outs[0], outs[1], outs[2:2 + n], outs[2 + n:2 + 2 * n], outs[-1]


def _exchange_wait(started, scatter, name, after):
    send_sems, recv_sems, srcs, lands, _ = started
    n = len(srcs)

    def body(*refs):
        ins, land_refs = refs[:n], refs[n:2 * n]
        send_sems, recv_sems = refs[2 * n], refs[2 * n + 1]
        copies = []
        for k in range(1, N_DEV):
            dev, pid = _mesh_peer(k)
            for a in range(n):
                copies.append(pltpu.make_async_remote_copy(
                    src_ref=ins[a].at[pid] if scatter else ins[a], dst_ref=land_refs[a].at[pid],
                    send_sem=send_sems.at[a * N_DEV + k], recv_sem=recv_sems.at[a * N_DEV + k],
                    device_id=dev, device_id_type=pl.DeviceIdType.MESH))
        for cp in copies:
            cp.wait_recv()
        for cp in copies:
            cp.wait_send()

    outs = pl.pallas_call(
        body, name=name,
        in_specs=[_HBM] * (2 * n) + [_SEM, _SEM, _ANY],
        out_specs=[_HBM] * (2 * n),
        out_shape=[pltpu.HBM(a.shape, a.dtype) for a in list(srcs) + list(lands)],
        input_output_aliases={i: i for i in range(2 * n)},
        compiler_params=pltpu.CompilerParams(has_side_effects=_EFFECT),
    )(*srcs, *lands, send_sems, recv_sems, after)
    me = 4 * lax.axis_index("x") + 2 * lax.axis_index("y") + lax.axis_index("c")
    filled = []
    for src, land in zip(outs[:n], outs[n:]):
        own = lax.dynamic_index_in_dim(src, me, 0, keepdims=True) if scatter else src[None]
        filled.append(lax.dynamic_update_slice(land, own, (me,) + (0,) * (land.ndim - 1)))
    return filled


def _adamw(parts, w, m, v, name, tr):
    nl = len(parts)
    r, c = parts[0].shape[1:]
    assert w.shape == (nl * r, c) and r % tr == 0, (name, w.shape, r, tr)
    nt = r // tr
    c1 = 1.0 - ADAM_B1 ** ADAM_STEP
    c2 = 1.0 - ADAM_B2 ** ADAM_STEP

    def body(*refs):
        p_refs = refs[:nl]
        w_ref, m_ref, v_ref, g_ref, d_ref, nm_ref, nv_ref = refs[nl:]
        layer = pl.program_id(0)
        for k in range(nl):
            @pl.when(layer == k)
            def _(p_ref=p_refs[k]):
                g = p_ref[0].astype(F32)
                for s in range(1, N_DEV):
                    g = g + p_ref[s].astype(F32)
                nm = ADAM_B1 * m_ref[...] + (1.0 - ADAM_B1) * g
                nv = ADAM_B2 * v_ref[...] + (1.0 - ADAM_B2) * (g * g)
                g_ref[...] = g
                nm_ref[...] = nm
                nv_ref[...] = nv
                d_ref[...] = -ADAM_LR * ((nm / c1) / (jnp.sqrt(nv / c2) + ADAM_EPS) + ADAM_WD * w_ref[...])

    def part_spec(k):
        return pl.BlockSpec((N_DEV, tr, c), lambda l, i: (0, jnp.where(l == k, i, 0), 0))

    row = pl.BlockSpec((tr, c), lambda l, i: (l * nt + i, 0))
    o = jax.ShapeDtypeStruct((nl * r, c), F32)
    return pl.pallas_call(
        body, name=name, grid=(nl, nt),
        in_specs=[part_spec(k) for k in range(nl)] + [row, row, row],
        out_specs=[row, row, row, row], out_shape=[o, o, o, o],
        compiler_params=_cp(("arbitrary", "arbitrary")),
    )(*parts, w, m, v)


def _with_token(gain, token):
    return gain if token is None else gain + token[0:1, 0:1]


def _layer_fwd(x, wl, sl, tabs, l, rest_fn=None, rest2_fn=None, h=None, after=None, next_gain=None):
    ctab, stab, bd = tabs
    n = f"l{l}_"
    if h is None:
        h = _rmsnorm_fwd(x, sl["norm_mix"], n + "norm_mix")
    qkv = _mm(h, wl["w_qkv"], "nn", out_dtype=F32, name=n + "proj_qkv", tm=1024, tn=QKV_W, tk=D_MODEL, after=after)
    uug = _mm(h, wl["w_uug"], "nn", out_dtype=F32, name=n + "proj_uug", tm=1024, tn=UUG_W, tk=D_MODEL, after=after)
    gg = _mm(h, wl["w_gg"], "nn", out_dtype=F32, name=n + "proj_gg", tm=1024, tn=GG_W, tk=D_MODEL, after=after)
    qs, kd, vd = _qk_prep_fwd(qkv, sl["gq"], sl["gk"], ctab, stab, bd, n + "qk_prep")
    a_out = _attn_fwd(qs, kd, vd, sl["sinks"], n + "attn")
    y0, y2 = _conv_fwd(uug, sl["conv_w32"], sl["conv_b"], sl["ln_g"], sl["ln_b"], n + "conv")
    token = None
    if rest_fn is not None:
        rest, token = rest_fn(y2)
        wl = {**wl, **rest}
    c_out, merged = _merge_fwd(y2, wl["w_conv_out"], a_out, gg, n + "merge")
    x1, h2 = _mm_rows([(merged, wl["w_out"])], "nn", name=n + "out_proj", tm=512, resid=x,
                      norm_fwd=_with_token(sl["norm_ffn"], token))
    token2 = None
    if rest2_fn is not None:
        rest2, token2 = rest2_fn(x1)
        wl = {**wl, **rest2}
    act, g_pre, u_pre = _ffn_up_fwd(h2, wl["w_gate_up"], n + "ffn_up", after=token2)
    if next_gain is None:
        x2 = _mm_rows([(act, wl["w_down"])], "nn", name=n + "ffn_down", tm=512, resid=x1)
        h_next = None
    else:
        x2, h_next = _mm_rows([(act, wl["w_down"])], "nn", name=n + "ffn_down", tm=512, resid=x1, norm_fwd=next_gain)
    saved = dict(x=x, h=h, qkv=qkv, uug=uug, gg=gg, qs=qs, kd=kd, vd=vd, a_out=a_out, y0=y0, y2=y2,
                 c_out=c_out, merged=merged, x1=x1, h2=h2, act=act, g_pre=g_pre, u_pre=u_pre)
    return x2, saved, wl, h_next


def _layer_bwd(dx2, sv, wl, sl, tabs, l, after=None, ffn_hook=None, mix_hook=None):
    ctab, stab, bd = tabs
    n = f"l{l}_b_"
    tk = 2048
    gw, gs = {}, {}
    gw["w_down"] = _mm(sv["act"], dx2, "tn", out_dtype=BF16, name=n + "dw_down", tm=FF_TN, tn=D_MODEL, tk=tk,
                       after=after)
    dgu = _ffn_bwd_mid(sv["g_pre"], sv["u_pre"], dx2, wl["w_down"], n + "ffn_mid", after=after)
    gw["w_gate_up"] = _mm(sv["h2"], dgu, "tn", out_dtype=BF16, name=n + "dw_gate_up", tm=D_MODEL, tn=FF_TN, tk=tk)
    token = None if ffn_hook is None else ffn_hook(gw)
    dx1, gs["norm_ffn"] = _mm_rows([(dgu, wl["w_gate_up"], 0, 0), (dgu, wl["w_gate_up"], 1, 1)], "nt", name=n + "dh2", tm=512,
                                   norm_bwd=(sv["x1"], _with_token(sl["norm_ffn"], token), dx2))
    gw["w_out"] = _mm(sv["merged"], dx1, "tn", out_dtype=BF16, name=n + "dw_out", tm=D_MODEL, tn=D_MODEL, tk=tk)
    da_out, dc_out, dgg = _merge_bwd(dx1, wl["w_out"], sv["a_out"], sv["c_out"], sv["gg"], n + "merge")
    gw["w_conv_out"] = _mm(sv["y2"], dc_out, "tn", out_dtype=BF16, name=n + "dw_conv_out", tm=CONV_CH, tn=D_MODEL,
                           tk=tk)
    dy0, gs["ln_g"], gs["ln_b"], gs["conv_b"] = _conv_bwd_ln(dc_out, wl["w_conv_out"], sv["y0"], sl["ln_g"],
                                                                 sl["ln_b"], n + "conv_ln")
    duug, gs["conv_w"] = _conv_bwd_taps(dy0, sv["uug"], sl["conv_w32"], n + "conv_taps")
    dqs, dkd, dvd, gs["sinks"] = _attn_bwd(sv["qs"], sv["kd"], sv["vd"], sl["sinks"], da_out, n + "attn")
    dqkv, gs["gq"], gs["gk"] = _qk_prep_bwd(dqs, dkd, dvd, sv["qkv"], sl["gq"], sl["gk"], ctab, stab, bd,
                                            n + "qk_prep")
    gw["w_qkv"] = _mm(sv["h"], dqkv, "tn", out_dtype=BF16, name=n + "dw_qkv", tm=D_MODEL, tn=QKV_W, tk=tk)
    gw["w_uug"] = _mm(sv["h"], duug, "tn", out_dtype=BF16, name=n + "dw_uug", tm=D_MODEL, tn=UUG_W, tk=tk)
    gw["w_gg"] = _mm(sv["h"], dgg, "tn", out_dtype=BF16, name=n + "dw_gg", tm=D_MODEL, tn=GG_W, tk=tk)
    token_mix = None if mix_hook is None else mix_hook(gw)
    dx, gs["norm_mix"] = _mm_rows([(dqkv, wl["w_qkv"]), (duug, wl["w_uug"]), (dgg, wl["w_gg"])], "nt",
                                  name=n + "dh", tm=512,
                                  norm_bwd=(sv["x"], _with_token(sl["norm_mix"], token_mix), dx1))
    return dx, gw, gs


def _cols_to_full(g):
    n, l, r, c = g.shape
    return jnp.transpose(g, (1, 2, 0, 3)).reshape(l, r, n * c)


def _rows_to_full(g):
    n, l, r, c = g.shape
    return jnp.transpose(g, (1, 0, 2, 3)).reshape(l, n * r, c)


def _full_to_cols(w):
    l, r, c = w.shape
    return jnp.transpose(w.reshape(l, r, N_DEV, c // N_DEV), (2, 0, 1, 3))


def _full_to_rows(w):
    l, r, c = w.shape
    return jnp.transpose(w.reshape(l, N_DEV, r // N_DEV, c), (1, 0, 2, 3))


SMALL = (("norm_mix", D_MODEL), ("q_norm", HEAD_DIM), ("k_norm", HEAD_DIM), ("sinks", N_HEADS),
         ("conv_w", CONV_WIDTH * CONV_CH), ("conv_b", CONV_CH), ("conv_ln_g", CONV_CH), ("conv_ln_b", CONV_CH),
         ("norm_ffn", D_MODEL))
SMALL_TOTAL = DEPTH * sum(s for _, s in SMALL)
SMALL_ROWS = -(-SMALL_TOTAL // (LANES * SUBLANES)) * SUBLANES


def _pack_small(d):
    flat = jnp.concatenate([d[k].reshape(-1).astype(F32) for k, _ in SMALL])
    flat = jnp.pad(flat, (0, SMALL_ROWS * LANES - SMALL_TOTAL))
    return flat.reshape(SMALL_ROWS, LANES)


def _unpack_small(buf, shapes):
    flat = buf.reshape(-1)
    out, o = {}, 0
    for k, s in SMALL:
        out[k] = flat[o:o + DEPTH * s].reshape(shapes[k])
        o += DEPTH * s
    return out


def kernel(x, norm_mix, w_in, q_norm, k_norm, sinks, conv_w, conv_b, conv_ln_g, conv_ln_b, w_conv_out, w_out, norm_ffn, w_gate_up, w_down, loss_target, m_norm_mix, m_w_in, m_q_norm, m_k_norm, m_sinks, m_conv_w, m_conv_b, m_conv_ln_g, m_conv_ln_b, m_w_conv_out, m_w_out, m_norm_ffn, m_w_gate_up, m_w_down, v_norm_mix, v_w_in, v_q_norm, v_k_norm, v_sinks, v_conv_w, v_conv_b, v_conv_ln_g, v_conv_ln_b, v_w_conv_out, v_w_out, v_norm_ffn, v_w_gate_up, v_w_down):
    t = x.shape[1]
    me = 4 * lax.axis_index("x") + 2 * lax.axis_index("y") + lax.axis_index("c")
    xs = x.reshape(t, D_MODEL)
    target = loss_target.reshape(t, D_MODEL)

    def shards_in(l):
        return [w_in[l].astype(BF16)]

    def shards_rest(l):
        return [w_conv_out[l].astype(BF16), w_out[l].astype(BF16), w_gate_up[l].astype(BF16), w_down[l].astype(BF16)]

    def weights_in(g_in):
        f_in = _cols_to_full(g_in[:, None])[0]
        return dict(w_qkv=f_in[:, :QKV_W], w_uug=f_in[:, QKV_W:QKV_W + UUG_W], w_gg=f_in[:, QKV_W + UUG_W:])

    def weights_merge(g):
        g_co, g_out = g
        return dict(w_conv_out=_cols_to_full(g_co[:, None])[0], w_out=_rows_to_full(g_out[:, None])[0])

    def weights_ffn(g):
        g_gu, g_dn = g
        f_gu = _cols_to_full(g_gu[:, None])[0]
        return dict(w_gate_up=f_gu, w_down=_rows_to_full(g_dn[:, None])[0])

    def weights_rest(g):
        return {**weights_merge(g[:2]), **weights_ffn(g[2:])}

    g_in0, g_cw = _gather_two_level(shards_in(0) + [conv_w], name="gather_w_in_0")
    f_cw = _cols_to_full(g_cw)
    tabs = _rope_tables(t) + (_block_diag_ones(),)

    def layer_small(l, token):
        return dict(norm_mix=_with_token(norm_mix[l][None], token), norm_ffn=norm_ffn[l][None],
                    gq=jnp.tile(q_norm[l], 2)[None], gk=jnp.tile(k_norm[l], 2)[None], sinks=sinks[l],
                    conv_w32=jnp.pad(f_cw[l], ((0, HALO - CONV_WIDTH), (0, 0))),
                    conv_b=conv_b[l][None], ln_g=conv_ln_g[l][None], ln_b=conv_ln_b[l][None])

    wls, sls, saved = [], [], []
    cur = xs
    flight = {"next": None}

    def start_next(l, after):
        flight["next"] = _exchange_start(shards_in(l + 1) + shards_rest(l + 1), False, f"gather_start_{l + 1}",
                                         after=after)
        return flight["next"][4]

    merge0 = _exchange_start(shards_rest(0)[:2], False, "gather_start_merge_0", after=g_in0)
    ffn0 = _exchange_start(shards_rest(0)[2:], False, "gather_start_ffn_0", after=merge0[4])

    def rest_fn0(after):
        g = _exchange_wait(merge0, False, "gather_wait_merge_0", after=after)
        return weights_merge(g), start_next(0, g[0])

    def rest2_fn0(after):
        return weights_ffn(_exchange_wait(ffn0, False, "gather_wait_ffn_0", after=after)), None

    gathered, h_next = None, None
    for l in range(DEPTH):
        if l == 0:
            w_first, token, rest_fn, rest2_fn = weights_in(g_in0), ffn0[4], rest_fn0, rest2_fn0
        else:
            w_first = {**weights_in(gathered[0]), **weights_rest(gathered[1:])}
            token = start_next(l, gathered[0]) if l + 1 < DEPTH else None
            rest_fn, rest2_fn = None, None
        sls.append(layer_small(l, token if l == 0 else None))
        cur, sv, wl, h_next = _layer_fwd(cur, w_first, sls[l], tabs, l, rest_fn=rest_fn, rest2_fn=rest2_fn, h=h_next,
                                         after=None if l == 0 else token,
                                         next_gain=norm_mix[l + 1][None] if l + 1 < DEPTH else None)
        wls.append(wl)
        saved.append(sv)
        if l + 1 < DEPTH:
            gathered = _exchange_wait(flight["next"], False, f"gather_wait_{l + 1}", after=cur)
    dy, loss_part = _loss_head(cur, target, "loss_head")
    loss = lax.psum(jnp.sum(loss_part), ("x", "y", "c"))

    def slabs_ffn(gw):
        d_gu = gw["w_gate_up"][None]
        return [_full_to_cols(d_gu)[:, 0].astype(BF16), _full_to_rows(gw["w_down"][None])[:, 0].astype(BF16)]

    def slabs_mix(gw):
        d_in = jnp.concatenate([gw["w_qkv"], gw["w_uug"], gw["w_gg"]], axis=1)[None]
        return [_full_to_cols(d_in)[:, 0].astype(BF16), _full_to_cols(gw["w_conv_out"][None])[:, 0].astype(BF16),
                _full_to_rows(gw["w_out"][None])[:, 0].astype(BF16)]

    gss = [None] * DEPTH
    parts_ffn, parts_mix = [None] * DEPTH, [None] * DEPTH
    dcur = dy
    state = {"mix": None, "ffn": None}

    def make_ffn_hook(l):
        def hook(gw):
            sends = slabs_ffn(gw)
            after = sends[0]
            if state["mix"] is not None:
                parts_mix[l + 1] = _exchange_wait(state["mix"], True, f"scatter_wait_mix_{l + 1}", after=sends[0])
                after = parts_mix[l + 1][0]
            state["ffn"] = _exchange_start(sends, True, f"scatter_start_ffn_{l}", after=after)
            return state["ffn"][4]
        return hook

    def last_mix_hook(gw):
        sends = slabs_mix(gw)
        parts_ffn[0] = _exchange_wait(state["ffn"], True, "scatter_wait_ffn_0", after=sends[0])
        state["mix"] = _exchange_start(sends, True, "scatter_start_mix_0", after=parts_ffn[0][0])
        return state["mix"][4]

    for l in reversed(range(DEPTH)):
        dcur, gw, gss[l] = _layer_bwd(dcur, saved[l], wls[l], sls[l], tabs, l,
                                      after=None if state["mix"] is None else state["mix"][4],
                                      ffn_hook=make_ffn_hook(l), mix_hook=last_mix_hook if l == 0 else None)
        if l > 0:
            parts_ffn[l] = _exchange_wait(state["ffn"], True, f"scatter_wait_ffn_{l}", after=dcur)
            state["mix"] = _exchange_start(slabs_mix(gw), True, f"scatter_start_mix_{l}", after=parts_ffn[l][0])
        else:
            parts_mix[0] = _exchange_wait(state["mix"], True, "scatter_wait_mix_0", after=dcur)
    grad_x = dcur.reshape(x.shape)
    parts = [[parts_mix[l][0] for l in range(DEPTH)], [parts_mix[l][1] for l in range(DEPTH)],
             [parts_mix[l][2] for l in range(DEPTH)], [parts_ffn[l][0] for l in range(DEPTH)],
             [parts_ffn[l][1] for l in range(DEPTH)]]

    def update(p, w, m, v, name, tr):
        shp = w.shape
        r = shp[0] * shp[1]
        flat = lambda a: a.reshape(r, shp[2])
        outs = _adamw(p, flat(w), flat(m), flat(v), name, tr)
        return [o.reshape(shp) for o in outs]

    u_in = update(parts[0], w_in, m_w_in, v_w_in, "adamw_w_in", 256)
    u_co = update(parts[1], w_conv_out, m_w_conv_out, v_w_conv_out, "adamw_w_conv_out", 512)
    u_out = update(parts[2], w_out, m_w_out, v_w_out, "adamw_w_out", 128)
    u_gu = update(parts[3], w_gate_up, m_w_gate_up, v_w_gate_up, "adamw_w_gate_up", 256)
    u_dn = update(parts[4], w_down, m_w_down, v_w_down, "adamw_w_down", 176)

    def fold_rows(a):
        return jnp.sum(a, axis=0)

    def fold_heads(a):
        return jnp.sum(a, axis=0).reshape(2, HEAD_DIM).sum(axis=0)

    small_g = {
        "norm_mix": jnp.stack([fold_rows(gss[l]["norm_mix"]) for l in range(DEPTH)]),
        "q_norm": jnp.stack([fold_heads(gss[l]["gq"]) for l in range(DEPTH)]),
        "k_norm": jnp.stack([fold_heads(gss[l]["gk"]) for l in range(DEPTH)]),
        "sinks": jnp.stack([gss[l]["sinks"][0, :N_HEADS] for l in range(DEPTH)]),
        "conv_w": jnp.stack([gss[l]["conv_w"].reshape(CONV_WIDTH, SUBLANES, CONV_CH).sum(axis=1)
                             for l in range(DEPTH)]),
        "conv_b": jnp.stack([fold_rows(gss[l]["conv_b"]) for l in range(DEPTH)]),
        "conv_ln_g": jnp.stack([fold_rows(gss[l]["ln_g"]) for l in range(DEPTH)]),
        "conv_ln_b": jnp.stack([fold_rows(gss[l]["ln_b"]) for l in range(DEPTH)]),
        "norm_ffn": jnp.stack([fold_rows(gss[l]["norm_ffn"]) for l in range(DEPTH)]),
    }
    (small_parts,) = _exchange([_pack_small(small_g)], scatter=False, name="gather_small_grads")
    shapes = {"norm_mix": norm_mix.shape, "q_norm": q_norm.shape, "k_norm": k_norm.shape, "sinks": sinks.shape,
              "conv_w": (DEPTH, CONV_WIDTH, CONV_CH), "conv_b": conv_b.shape, "conv_ln_g": conv_ln_g.shape,
              "conv_ln_b": conv_ln_b.shape, "norm_ffn": norm_ffn.shape}

    def widen(a):
        z = jnp.zeros((DEPTH, CONV_WIDTH, N_DEV, CONV_CH // N_DEV), F32)
        z = lax.dynamic_update_slice(z, a[:, :, None, :], (0, 0, me, 0))
        return z.reshape(DEPTH, CONV_WIDTH, CONV_CH)

    sw = _pack_small(dict(norm_mix=norm_mix, q_norm=q_norm, k_norm=k_norm, sinks=sinks, conv_w=widen(conv_w),
                          conv_b=conv_b, conv_ln_g=conv_ln_g, conv_ln_b=conv_ln_b, norm_ffn=norm_ffn))
    sm = _pack_small(dict(norm_mix=m_norm_mix, q_norm=m_q_norm, k_norm=m_k_norm, sinks=m_sinks,
                          conv_w=widen(m_conv_w), conv_b=m_conv_b, conv_ln_g=m_conv_ln_g, conv_ln_b=m_conv_ln_b,
                          norm_ffn=m_norm_ffn))
    sv_ = _pack_small(dict(norm_mix=v_norm_mix, q_norm=v_q_norm, k_norm=v_k_norm, sinks=v_sinks,
                           conv_w=widen(v_conv_w), conv_b=v_conv_b,
                           conv_ln_g=v_conv_ln_g, conv_ln_b=v_conv_ln_b, norm_ffn=v_norm_ffn))
    s_outs = [_unpack_small(o, shapes) for o in _adamw([small_parts], sw, sm, sv_, "adamw_small", SMALL_ROWS)]

    def narrow(a):
        a4 = a.reshape(DEPTH, CONV_WIDTH, N_DEV, CONV_CH // N_DEV)
        return lax.dynamic_slice(a4, (0, 0, me, 0), (DEPTH, CONV_WIDTH, 1, CONV_CH // N_DEV)).reshape(
            DEPTH, CONV_WIDTH, CONV_CH // N_DEV)

    big = {"w_in": u_in, "w_conv_out": u_co, "w_out": u_out, "w_gate_up": u_gu, "w_down": u_dn}
    order = ["norm_mix", "w_in", "q_norm", "k_norm", "sinks", "conv_w", "conv_b", "conv_ln_g", "conv_ln_b",
             "w_conv_out", "w_out", "norm_ffn", "w_gate_up", "w_down"]
    outs = [loss, grad_x]
    for kind in range(4):
        for name in order:
            if name in big:
                outs.append(big[name][kind])
            elif name == "conv_w":
                outs.append(narrow(s_outs[kind][name]))
            else:
                outs.append(s_outs[kind][name])
    return tuple(outs)
```

```python
import functools
import math

import jax
import jax.numpy as jnp
from jax import lax
from jax.experimental import pallas as pl
from jax.experimental.pallas import tpu as pltpu

F32 = jnp.float32
BF16 = jnp.bfloat16

D_MODEL = 1024
DEPTH = 4
N_HEADS = 16
N_KV_HEADS = 2
HEAD_DIM = 64
ROT_DIM = HEAD_DIM // 4
ROPE_THETA = 500000.0
BLOCK = 128
CONV_CH = D_MODEL // 2
CONV_WIDTH = 31
D_FF = 2816
EPS = 1e-6
Q_W = N_HEADS * HEAD_DIM
KV_W = N_KV_HEADS * HEAD_DIM
QKV_W = Q_W + 2 * KV_W
UUG_W = 2 * CONV_CH
GG_W = 2 * D_MODEL
IN_W = QKV_W + UUG_W + GG_W
N_DEV = 8

ADAM_LR = 0.001
ADAM_B1 = 0.9
ADAM_B2 = 0.999
ADAM_EPS = 1e-08
ADAM_WD = 0.01
ADAM_STEP = 10

LANES = 128
SUBLANES = 8
HALO = 32
VMEM_LIMIT = 56 * 1024 * 1024
NEG = -1e30


def _cp(sem=None):
    return pltpu.CompilerParams(dimension_semantics=sem, vmem_limit_bytes=VMEM_LIMIT)


def _sigmoid(z):
    return 1.0 / (1.0 + jnp.exp(-z))


def _rowgroup_sum(z):
    r, c = z.shape
    return jnp.sum(z.reshape(r // SUBLANES, SUBLANES, c), axis=0)


_DIMS = {"nn": (((1,), (0,)), ((), ())), "nt": (((1,), (1,)), ((), ())), "tn": (((0,), (0,)), ((), ()))}


def _mm(a, b, mode, *, out_dtype, name, tm, tn, tk, resid=None, after=None):
    if mode == "nn":
        (m, k), (k2, n) = a.shape, b.shape
        a_spec = pl.BlockSpec((tm, tk), lambda i, j, s: (i, s))
        b_spec = pl.BlockSpec((tk, tn), lambda i, j, s: (s, j))
    elif mode == "nt":
        (m, k), (n, k2) = a.shape, b.shape
        a_spec = pl.BlockSpec((tm, tk), lambda i, j, s: (i, s))
        b_spec = pl.BlockSpec((tn, tk), lambda i, j, s: (j, s))
    elif b.ndim == 3:
        (k, m), (parts, k2, n_part) = a.shape, b.shape
        n, per = parts * n_part, n_part // tn
        a_spec = pl.BlockSpec((tk, tm), lambda i, j, s: (s, i))
        b_spec = pl.BlockSpec((None, tk, tn), lambda i, j, s: (j // per, s, j % per))
    else:
        (k, m), (k2, n) = a.shape, b.shape
        a_spec = pl.BlockSpec((tk, tm), lambda i, j, s: (s, i))
        b_spec = pl.BlockSpec((tk, tn), lambda i, j, s: (s, j))
    assert k == k2 and m % tm == 0 and n % tn == 0 and k % tk == 0, (name, a.shape, b.shape, tm, tn, tk)
    nk = k // tk
    dims = _DIMS[mode]
    has_resid = resid is not None

    def body(*refs):
        a_ref, b_ref = refs[0], refs[1]
        r_ref = refs[2] if has_resid else None
        o_ref = refs[-1] if nk == 1 else refs[-2]
        part = lax.dot_general(a_ref[...].astype(BF16), b_ref[...].astype(BF16), dims, preferred_element_type=F32)

        def finish(acc):
            if has_resid:
                acc = acc + r_ref[...]
            o_ref[...] = acc.astype(out_dtype)

        if nk == 1:
            finish(part)
            return
        acc_ref = refs[-1]
        s = pl.program_id(2)

        @pl.when(s == 0)
        def _():
            acc_ref[...] = part

        @pl.when(s > 0)
        def _():
            acc_ref[...] += part

        @pl.when(s == nk - 1)
        def _():
            finish(acc_ref[...])

    in_specs = [a_spec, b_spec]
    args = [a, b]
    if has_resid:
        in_specs.append(pl.BlockSpec((tm, tn), lambda i, j, s: (i, j)))
        args.append(resid)
    if after is not None:
        in_specs.append(_ANY)
        args.append(after)
    return pl.pallas_call(
        body, name=name, grid=(m // tm, n // tn, nk),
        in_specs=in_specs, out_specs=pl.BlockSpec((tm, tn), lambda i, j, s: (i, j)),
        out_shape=jax.ShapeDtypeStruct((m, n), out_dtype),
        scratch_shapes=[] if nk == 1 else [pltpu.VMEM((tm, tn), F32)],
        compiler_params=_cp(("parallel", "parallel", "arbitrary")),
    )(*args)


def _mm_rows(pairs, mode, *, name, tm, resid=None, norm_fwd=None, norm_bwd=None, after=None, bf16_copy=False):
    pairs = [tuple(p) for p in pairs]
    m = pairs[0][0].shape[-2]
    n = pairs[0][1].shape[1] if mode == "nn" else pairs[0][1].shape[0]
    assert m % tm == 0 and not (norm_fwd is not None and norm_bwd is not None), name
    dims = _DIMS[mode]
    np_ = len(pairs)
    row = pl.BlockSpec((tm, n), lambda i: (i, 0))
    vec = pl.BlockSpec((1, n), lambda i: (0, 0))
    in_specs, args = [], []
    for pair in pairs:
        a, b = pair[0], pair[1]
        k = a.shape[-1]
        assert a.shape[-2] == m, (name, a.shape)
        if len(pair) >= 3:
            assert mode == "nt" and b.shape[0] == n and b.shape[1] % k == 0, (name, a.shape, b.shape)
            b_spec = pl.BlockSpec((n, k), functools.partial(lambda i, j: (0, j), j=pair[2]))
        else:
            assert b.shape == (k, n) if mode == "nn" else b.shape == (n, k), (name, a.shape, b.shape)
            b_spec = pl.BlockSpec(b.shape, lambda i: (0, 0))
        if len(pair) == 4:
            a_spec = pl.BlockSpec((None, tm, k), functools.partial(lambda i, p: (p, i, 0), p=pair[3]))
        else:
            a_spec = pl.BlockSpec((tm, k), lambda i: (i, 0))
        in_specs += [a_spec, b_spec]
        args += [a, b]
    if resid is not None:
        in_specs.append(row)
        args.append(resid)
    if norm_fwd is not None:
        in_specs.append(vec)
        args.append(norm_fwd)
    if norm_bwd is not None:
        in_specs += [row, vec, row]
        args += list(norm_bwd)
    if after is not None:
        in_specs.append(_ANY)
        args.append(after)
    n_out = (1 if (norm_fwd is None and norm_bwd is None) else 2) + (1 if bf16_copy else 0)

    def body(*refs):
        outs = refs[len(refs) - n_out:]
        pos = 2 * np_
        acc = None
        for p in range(np_):
            part = lax.dot_general(refs[2 * p][...].astype(BF16), refs[2 * p + 1][...].astype(BF16), dims,
                                   preferred_element_type=F32)
            acc = part if acc is None else acc + part
        if resid is not None:
            acc = acc + refs[pos][...]
            pos += 1
        if norm_fwd is not None:
            r = lax.rsqrt(jnp.mean(acc * acc, axis=-1, keepdims=True) + EPS)
            outs[1][...] = ((acc * r) * refs[pos][...]).astype(BF16)
        if norm_bwd is not None:
            @pl.when(pl.program_id(0) == 0)
            def _():
                outs[1][...] = jnp.zeros_like(outs[1])

            xv = refs[pos][...]
            r = lax.rsqrt(jnp.mean(xv * xv, axis=-1, keepdims=True) + EPS)
            y = xv * r
            outs[1][...] += _rowgroup_sum(acc * y)
            dy = acc * refs[pos + 1][...]
            acc = refs[pos + 2][...] + r * (dy - y * jnp.mean(dy * y, axis=-1, keepdims=True))
        outs[0][...] = acc
        if bf16_copy:
            outs[-1][...] = acc.astype(BF16)

    out_specs = [row]
    out_shape = [jax.ShapeDtypeStruct((m, n), F32)]
    if norm_fwd is not None:
        out_specs.append(row)
        out_shape.append(jax.ShapeDtypeStruct((m, n), BF16))
    if norm_bwd is not None:
        out_specs.append(pl.BlockSpec((SUBLANES, n), lambda i: (0, 0)))
        out_shape.append(jax.ShapeDtypeStruct((SUBLANES, n), F32))
    if bf16_copy:
        out_specs.append(row)
        out_shape.append(jax.ShapeDtypeStruct((m, n), BF16))
    res = pl.pallas_call(
        body, name=name, grid=(m // tm,), in_specs=in_specs, out_specs=out_specs, out_shape=out_shape,
        compiler_params=_cp(("arbitrary",) if norm_bwd is not None else ("parallel",)),
    )(*args)
    return res[0] if n_out == 1 else res


def _rmsnorm_fwd(x, g, name, tm=512):
    t, d = x.shape

    def body(x_ref, g_ref, h_ref):
        xv = x_ref[...]
        r = lax.rsqrt(jnp.mean(xv * xv, axis=-1, keepdims=True) + EPS)
        h_ref[...] = ((xv * r) * g_ref[...]).astype(BF16)

    return pl.pallas_call(
        body, name=name, grid=(t // tm,),
        in_specs=[pl.BlockSpec((tm, d), lambda i: (i, 0)), pl.BlockSpec((1, d), lambda i: (0, 0))],
        out_specs=pl.BlockSpec((tm, d), lambda i: (i, 0)),
        out_shape=jax.ShapeDtypeStruct((t, d), BF16),
        compiler_params=_cp(("parallel",)),
    )(x, g)


def _rmsnorm_bwd(dh, x, g, resid, name, tm=512):
    t, d = x.shape

    def body(dh_ref, x_ref, g_ref, r_ref, dx_ref, dg_ref):
        @pl.when(pl.program_id(0) == 0)
        def _():
            dg_ref[...] = jnp.zeros_like(dg_ref)

        xv = x_ref[...]
        dhv = dh_ref[...]
        r = lax.rsqrt(jnp.mean(xv * xv, axis=-1, keepdims=True) + EPS)
        y = xv * r
        dg_ref[...] += _rowgroup_sum(dhv * y)
        dy = dhv * g_ref[...]
        dx_ref[...] = r_ref[...] + r * (dy - y * jnp.mean(dy * y, axis=-1, keepdims=True))

    row = pl.BlockSpec((tm, d), lambda i: (i, 0))
    return pl.pallas_call(
        body, name=name, grid=(t // tm,),
        in_specs=[row, row, pl.BlockSpec((1, d), lambda i: (0, 0)), row],
        out_specs=[row, pl.BlockSpec((SUBLANES, d), lambda i: (0, 0))],
        out_shape=[jax.ShapeDtypeStruct((t, d), F32), jax.ShapeDtypeStruct((SUBLANES, d), F32)],
        compiler_params=_cp(("arbitrary",)),
    )(dh, x, g, resid)


def _seg_sum(z, bd):
    hi = z.astype(BF16)
    lo = (z - hi.astype(F32)).astype(BF16)
    return jnp.dot(hi, bd, preferred_element_type=F32) + jnp.dot(lo, bd, preferred_element_type=F32)


def _partner(z, lane64):
    return jnp.where(lane64 < ROT_DIM // 2, pltpu.roll(z, LANES - ROT_DIM // 2, 1), pltpu.roll(z, ROT_DIM // 2, 1))


def _rope_tables(t):
    inv_freq = ROPE_THETA ** (-jnp.arange(0, ROT_DIM, 2, dtype=F32) / ROT_DIM)
    ang = jnp.arange(t, dtype=F32)[:, None] * inv_freq[None, :]
    cos, sin = jnp.cos(ang), jnp.sin(ang)
    c64 = jnp.concatenate([cos, cos, jnp.ones((t, HEAD_DIM - ROT_DIM), F32)], axis=1)
    s64 = jnp.concatenate([-sin, sin, jnp.zeros((t, HEAD_DIM - ROT_DIM), F32)], axis=1)
    return jnp.tile(c64, (1, 2)), jnp.tile(s64, (1, 2))


def _block_diag_ones():
    r = lax.broadcasted_iota(jnp.int32, (LANES, LANES), 0) // HEAD_DIM
    c = lax.broadcasted_iota(jnp.int32, (LANES, LANES), 1) // HEAD_DIM
    return (r == c).astype(BF16)


def _qk_prep_fwd(qkv, gq, gk, ctab, stab, bd, name, tm=512):
    t = qkv.shape[0]
    scale = HEAD_DIM ** -0.5
    n_qg = Q_W // LANES

    def body(q_ref, kv_ref, gq_ref, gk_ref, c_ref, s_ref, bd_ref, qs_ref, kd_ref, vd_ref):
        lane = lax.broadcasted_iota(jnp.int32, (tm, LANES), 1)
        lane64 = lane % HEAD_DIM
        lo_half = lane < HEAD_DIM
        cv, sv, bdv = c_ref[...], s_ref[...], bd_ref[...]

        def norm_rope(xg, g):
            r = lax.rsqrt(_seg_sum(xg * xg, bdv) * (1.0 / HEAD_DIM) + EPS)
            yn = (xg * r) * g
            return yn * cv + _partner(yn, lane64) * sv

        zero = jnp.zeros((BLOCK, LANES), BF16)
        lo_blk = lo_half[0:BLOCK]
        for c in range(n_qg):
            xg = q_ref[:, c * LANES:(c + 1) * LANES]
            qn = (norm_rope(xg, gq_ref[...]) * scale).astype(BF16)
            for b in range(tm // BLOCK):
                rows = qn[b * BLOCK:(b + 1) * BLOCK]
                qs_ref[b, 2 * c] = jnp.where(lo_blk, rows, zero)
                qs_ref[b, 2 * c + 1] = jnp.where(lo_blk, zero, rows)
        kk = norm_rope(kv_ref[:, 0:LANES], gk_ref[...])
        kr = pltpu.roll(kk, HEAD_DIM, 1)
        kd_ref[:, 0:LANES] = jnp.where(lo_half, kk, kr).astype(BF16)
        kd_ref[:, LANES:2 * LANES] = jnp.where(lo_half, kr, kk).astype(BF16)
        vv = kv_ref[:, LANES:2 * LANES]
        vr = pltpu.roll(vv, HEAD_DIM, 1)
        vd_ref[:, 0:LANES] = jnp.where(lo_half, vv, vr).astype(BF16)
        vd_ref[:, LANES:2 * LANES] = jnp.where(lo_half, vr, vv).astype(BF16)

    vec = pl.BlockSpec((1, LANES), lambda i: (0, 0))
    tab = pl.BlockSpec((tm, LANES), lambda i: (i, 0))
    return pl.pallas_call(
        body, name=name, grid=(t // tm,),
        in_specs=[pl.BlockSpec((tm, Q_W), lambda i: (i, 0)),
                  pl.BlockSpec((tm, 2 * KV_W), lambda i: (i, Q_W // (2 * KV_W))),
                  vec, vec, tab, tab, pl.BlockSpec((LANES, LANES), lambda i: (0, 0))],
        out_specs=[pl.BlockSpec((tm // BLOCK, N_HEADS, BLOCK, LANES), lambda i: (i, 0, 0, 0)),
                   pl.BlockSpec((tm, 2 * LANES), lambda i: (i, 0)),
                   pl.BlockSpec((tm, 2 * LANES), lambda i: (i, 0))],
        out_shape=[jax.ShapeDtypeStruct((t // BLOCK, N_HEADS, BLOCK, LANES), BF16),
                   jax.ShapeDtypeStruct((t, 2 * LANES), BF16), jax.ShapeDtypeStruct((t, 2 * LANES), BF16)],
        compiler_params=_cp(("parallel",)),
    )(qkv, qkv, gq, gk, ctab, stab, bd)


def _qk_prep_bwd(dqs, dkd, dvd, qkv, gq, gk, ctab, stab, bd, name, tm=512):
    t = qkv.shape[0]
    scale = HEAD_DIM ** -0.5
    n_qg = Q_W // LANES

    def body(dqs_ref, dkd_ref, dvd_ref, q_ref, kv_ref, gq_ref, gk_ref, c_ref, s_ref, bd_ref,
             dqkv_ref, dgq_ref, dgk_ref):
        @pl.when(pl.program_id(0) == 0)
        def _():
            dgq_ref[...] = jnp.zeros_like(dgq_ref)
            dgk_ref[...] = jnp.zeros_like(dgk_ref)

        lane = lax.broadcasted_iota(jnp.int32, (tm, LANES), 1)
        lane64 = lane % HEAD_DIM
        lo_half = lane < HEAD_DIM
        cv, sv, bdv = c_ref[...], s_ref[...], bd_ref[...]

        def bwd(xg, g, dout):
            r = lax.rsqrt(_seg_sum(xg * xg, bdv) * (1.0 / HEAD_DIM) + EPS)
            y = xg * r
            dyn = dout * cv + jnp.where(lane64 < ROT_DIM, _partner(dout * sv, lane64), 0.0)
            dy = dyn * g
            dx = r * (dy - y * (_seg_sum(dy * y, bdv) * (1.0 / HEAD_DIM)))
            return dx, _rowgroup_sum(dyn * y)

        dgq = jnp.zeros((SUBLANES, LANES), F32)
        for c in range(n_qg):
            sl = slice(c * LANES, (c + 1) * LANES)
            dx, dg = bwd(q_ref[:, sl], gq_ref[...], dqs_ref[:, sl] * scale)
            dqkv_ref[:, sl] = dx.astype(BF16)
            dgq = dgq + dg
        dgq_ref[...] += dgq
        dk = jnp.where(lo_half, dkd_ref[:, 0:LANES], dkd_ref[:, LANES:2 * LANES])
        dx, dg = bwd(kv_ref[:, 0:LANES], gk_ref[...], dk)
        dqkv_ref[:, Q_W:Q_W + LANES] = dx.astype(BF16)
        dgk_ref[...] += dg
        dv = jnp.where(lo_half, dvd_ref[:, 0:LANES], dvd_ref[:, LANES:2 * LANES])
        dqkv_ref[:, Q_W + LANES:Q_W + 2 * LANES] = dv.astype(BF16)

    vec = pl.BlockSpec((1, LANES), lambda i: (0, 0))
    tab = pl.BlockSpec((tm, LANES), lambda i: (i, 0))
    wide = pl.BlockSpec((tm, 2 * LANES), lambda i: (i, 0))
    acc = pl.BlockSpec((SUBLANES, LANES), lambda i: (0, 0))
    return pl.pallas_call(
        body, name=name, grid=(t // tm,),
        in_specs=[pl.BlockSpec((tm, Q_W), lambda i: (i, 0)), wide, wide,
                  pl.BlockSpec((tm, Q_W), lambda i: (i, 0)),
                  pl.BlockSpec((tm, 2 * KV_W), lambda i: (i, Q_W // (2 * KV_W))),
                  vec, vec, tab, tab, pl.BlockSpec((LANES, LANES), lambda i: (0, 0))],
        out_specs=[pl.BlockSpec((tm, QKV_W), lambda i: (i, 0)), acc, acc],
        out_shape=[jax.ShapeDtypeStruct((t, QKV_W), BF16), jax.ShapeDtypeStruct((SUBLANES, LANES), F32),
                   jax.ShapeDtypeStruct((SUBLANES, LANES), F32)],
        compiler_params=_cp(("arbitrary",)),
    )(dqs, dkd, dvd, qkv, qkv, gq, gk, ctab, stab, bd)


GROUP = N_HEADS // N_KV_HEADS
GROUP_ROWS = GROUP * BLOCK


def _attn_masks():
    row = lax.broadcasted_iota(jnp.int32, (BLOCK, BLOCK), 0)
    col = lax.broadcasted_iota(jnp.int32, (BLOCK, BLOCK), 1)
    return col <= row, col < HEAD_DIM


def _window_softmax(s_c, s_p, is_cur, has_prev, sink):
    s = jnp.where(is_cur, s_c, jnp.where(has_prev, s_p, NEG))
    m = jnp.maximum(jnp.max(s, axis=-1, keepdims=True), sink)
    e = jnp.exp(s - m)
    e_s = jnp.exp(sink - m)
    inv = 1.0 / (jnp.sum(e, axis=-1, keepdims=True) + e_s)
    return e * inv, e_s * inv


def _attn_fwd(qs, kd, vd, sinks, name):
    nb = qs.shape[0]
    t = nb * BLOCK

    def body(sink_ref, q_ref, kc_ref, kp_ref, vc_ref, vp_ref, o_ref, pc_scr, pp_scr):
        has_prev = pl.program_id(0) > 0
        is_cur, lo_half = _attn_masks()
        for j in range(N_KV_HEADS):
            ks = slice(j * LANES, (j + 1) * LANES)
            qg = q_ref[0, j * GROUP:(j + 1) * GROUP].reshape(GROUP_ROWS, LANES)
            s_c = lax.dot_general(qg, kc_ref[:, ks], _DIMS["nt"], preferred_element_type=F32)
            s_p = lax.dot_general(qg, kp_ref[:, ks], _DIMS["nt"], preferred_element_type=F32)
            for g in range(GROUP):
                rs = slice(g * BLOCK, (g + 1) * BLOCK)
                p, _ = _window_softmax(s_c[rs], s_p[rs], is_cur, has_prev, sink_ref[j * GROUP + g])
                rg = slice(j * GROUP_ROWS + g * BLOCK, j * GROUP_ROWS + (g + 1) * BLOCK)
                pc_scr[rg, :] = jnp.where(is_cur, p, 0.0).astype(BF16)
                pp_scr[rg, :] = jnp.where(is_cur, 0.0, p).astype(BF16)
            grp = slice(j * GROUP_ROWS, (j + 1) * GROUP_ROWS)
            o2 = (jnp.dot(pc_scr[grp, :], vc_ref[:, ks], preferred_element_type=F32)
                  + jnp.dot(pp_scr[grp, :], vp_ref[:, ks], preferred_element_type=F32))
            for pp in range(GROUP // 2):
                c0 = (j * (GROUP // 2) + pp) * LANES
                o_ref[:, c0:c0 + LANES] = jnp.where(lo_half, o2[2 * pp * BLOCK:(2 * pp + 1) * BLOCK],
                                                    o2[(2 * pp + 1) * BLOCK:(2 * pp + 2) * BLOCK])

    cur = lambda i: (i, 0)
    prev = lambda i: (jnp.maximum(i - 1, 0), 0)
    kvs = (BLOCK, 2 * LANES)
    return pl.pallas_call(
        body, name=name, grid=(nb,),
        in_specs=[pl.BlockSpec(memory_space=pltpu.SMEM),
                  pl.BlockSpec((1, N_HEADS, BLOCK, LANES), lambda i: (i, 0, 0, 0)),
                  pl.BlockSpec(kvs, cur), pl.BlockSpec(kvs, prev), pl.BlockSpec(kvs, cur), pl.BlockSpec(kvs, prev)],
        out_specs=pl.BlockSpec((BLOCK, Q_W), cur),
        out_shape=jax.ShapeDtypeStruct((t, Q_W), F32),
        scratch_shapes=[pltpu.VMEM((N_KV_HEADS * GROUP_ROWS, LANES), BF16)] * 2,
        compiler_params=_cp(("parallel",)),
    )(sinks, qs, kd, kd, vd, vd)


def _attn_bwd(qs, kd, vd, sinks, do, name):
    nb = qs.shape[0]
    t = nb * BLOCK

    def body(sink_ref, q_ref, do_ref, kc_ref, kp_ref, vc_ref, vp_ref,
             dq_ref, dk_ref, dv_ref, dsink_ref,
             carry_k, carry_v, dsink_acc, do_scr, pc_scr, pp_scr, dsc_scr, dsp_scr):
        i = pl.program_id(0)

        @pl.when(i == 0)
        def _():
            carry_k[...] = jnp.zeros_like(carry_k)
            carry_v[...] = jnp.zeros_like(carry_v)
            dsink_acc[...] = jnp.zeros_like(dsink_acc)

        @pl.when(i < nb)
        def _():
            has_prev = i > 0
            is_cur, lo_half = _attn_masks()
            srow = lax.broadcasted_iota(jnp.int32, (SUBLANES, LANES), 0)
            scol = lax.broadcasted_iota(jnp.int32, (SUBLANES, LANES), 1)
            dsink = jnp.zeros((SUBLANES, LANES), F32)
            for j in range(N_KV_HEADS):
                ks = slice(j * LANES, (j + 1) * LANES)
                kc, kp, vc, vp = kc_ref[:, ks], kp_ref[:, ks], vc_ref[:, ks], vp_ref[:, ks]
                qg = q_ref[0, j * GROUP:(j + 1) * GROUP].reshape(GROUP_ROWS, LANES)
                for pp in range(GROUP // 2):
                    c0 = (j * (GROUP // 2) + pp) * LANES
                    dop = do_ref[:, c0:c0 + LANES]
                    r0 = j * GROUP_ROWS + 2 * pp * BLOCK
                    do_scr[r0:r0 + BLOCK, :] = jnp.where(lo_half, dop, 0.0).astype(BF16)
                    do_scr[r0 + BLOCK:r0 + 2 * BLOCK, :] = jnp.where(lo_half, 0.0, dop).astype(BF16)
                grp = slice(j * GROUP_ROWS, (j + 1) * GROUP_ROWS)
                dog = do_scr[grp, :]
                s_c = lax.dot_general(qg, kc, _DIMS["nt"], preferred_element_type=F32)
                s_p = lax.dot_general(qg, kp, _DIMS["nt"], preferred_element_type=F32)
                dp_c = lax.dot_general(dog, vc, _DIMS["nt"], preferred_element_type=F32)
                dp_p = lax.dot_general(dog, vp, _DIMS["nt"], preferred_element_type=F32)
                for g in range(GROUP):
                    rs = slice(g * BLOCK, (g + 1) * BLOCK)
                    h = j * GROUP + g
                    p, p_s = _window_softmax(s_c[rs], s_p[rs], is_cur, has_prev, sink_ref[h])
                    dp = jnp.where(is_cur, dp_c[rs], dp_p[rs])
                    delta = jnp.sum(p * dp, axis=-1, keepdims=True)
                    ds = p * (dp - delta)
                    dsv = -jnp.sum(p_s * delta, axis=0, keepdims=True)
                    dsink = dsink + jnp.where(jnp.logical_and(srow == 0, scol == h), dsv, 0.0)
                    rg = slice(j * GROUP_ROWS + g * BLOCK, j * GROUP_ROWS + (g + 1) * BLOCK)
                    pc_scr[rg, :] = jnp.where(is_cur, p, 0.0).astype(BF16)
                    pp_scr[rg, :] = jnp.where(is_cur, 0.0, p).astype(BF16)
                    dsc_scr[rg, :] = jnp.where(is_cur, ds, 0.0).astype(BF16)
                    dsp_scr[rg, :] = jnp.where(is_cur, 0.0, ds).astype(BF16)
                dsc, dsp = dsc_scr[grp, :], dsp_scr[grp, :]
                dq2 = jnp.dot(dsc, kc, preferred_element_type=F32) + jnp.dot(dsp, kp, preferred_element_type=F32)
                for pp in range(GROUP // 2):
                    c0 = (j * (GROUP // 2) + pp) * LANES
                    dq_ref[:, c0:c0 + LANES] = jnp.where(lo_half, dq2[2 * pp * BLOCK:(2 * pp + 1) * BLOCK],
                                                         dq2[(2 * pp + 1) * BLOCK:(2 * pp + 2) * BLOCK])
                dk_c = lax.dot_general(dsc, qg, _DIMS["tn"], preferred_element_type=F32)
                dk_p = lax.dot_general(dsp, qg, _DIMS["tn"], preferred_element_type=F32)
                dv_c = lax.dot_general(pc_scr[grp, :], dog, _DIMS["tn"], preferred_element_type=F32)
                dv_p = lax.dot_general(pp_scr[grp, :], dog, _DIMS["tn"], preferred_element_type=F32)
                dk_ref[:, ks] = carry_k[:, ks] + dk_p + pltpu.roll(dk_p, HEAD_DIM, 1)
                dv_ref[:, ks] = carry_v[:, ks] + dv_p + pltpu.roll(dv_p, HEAD_DIM, 1)
                carry_k[:, ks] = dk_c + pltpu.roll(dk_c, HEAD_DIM, 1)
                carry_v[:, ks] = dv_c + pltpu.roll(dv_c, HEAD_DIM, 1)
            dsink_acc[...] += dsink

        @pl.when(i == nb)
        def _():
            dk_ref[...] = carry_k[...]
            dv_ref[...] = carry_v[...]
            dsink_ref[...] = dsink_acc[...]

    last = nb - 1
    cur = lambda i: (jnp.minimum(i, last), 0)
    prev = lambda i: (jnp.clip(i - 1, 0, last), 0)
    kvs = (BLOCK, 2 * LANES)
    stk = pltpu.VMEM((N_KV_HEADS * GROUP_ROWS, LANES), BF16)
    return pl.pallas_call(
        body, name=name, grid=(nb + 1,),
        in_specs=[pl.BlockSpec(memory_space=pltpu.SMEM),
                  pl.BlockSpec((1, N_HEADS, BLOCK, LANES), lambda i: (jnp.minimum(i, last), 0, 0, 0)),
                  pl.BlockSpec((BLOCK, Q_W), cur),
                  pl.BlockSpec(kvs, cur), pl.BlockSpec(kvs, prev), pl.BlockSpec(kvs, cur), pl.BlockSpec(kvs, prev)],
        out_specs=[pl.BlockSpec((BLOCK, Q_W), cur), pl.BlockSpec(kvs, prev), pl.BlockSpec(kvs, prev),
                   pl.BlockSpec((SUBLANES, LANES), lambda i: (0, 0))],
        out_shape=[jax.ShapeDtypeStruct((t, Q_W), F32), jax.ShapeDtypeStruct((t, 2 * LANES), F32),
                   jax.ShapeDtypeStruct((t, 2 * LANES), F32), jax.ShapeDtypeStruct((SUBLANES, LANES), F32)],
        scratch_shapes=[pltpu.VMEM(kvs, F32), pltpu.VMEM(kvs, F32), pltpu.VMEM((SUBLANES, LANES), F32),
                        stk, stk, stk, stk, stk],
        compiler_params=_cp(("arbitrary",)),
    )(sinks, qs, do, kd, kd, vd, vd)


CONV_CHUNK = 64


def _fill_row_shifts(sh):
    rows = sh.shape[1] - SUBLANES
    for r in range(1, SUBLANES):
        sh[r, 0:rows, :] = sh[0, r:r + rows, :]


def _shifted_rows(sh, start, size):
    q, r = divmod(start, SUBLANES)
    return sh[r, q * SUBLANES:q * SUBLANES + size, :]


def _conv_fwd(uug, w32, cb, lg, lb, name, tm=512):
    t = uug.shape[0]
    hb = tm // HALO

    def body(m_ref, h_ref, w_ref, cb_ref, lg_ref, lb_ref, y0_ref, y2_ref, a_sh):
        i = pl.program_id(0)
        a_sh[0, HALO:, :] = m_ref[:, 0:CONV_CH] * _sigmoid(m_ref[:, CONV_CH:])
        ah = h_ref[:, 0:CONV_CH] * _sigmoid(h_ref[:, CONV_CH:])
        a_sh[0, 0:HALO, :] = jnp.where(i > 0, ah, 0.0)
        _fill_row_shifts(a_sh)
        off = HALO - (CONV_WIDTH - 1)
        for c in range(tm // CONV_CHUNK):
            r0 = c * CONV_CHUNK
            acc = jnp.zeros((CONV_CHUNK, CONV_CH), F32)
            for k in range(CONV_WIDTH):
                acc = acc + w_ref[k:k + 1, :] * _shifted_rows(a_sh, r0 + off + k, CONV_CHUNK)
            y0 = acc + cb_ref[...]
            y0_ref[r0:r0 + CONV_CHUNK, :] = y0
            mu = jnp.mean(y0, axis=-1, keepdims=True)
            dlt = y0 - mu
            rstd = lax.rsqrt(jnp.mean(dlt * dlt, axis=-1, keepdims=True) + EPS)
            y1 = (dlt * rstd) * lg_ref[...] + lb_ref[...]
            y2_ref[r0:r0 + CONV_CHUNK, :] = (y1 * _sigmoid(y1)).astype(BF16)

    vec = pl.BlockSpec((1, CONV_CH), lambda i: (0, 0))
    return pl.pallas_call(
        body, name=name, grid=(t // tm,),
        in_specs=[pl.BlockSpec((tm, UUG_W), lambda i: (i, 0)),
                  pl.BlockSpec((HALO, UUG_W), lambda i: (jnp.maximum(i * hb - 1, 0), 0)),
                  pl.BlockSpec((HALO, CONV_CH), lambda i: (0, 0)), vec, vec, vec],
        out_specs=[pl.BlockSpec((tm, CONV_CH), lambda i: (i, 0)), pl.BlockSpec((tm, CONV_CH), lambda i: (i, 0))],
        out_shape=[jax.ShapeDtypeStruct((t, CONV_CH), F32), jax.ShapeDtypeStruct((t, CONV_CH), BF16)],
        scratch_shapes=[pltpu.VMEM((SUBLANES, tm + HALO, CONV_CH), F32)],
        compiler_params=_cp(("parallel",)),
    )(uug, uug, w32, cb, lg, lb)


def _conv_bwd_ln(dc_out, w_conv_out, y0, lg, lb, name, tm=512):
    t = y0.shape[0]
    d = dc_out.shape[1]

    def body(dc_ref, w_ref, y0_ref, lg_ref, lb_ref, dy0_ref, dlg_ref, dlb_ref, dcb_ref):
        @pl.when(pl.program_id(0) == 0)
        def _():
            dlg_ref[...] = jnp.zeros_like(dlg_ref)
            dlb_ref[...] = jnp.zeros_like(dlb_ref)
            dcb_ref[...] = jnp.zeros_like(dcb_ref)

        y0 = y0_ref[...]
        mu = jnp.mean(y0, axis=-1, keepdims=True)
        dlt = y0 - mu
        rstd = lax.rsqrt(jnp.mean(dlt * dlt, axis=-1, keepdims=True) + EPS)
        yh = dlt * rstd
        y1 = yh * lg_ref[...] + lb_ref[...]
        sg = _sigmoid(y1)
        dy2 = lax.dot_general(dc_ref[...], w_ref[...], _DIMS["nt"], preferred_element_type=F32)
        dy1 = dy2 * (sg * (1.0 + y1 * (1.0 - sg)))
        dlg_ref[...] += _rowgroup_sum(dy1 * yh)
        dlb_ref[...] += _rowgroup_sum(dy1)
        dyh = dy1 * lg_ref[...]
        dy0 = rstd * (dyh - jnp.mean(dyh, axis=-1, keepdims=True)
                      - yh * jnp.mean(dyh * yh, axis=-1, keepdims=True))
        dcb_ref[...] += _rowgroup_sum(dy0)
        dy0_ref[...] = dy0

    row = pl.BlockSpec((tm, CONV_CH), lambda i: (i, 0))
    vec = pl.BlockSpec((1, CONV_CH), lambda i: (0, 0))
    acc = pl.BlockSpec((SUBLANES, CONV_CH), lambda i: (0, 0))
    accs = jax.ShapeDtypeStruct((SUBLANES, CONV_CH), F32)
    return pl.pallas_call(
        body, name=name, grid=(t // tm,),
        in_specs=[pl.BlockSpec((tm, d), lambda i: (i, 0)), pl.BlockSpec((CONV_CH, d), lambda i: (0, 0)), row, vec, vec],
        out_specs=[row, acc, acc, acc],
        out_shape=[jax.ShapeDtypeStruct((t, CONV_CH), F32), accs, accs, accs],
        compiler_params=_cp(("arbitrary",)),
    )(dc_out, w_conv_out, y0, lg, lb)


def _conv_bwd_taps(dy0, uug, w32, name, tm=512):
    t = uug.shape[0]
    hb = tm // HALO
    n_halo_blocks = t // HALO
    nt = t // tm

    def body(dm_ref, dn_ref, m_ref, h_ref, w_ref, duug_ref, dw_ref, a_sh, d_sh):
        i = pl.program_id(0)

        @pl.when(i == 0)
        def _():
            dw_ref[...] = jnp.zeros_like(dw_ref)

        u = m_ref[:, 0:CONV_CH]
        sg = _sigmoid(m_ref[:, CONV_CH:])
        a_sh[0, HALO:, :] = u * sg
        ah = h_ref[:, 0:CONV_CH] * _sigmoid(h_ref[:, CONV_CH:])
        a_sh[0, 0:HALO, :] = jnp.where(i > 0, ah, 0.0)
        d_sh[0, 0:tm, :] = dm_ref[...]
        d_sh[0, tm:, :] = jnp.where(i < nt - 1, dn_ref[...], 0.0)
        _fill_row_shifts(a_sh)
        _fill_row_shifts(d_sh)
        off = HALO - (CONV_WIDTH - 1)
        for c in range(tm // CONV_CHUNK):
            r0 = c * CONV_CHUNK
            da = jnp.zeros((CONV_CHUNK, CONV_CH), F32)
            for k in range(CONV_WIDTH):
                sh = CONV_WIDTH - 1 - k
                da = da + w_ref[k:k + 1, :] * _shifted_rows(d_sh, r0 + sh, CONV_CHUNK)
            uc = u[r0:r0 + CONV_CHUNK, :]
            sc = sg[r0:r0 + CONV_CHUNK, :]
            duug_ref[r0:r0 + CONV_CHUNK, 0:CONV_CH] = (da * sc).astype(BF16)
            duug_ref[r0:r0 + CONV_CHUNK, CONV_CH:] = (da * uc * sc * (1.0 - sc)).astype(BF16)
            dch = d_sh[0, r0:r0 + CONV_CHUNK, :]
            for k in range(CONV_WIDTH):
                prod = dch * _shifted_rows(a_sh, r0 + off + k, CONV_CHUNK)
                dw_ref[k * SUBLANES:(k + 1) * SUBLANES, :] += _rowgroup_sum(prod)

    return pl.pallas_call(
        body, name=name, grid=(nt,),
        in_specs=[pl.BlockSpec((tm, CONV_CH), lambda i: (i, 0)),
                  pl.BlockSpec((HALO, CONV_CH), lambda i: (jnp.minimum((i + 1) * hb, n_halo_blocks - 1), 0)),
                  pl.BlockSpec((tm, UUG_W), lambda i: (i, 0)),
                  pl.BlockSpec((HALO, UUG_W), lambda i: (jnp.maximum(i * hb - 1, 0), 0)),
                  pl.BlockSpec((HALO, CONV_CH), lambda i: (0, 0))],
        out_specs=[pl.BlockSpec((tm, UUG_W), lambda i: (i, 0)),
                   pl.BlockSpec((CONV_WIDTH * SUBLANES, CONV_CH), lambda i: (0, 0))],
        out_shape=[jax.ShapeDtypeStruct((t, UUG_W), BF16),
                   jax.ShapeDtypeStruct((CONV_WIDTH * SUBLANES, CONV_CH), F32)],
        scratch_shapes=[pltpu.VMEM((SUBLANES, tm + HALO, CONV_CH), F32),
                        pltpu.VMEM((SUBLANES, tm + HALO, CONV_CH), F32)],
        compiler_params=_cp(("arbitrary",)),
    )(dy0, dy0, uug, uug, w32)


def _merge_fwd(y2, w_conv_out, a_out, gg, name, tm=512):
    t, d = a_out.shape
    k = y2.shape[1]

    def body(y_ref, w_ref, a_ref, g_ref, c_ref, o_ref):
        c = jnp.dot(y_ref[...], w_ref[...], preferred_element_type=F32)
        c_ref[...] = c
        o_ref[...] = (_sigmoid(g_ref[:, 0:d]) * a_ref[...] + _sigmoid(g_ref[:, d:]) * c).astype(BF16)

    row = pl.BlockSpec((tm, d), lambda i: (i, 0))
    return pl.pallas_call(
        body, name=name, grid=(t // tm,),
        in_specs=[pl.BlockSpec((tm, k), lambda i: (i, 0)), pl.BlockSpec((k, d), lambda i: (0, 0)), row,
                  pl.BlockSpec((tm, 2 * d), lambda i: (i, 0))],
        out_specs=[row, row],
        out_shape=[jax.ShapeDtypeStruct((t, d), F32), jax.ShapeDtypeStruct((t, d), BF16)],
        compiler_params=_cp(("parallel",)),
    )(y2, w_conv_out, a_out, gg)


def _merge_bwd(dx1, w_out, a_out, c_out, gg, name, tm=512):
    t, d = a_out.shape

    def body(dx_ref, w_ref, a_ref, c_ref, g_ref, da_ref, dc_ref, dg_ref):
        dmv = lax.dot_general(dx_ref[...].astype(BF16), w_ref[...], _DIMS["nt"], preferred_element_type=F32)
        sa = _sigmoid(g_ref[:, 0:d])
        sb = _sigmoid(g_ref[:, d:])
        da_ref[...] = dmv * sa
        dc_ref[...] = (dmv * sb).astype(BF16)
        dg_ref[:, 0:d] = (dmv * a_ref[...] * sa * (1.0 - sa)).astype(BF16)
        dg_ref[:, d:] = (dmv * c_ref[...] * sb * (1.0 - sb)).astype(BF16)

    row = pl.BlockSpec((tm, d), lambda i: (i, 0))
    wide = pl.BlockSpec((tm, 2 * d), lambda i: (i, 0))
    return pl.pallas_call(
        body, name=name, grid=(t // tm,),
        in_specs=[row, pl.BlockSpec((d, d), lambda i: (0, 0)), row, row, wide], out_specs=[row, row, wide],
        out_shape=[jax.ShapeDtypeStruct((t, d), F32), jax.ShapeDtypeStruct((t, d), BF16),
                   jax.ShapeDtypeStruct((t, 2 * d), BF16)],
        compiler_params=_cp(("parallel",)),
    )(dx1, w_out, a_out, c_out, gg)


FF_TN = 1408


def _ffn_up_fwd(h2, wgu, name, tm=512, after=None):
    t, d = h2.shape
    nj = D_FF // FF_TN

    def body(*refs):
        h_ref, wg_ref, wu_ref = refs[:3]
        o_ref, g_ref, u_ref = refs[-3:]
        hv = h_ref[...]
        g = jnp.dot(hv, wg_ref[...], preferred_element_type=F32)
        u = jnp.dot(hv, wu_ref[...], preferred_element_type=F32)
        o_ref[...] = ((g * _sigmoid(g)) * u).astype(BF16)
        g_ref[...] = g.astype(BF16)
        u_ref[...] = u.astype(BF16)

    tile = pl.BlockSpec((tm, FF_TN), lambda j, i: (i, j))
    o = jax.ShapeDtypeStruct((t, D_FF), BF16)
    return pl.pallas_call(
        body, name=name, grid=(nj, t // tm),
        in_specs=[pl.BlockSpec((tm, d), lambda j, i: (i, 0)),
                  pl.BlockSpec((d, FF_TN), lambda j, i: (0, j)),
                  pl.BlockSpec((d, FF_TN), lambda j, i: (0, j + nj))] + ([] if after is None else [_ANY]),
        out_specs=[tile, tile, tile], out_shape=[o, o, o],
        compiler_params=_cp(("parallel", "parallel")),
    )(h2, wgu, wgu, *([] if after is None else [after]))


def _ffn_bwd_mid(g, u, dx2, wd, name, tm=512, after=None):
    t, d = dx2.shape
    nj = D_FF // FF_TN

    def body(*refs):
        g_ref, u_ref, dx_ref, wd_ref = refs[:4]
        dgu_ref = refs[-1]
        gv = g_ref[...].astype(F32)
        uv = u_ref[...].astype(F32)
        dact = lax.dot_general(dx_ref[...].astype(BF16), wd_ref[...], _DIMS["nt"], preferred_element_type=F32)
        sg = _sigmoid(gv)
        dgu_ref[0] = (dact * uv * (sg * (1.0 + gv * (1.0 - sg)))).astype(BF16)
        dgu_ref[1] = (dact * (gv * sg)).astype(BF16)

    tile = pl.BlockSpec((tm, FF_TN), lambda j, i: (i, j))
    o = jax.ShapeDtypeStruct((t, D_FF), BF16)
    return pl.pallas_call(
        body, name=name, grid=(nj, t // tm),
        in_specs=[tile, tile, pl.BlockSpec((tm, d), lambda j, i: (i, 0)),
                  pl.BlockSpec((FF_TN, d), lambda j, i: (j, 0))] + ([] if after is None else [_ANY]),
        out_specs=pl.BlockSpec((2, tm, FF_TN), lambda j, i: (0, i, j)),
        out_shape=jax.ShapeDtypeStruct((2, t, D_FF), BF16),
        compiler_params=_cp(("parallel", "parallel")),
    )(g, u, dx2, wd, *([] if after is None else [after]))


def _loss_head(y, target, name, tm=512):
    t, d = y.shape

    def body(y_ref, t_ref, dy_ref, loss_ref, dyb_ref):
        @pl.when(pl.program_id(0) == 0)
        def _():
            loss_ref[...] = jnp.zeros_like(loss_ref)

        e = y_ref[...] - t_ref[...]
        dy_ref[...] = e * (1.0 / d)
        dyb_ref[...] = (e * (1.0 / d)).astype(BF16)
        s = _rowgroup_sum(e * e)
        acc = s[:, 0:LANES]
        for c in range(1, d // LANES):
            acc = acc + s[:, c * LANES:(c + 1) * LANES]
        loss_ref[...] += acc * (0.5 / d)

    row = pl.BlockSpec((tm, d), lambda i: (i, 0))
    return pl.pallas_call(
        body, name=name, grid=(t // tm,),
        in_specs=[row, row], out_specs=[row, pl.BlockSpec((SUBLANES, LANES), lambda i: (0, 0)), row],
        out_shape=[jax.ShapeDtypeStruct((t, d), F32), jax.ShapeDtypeStruct((SUBLANES, LANES), F32),
                   jax.ShapeDtypeStruct((t, d), BF16)],
        compiler_params=_cp(("arbitrary",)),
    )(y, target)


def _exchange(arrays, scatter, name):
    n = len(arrays)

    def body(*refs):
        ins, outs = refs[:n], refs[n:2 * n]
        send_sems, recv_sems, local_sems = refs[2 * n:]
        x, y, c = lax.axis_index("x"), lax.axis_index("y"), lax.axis_index("c")
        me = 4 * x + 2 * y + c

        def peer(k):
            px, py, pc = x ^ ((k >> 2) & 1), y ^ ((k >> 1) & 1), c ^ (k & 1)
            return (px, py, pc), 4 * px + 2 * py + pc

        def src(a, dst_id):
            return ins[a].at[dst_id] if scatter else ins[a]

        locals_ = [pltpu.make_async_copy(src(a, me), outs[a].at[me], local_sems.at[a]) for a in range(n)]
        for cp in locals_:
            cp.start()
        sends = []
        for k in range(1, N_DEV):
            dev, pid = peer(k)
            for a in range(n):
                sends.append(pltpu.make_async_remote_copy(
                    src_ref=src(a, pid), dst_ref=outs[a].at[me],
                    send_sem=send_sems.at[a, k], recv_sem=recv_sems.at[a, k],
                    device_id=dev, device_id_type=pl.DeviceIdType.MESH))
        for cp in sends:
            cp.start()
        for k in range(1, N_DEV):
            dev, pid = peer(k)
            for a in range(n):
                pltpu.make_async_remote_copy(
                    src_ref=src(a, pid), dst_ref=outs[a].at[pid],
                    send_sem=send_sems.at[a, k], recv_sem=recv_sems.at[a, k],
                    device_id=dev, device_id_type=pl.DeviceIdType.MESH).wait_recv()
        for cp in sends:
            cp.wait_send()
        for cp in locals_:
            cp.wait()

    def out_shape(a):
        return jax.ShapeDtypeStruct(a.shape if scatter else (N_DEV,) + a.shape, a.dtype)

    anyspec = pl.BlockSpec(memory_space=pl.ANY)
    return pl.pallas_call(
        body, name=name,
        in_specs=[anyspec] * n, out_specs=[anyspec] * n,
        out_shape=[out_shape(a) for a in arrays],
        scratch_shapes=[pltpu.SemaphoreType.DMA((n, N_DEV)), pltpu.SemaphoreType.DMA((n, N_DEV)),
                        pltpu.SemaphoreType.DMA((n,))],
    )(*arrays)


def _gather_two_level(arrays, name):
    n = len(arrays)
    nk = N_DEV - 1

    def body(*refs):
        ins, outs = refs[:n], refs[n:2 * n]
        send_sems, recv_sems, local_sems = refs[2 * n:]
        x, y, c = lax.axis_index("x"), lax.axis_index("y"), lax.axis_index("c")
        me, sibling = (x, y, c), (x, y, 1 - c)
        chips = [(1 - x, y), (x, 1 - y), (1 - x, 1 - y)]

        def slot(a, dev):
            return outs[a].at[4 * dev[0] + 2 * dev[1] + dev[2]]

        def copy(a, k, block, to, src=None):
            return pltpu.make_async_remote_copy(
                src_ref=slot(a, block) if src is None else src, dst_ref=slot(a, block),
                send_sem=send_sems.at[a * nk + k], recv_sem=recv_sems.at[a * nk + k],
                device_id=to, device_id_type=pl.DeviceIdType.MESH)

        mine = [pltpu.make_async_copy(ins[a], slot(a, me), local_sems.at[a]) for a in range(n)]
        for cp in mine:
            cp.start()
        first = []
        for a in range(n):
            first.append(copy(a, 0, me, sibling, src=ins[a]))
            first += [copy(a, 1 + j, me, (*chip, c), src=ins[a]) for j, chip in enumerate(chips)]
        for cp in first:
            cp.start()
        passed = []
        for j, chip in enumerate(chips):
            for a in range(n):
                copy(a, 1 + j, (*chip, c), me).wait_recv()
                fwd = copy(a, 4 + j, (*chip, c), sibling)
                fwd.start()
                passed.append(fwd)
        for a in range(n):
            copy(a, 0, sibling, me).wait_recv()
            for j, chip in enumerate(chips):
                copy(a, 4 + j, (*chip, 1 - c), me).wait_recv()
        for cp in first + passed:
            cp.wait_send()
        for cp in mine:
            cp.wait()

    anyspec = pl.BlockSpec(memory_space=pl.ANY)
    return pl.pallas_call(
        body, name=name,
        in_specs=[anyspec] * n, out_specs=[anyspec] * n,
        out_shape=[jax.ShapeDtypeStruct((N_DEV,) + a.shape, a.dtype) for a in arrays],
        scratch_shapes=[pltpu.SemaphoreType.DMA((n * nk,)), pltpu.SemaphoreType.DMA((n * nk,)),
                        pltpu.SemaphoreType.DMA((n,))],
    )(*arrays)


_HBM = pl.BlockSpec(memory_space=pltpu.HBM)
_SEM = pl.BlockSpec(memory_space=pltpu.SEMAPHORE)
_ANY = pl.BlockSpec(memory_space=pl.ANY)
_EFFECT = pltpu.SideEffectType.DATAFLOW_SIDE_EFFECTING


def _mesh_peer(k):
    x, y, c = lax.axis_index("x"), lax.axis_index("y"), lax.axis_index("c")
    px, py, pc = x ^ ((k >> 2) & 1), y ^ ((k >> 1) & 1), c ^ (k & 1)
    return (px, py, pc), 4 * px + 2 * py + pc


def _exchange_start(arrays, scatter, name, after):
    n = len(arrays)
    lands = [lax.empty(a.shape if scatter else (N_DEV,) + a.shape, a.dtype) for a in arrays]

    def body(*refs):
        ins, land_refs = refs[:n], refs[n:2 * n]
        send_sems, recv_sems = refs[2 * n + 1], refs[2 * n + 2]
        token = refs[-1]
        _, me = _mesh_peer(0)
        for k in range(1, N_DEV):
            dev, pid = _mesh_peer(k)
            for a in range(n):
                pltpu.make_async_remote_copy(
                    src_ref=ins[a].at[pid] if scatter else ins[a], dst_ref=land_refs[a].at[me],
                    send_sem=send_sems.at[a * N_DEV + k], recv_sem=recv_sems.at[a * N_DEV + k],
                    device_id=dev, device_id_type=pl.DeviceIdType.MESH).start()
        token[...] = jnp.zeros_like(token)

    hbm_in = [pltpu.with_memory_space_constraint(a, pltpu.HBM) for a in list(arrays) + lands]
    outs = pl.pallas_call(
        body, name=name,
        in_specs=[_HBM] * (2 * n) + [_ANY],
        out_specs=[_SEM, _SEM] + [_HBM] * (2 * n) + [pl.BlockSpec(memory_space=pltpu.VMEM)],
        out_shape=[pltpu.SemaphoreType.DMA((n * N_DEV,)), pltpu.SemaphoreType.DMA((n * N_DEV,))]
        + [pltpu.HBM(a.shape, a.dtype) for a in hbm_in]
        + [jax.ShapeDtypeStruct((SUBLANES, LANES), F32)],
        input_output_aliases={i: 2 + i for i in range(2 * n)},
        compiler_params=pltpu.CompilerParams(has_side_effects=_EFFECT),
    )(*hbm_in, after)
    return outs[0], outs[1], outs[2:2 + n], outs[2 + n:2 + 2 * n], outs[-1]


def _exchange_wait(started, scatter, name, after):
    send_sems, recv_sems, srcs, lands, _ = started
    n = len(srcs)

    def body(*refs):
        ins, land_refs = refs[:n], refs[n:2 * n]
        send_sems, recv_sems = refs[2 * n], refs[2 * n + 1]
        copies = []
        for k in range(1, N_DEV):
            dev, pid = _mesh_peer(k)
            for a in range(n):
                copies.append(pltpu.make_async_remote_copy(
                    src_ref=ins[a].at[pid] if scatter else ins[a], dst_ref=land_refs[a].at[pid],
                    send_sem=send_sems.at[a * N_DEV + k], recv_sem=recv_sems.at[a * N_DEV + k],
                    device_id=dev, device_id_type=pl.DeviceIdType.MESH))
        for cp in copies:
            cp.wait_recv()
        for cp in copies:
            cp.wait_send()

    outs = pl.pallas_call(
        body, name=name,
        in_specs=[_HBM] * (2 * n) + [_SEM, _SEM, _ANY],
        out_specs=[_HBM] * (2 * n),
        out_shape=[pltpu.HBM(a.shape, a.dtype) for a in list(srcs) + list(lands)],
        input_output_aliases={i: i for i in range(2 * n)},
        compiler_params=pltpu.CompilerParams(has_side_effects=_EFFECT),
    )(*srcs, *lands, send_sems, recv_sems, after)
    me = 4 * lax.axis_index("x") + 2 * lax.axis_index("y") + lax.axis_index("c")
    filled = []
    for src, land in zip(outs[:n], outs[n:]):
        own = lax.dynamic_index_in_dim(src, me, 0, keepdims=True) if scatter else src[None]
        filled.append(lax.dynamic_update_slice(land, own, (me,) + (0,) * (land.ndim - 1)))
    return filled


def _adamw(parts, w, m, v, name, tr):
    nl = len(parts)
    r, c = parts[0].shape[1:]
    assert w.shape == (nl * r, c) and r % tr == 0, (name, w.shape, r, tr)
    nt = r // tr
    c1 = 1.0 - ADAM_B1 ** ADAM_STEP
    c2 = 1.0 - ADAM_B2 ** ADAM_STEP

    def body(*refs):
        p_refs = refs[:nl]
        w_ref, m_ref, v_ref, g_ref, d_ref, nm_ref, nv_ref = refs[nl:]
        layer = pl.program_id(0)
        for k in range(nl):
            @pl.when(layer == k)
            def _(p_ref=p_refs[k]):
                g = p_ref[0].astype(F32)
                for s in range(1, N_DEV):
                    g = g + p_ref[s].astype(F32)
                nm = ADAM_B1 * m_ref[...] + (1.0 - ADAM_B1) * g
                nv = ADAM_B2 * v_ref[...] + (1.0 - ADAM_B2) * (g * g)
                g_ref[...] = g
                nm_ref[...] = nm
                nv_ref[...] = nv
                d_ref[...] = -ADAM_LR * ((nm / c1) / (jnp.sqrt(nv / c2) + ADAM_EPS) + ADAM_WD * w_ref[...])

    def part_spec(k):
        return pl.BlockSpec((N_DEV, tr, c), lambda l, i: (0, jnp.where(l == k, i, 0), 0))

    row = pl.BlockSpec((tr, c), lambda l, i: (l * nt + i, 0))
    o = jax.ShapeDtypeStruct((nl * r, c), F32)
    return pl.pallas_call(
        body, name=name, grid=(nl, nt),
        in_specs=[part_spec(k) for k in range(nl)] + [row, row, row],
        out_specs=[row, row, row, row], out_shape=[o, o, o, o],
        compiler_params=_cp(("arbitrary", "arbitrary")),
    )(*parts, w, m, v)


def _with_token(gain, token):
    return gain if token is None else gain + token[0:1, 0:1]


def _layer_fwd(x, wl, sl, tabs, l, rest_fn=None, rest2_fn=None, h=None, after=None, next_gain=None):
    ctab, stab, bd = tabs
    n = f"l{l}_"
    if h is None:
        h = _rmsnorm_fwd(x, sl["norm_mix"], n + "norm_mix")
    qkv = _mm(h, wl["w_qkv"], "nn", out_dtype=F32, name=n + "proj_qkv", tm=1024, tn=QKV_W, tk=D_MODEL, after=after)
    uug = _mm(h, wl["w_uug"], "nn", out_dtype=F32, name=n + "proj_uug", tm=1024, tn=UUG_W, tk=D_MODEL, after=after)
    gg = _mm(h, wl["w_gg"], "nn", out_dtype=F32, name=n + "proj_gg", tm=1024, tn=GG_W, tk=D_MODEL, after=after)
    qs, kd, vd = _qk_prep_fwd(qkv, sl["gq"], sl["gk"], ctab, stab, bd, n + "qk_prep")
    a_out = _attn_fwd(qs, kd, vd, sl["sinks"], n + "attn")
    y0, y2 = _conv_fwd(uug, sl["conv_w32"], sl["conv_b"], sl["ln_g"], sl["ln_b"], n + "conv")
    token = None
    if rest_fn is not None:
        rest, token = rest_fn(y2)
        wl = {**wl, **rest}
    c_out, merged = _merge_fwd(y2, wl["w_conv_out"], a_out, gg, n + "merge")
    x1, h2 = _mm_rows([(merged, wl["w_out"])], "nn", name=n + "out_proj", tm=512, resid=x,
                      norm_fwd=_with_token(sl["norm_ffn"], token))
    token2 = None
    if rest2_fn is not None:
        rest2, token2 = rest2_fn(x1)
        wl = {**wl, **rest2}
    act, g_pre, u_pre = _ffn_up_fwd(h2, wl["w_gate_up"], n + "ffn_up", after=token2)
    if next_gain is None:
        x2 = _mm_rows([(act, wl["w_down"])], "nn", name=n + "ffn_down", tm=512, resid=x1)
        h_next = None
    else:
        x2, h_next = _mm_rows([(act, wl["w_down"])], "nn", name=n + "ffn_down", tm=512, resid=x1, norm_fwd=next_gain)
    saved = dict(x=x, h=h, qkv=qkv, uug=uug, gg=gg, qs=qs, kd=kd, vd=vd, a_out=a_out, y0=y0, y2=y2,
                 c_out=c_out, merged=merged, x1=x1, h2=h2, act=act, g_pre=g_pre, u_pre=u_pre)
    return x2, saved, wl, h_next


def _layer_bwd(dx2, sv, wl, sl, tabs, l, after=None, ffn_hook=None, mix_hook=None, dx2_b=None):
    ctab, stab, bd = tabs
    n = f"l{l}_b_"
    tk = 2048
    gw, gs = {}, {}
    dx2_mm = dx2 if dx2_b is None else dx2_b
    gw["w_down"] = _mm(sv["act"], dx2_mm, "tn", out_dtype=BF16, name=n + "dw_down", tm=FF_TN, tn=D_MODEL, tk=tk,
                       after=after)
    dgu = _ffn_bwd_mid(sv["g_pre"], sv["u_pre"], dx2_mm, wl["w_down"], n + "ffn_mid", after=after)
    gw["w_gate_up"] = _mm(sv["h2"], dgu, "tn", out_dtype=BF16, name=n + "dw_gate_up", tm=D_MODEL, tn=FF_TN, tk=tk)
    token = None if ffn_hook is None else ffn_hook(gw)
    dx1, gs["norm_ffn"] = _mm_rows([(dgu, wl["w_gate_up"], 0, 0), (dgu, wl["w_gate_up"], 1, 1)], "nt", name=n + "dh2", tm=512,
                                   norm_bwd=(sv["x1"], _with_token(sl["norm_ffn"], token), dx2))
    gw["w_out"] = _mm(sv["merged"], dx1, "tn", out_dtype=BF16, name=n + "dw_out", tm=D_MODEL, tn=D_MODEL, tk=tk)
    da_out, dc_out, dgg = _merge_bwd(dx1, wl["w_out"], sv["a_out"], sv["c_out"], sv["gg"], n + "merge")
    gw["w_conv_out"] = _mm(sv["y2"], dc_out, "tn", out_dtype=BF16, name=n + "dw_conv_out", tm=CONV_CH, tn=D_MODEL,
                           tk=tk)
    dy0, gs["ln_g"], gs["ln_b"], gs["conv_b"] = _conv_bwd_ln(dc_out, wl["w_conv_out"], sv["y0"], sl["ln_g"],
                                                                 sl["ln_b"], n + "conv_ln")
    duug, gs["conv_w"] = _conv_bwd_taps(dy0, sv["uug"], sl["conv_w32"], n + "conv_taps")
    dqs, dkd, dvd, gs["sinks"] = _attn_bwd(sv["qs"], sv["kd"], sv["vd"], sl["sinks"], da_out, n + "attn")
    dqkv, gs["gq"], gs["gk"] = _qk_prep_bwd(dqs, dkd, dvd, sv["qkv"], sl["gq"], sl["gk"], ctab, stab, bd,
                                            n + "qk_prep")
    gw["w_qkv"] = _mm(sv["h"], dqkv, "tn", out_dtype=BF16, name=n + "dw_qkv", tm=D_MODEL, tn=QKV_W, tk=tk)
    gw["w_uug"] = _mm(sv["h"], duug, "tn", out_dtype=BF16, name=n + "dw_uug", tm=D_MODEL, tn=UUG_W, tk=tk)
    gw["w_gg"] = _mm(sv["h"], dgg, "tn", out_dtype=BF16, name=n + "dw_gg", tm=D_MODEL, tn=GG_W, tk=tk)
    token_mix = None if mix_hook is None else mix_hook(gw)
    dx, gs["norm_mix"], dx_b = _mm_rows([(dqkv, wl["w_qkv"]), (duug, wl["w_uug"]), (dgg, wl["w_gg"])], "nt",
                                        name=n + "dh", tm=512, bf16_copy=True,
                                        norm_bwd=(sv["x"], _with_token(sl["norm_mix"], token_mix), dx1))
    return dx, gw, gs, dx_b


def _cols_to_full(g):
    n, l, r, c = g.shape
    return jnp.transpose(g, (1, 2, 0, 3)).reshape(l, r, n * c)


def _rows_to_full(g):
    n, l, r, c = g.shape
    return jnp.transpose(g, (1, 0, 2, 3)).reshape(l, n * r, c)


def _full_to_cols(w):
    l, r, c = w.shape
    return jnp.transpose(w.reshape(l, r, N_DEV, c // N_DEV), (2, 0, 1, 3))


def _full_to_rows(w):
    l, r, c = w.shape
    return jnp.transpose(w.reshape(l, N_DEV, r // N_DEV, c), (1, 0, 2, 3))


SMALL = (("norm_mix", D_MODEL), ("q_norm", HEAD_DIM), ("k_norm", HEAD_DIM), ("sinks", N_HEADS),
         ("conv_w", CONV_WIDTH * CONV_CH), ("conv_b", CONV_CH), ("conv_ln_g", CONV_CH), ("conv_ln_b", CONV_CH),
         ("norm_ffn", D_MODEL))
SMALL_TOTAL = DEPTH * sum(s for _, s in SMALL)
SMALL_ROWS = -(-SMALL_TOTAL // (LANES * SUBLANES)) * SUBLANES


def _pack_small(d):
    flat = jnp.concatenate([d[k].reshape(-1).astype(F32) for k, _ in SMALL])
    flat = jnp.pad(flat, (0, SMALL_ROWS * LANES - SMALL_TOTAL))
    return flat.reshape(SMALL_ROWS, LANES)


def _unpack_small(buf, shapes):
    flat = buf.reshape(-1)
    out, o = {}, 0
    for k, s in SMALL:
        out[k] = flat[o:o + DEPTH * s].reshape(shapes[k])
        o += DEPTH * s
    return out


def kernel(x, norm_mix, w_in, q_norm, k_norm, sinks, conv_w, conv_b, conv_ln_g, conv_ln_b, w_conv_out, w_out, norm_ffn, w_gate_up, w_down, loss_target, m_norm_mix, m_w_in, m_q_norm, m_k_norm, m_sinks, m_conv_w, m_conv_b, m_conv_ln_g, m_conv_ln_b, m_w_conv_out, m_w_out, m_norm_ffn, m_w_gate_up, m_w_down, v_norm_mix, v_w_in, v_q_norm, v_k_norm, v_sinks, v_conv_w, v_conv_b, v_conv_ln_g, v_conv_ln_b, v_w_conv_out, v_w_out, v_norm_ffn, v_w_gate_up, v_w_down):
    t = x.shape[1]
    me = 4 * lax.axis_index("x") + 2 * lax.axis_index("y") + lax.axis_index("c")
    xs = x.reshape(t, D_MODEL)
    target = loss_target.reshape(t, D_MODEL)

    def shards_in(l):
        return [w_in[l].astype(BF16)]

    def shards_rest(l):
        return [w_conv_out[l].astype(BF16), w_out[l].astype(BF16), w_gate_up[l].astype(BF16), w_down[l].astype(BF16)]

    def weights_in(g_in):
        f_in = _cols_to_full(g_in[:, None])[0]
        return dict(w_qkv=f_in[:, :QKV_W], w_uug=f_in[:, QKV_W:QKV_W + UUG_W], w_gg=f_in[:, QKV_W + UUG_W:])

    def weights_merge(g):
        g_co, g_out = g
        return dict(w_conv_out=_cols_to_full(g_co[:, None])[0], w_out=_rows_to_full(g_out[:, None])[0])

    def weights_ffn(g):
        g_gu, g_dn = g
        f_gu = _cols_to_full(g_gu[:, None])[0]
        return dict(w_gate_up=f_gu, w_down=_rows_to_full(g_dn[:, None])[0])

    def weights_rest(g):
        return {**weights_merge(g[:2]), **weights_ffn(g[2:])}

    g_in0, g_cw = _gather_two_level(shards_in(0) + [conv_w], name="gather_w_in_0")
    f_cw = _cols_to_full(g_cw)
    tabs = _rope_tables(t) + (_block_diag_ones(),)

    def layer_small(l, token):
        return dict(norm_mix=_with_token(norm_mix[l][None], token), norm_ffn=norm_ffn[l][None],
                    gq=jnp.tile(q_norm[l], 2)[None], gk=jnp.tile(k_norm[l], 2)[None], sinks=sinks[l],
                    conv_w32=jnp.pad(f_cw[l], ((0, HALO - CONV_WIDTH), (0, 0))),
                    conv_b=conv_b[l][None], ln_g=conv_ln_g[l][None], ln_b=conv_ln_b[l][None])

    wls, sls, saved = [], [], []
    cur = xs
    flight = {"next": None}

    def start_next(l, after):
        flight["next"] = _exchange_start(shards_in(l + 1) + shards_rest(l + 1), False, f"gather_start_{l + 1}",
                                         after=after)
        return flight["next"][4]

    merge0 = _exchange_start(shards_rest(0)[:2], False, "gather_start_merge_0", after=g_in0)
    ffn0 = _exchange_start(shards_rest(0)[2:], False, "gather_start_ffn_0", after=merge0[4])

    def rest_fn0(after):
        g = _exchange_wait(merge0, False, "gather_wait_merge_0", after=after)
        return weights_merge(g), start_next(0, g[0])

    def rest2_fn0(after):
        return weights_ffn(_exchange_wait(ffn0, False, "gather_wait_ffn_0", after=after)), None

    gathered, h_next = None, None
    for l in range(DEPTH):
        if l == 0:
            w_first, token, rest_fn, rest2_fn = weights_in(g_in0), ffn0[4], rest_fn0, rest2_fn0
        else:
            w_first = {**weights_in(gathered[0]), **weights_rest(gathered[1:])}
            token = start_next(l, gathered[0]) if l + 1 < DEPTH else None
            rest_fn, rest2_fn = None, None
        sls.append(layer_small(l, token if l == 0 else None))
        cur, sv, wl, h_next = _layer_fwd(cur, w_first, sls[l], tabs, l, rest_fn=rest_fn, rest2_fn=rest2_fn, h=h_next,
                                         after=None if l == 0 else token,
                                         next_gain=norm_mix[l + 1][None] if l + 1 < DEPTH else None)
        wls.append(wl)
        saved.append(sv)
        if l + 1 < DEPTH:
            gathered = _exchange_wait(flight["next"], False, f"gather_wait_{l + 1}", after=cur)
    dy, loss_part, dcur_b = _loss_head(cur, target, "loss_head")
    loss = lax.psum(jnp.sum(loss_part), ("x", "y", "c"))

    def slabs_ffn(gw):
        d_gu = gw["w_gate_up"][None]
        return [_full_to_cols(d_gu)[:, 0].astype(BF16), _full_to_rows(gw["w_down"][None])[:, 0].astype(BF16)]

    def slabs_mix(gw):
        d_in = jnp.concatenate([gw["w_qkv"], gw["w_uug"], gw["w_gg"]], axis=1)[None]
        return [_full_to_cols(d_in)[:, 0].astype(BF16), _full_to_cols(gw["w_conv_out"][None])[:, 0].astype(BF16),
                _full_to_rows(gw["w_out"][None])[:, 0].astype(BF16)]

    gss = [None] * DEPTH
    parts_ffn, parts_mix = [None] * DEPTH, [None] * DEPTH
    dcur = dy
    state = {"mix": None, "ffn": None}

    def make_ffn_hook(l):
        def hook(gw):
            sends = slabs_ffn(gw)
            after = sends[0]
            if state["mix"] is not None:
                parts_mix[l + 1] = _exchange_wait(state["mix"], True, f"scatter_wait_mix_{l + 1}", after=sends[0])
                after = parts_mix[l + 1][0]
            state["ffn"] = _exchange_start(sends, True, f"scatter_start_ffn_{l}", after=after)
            return state["ffn"][4]
        return hook

    def last_mix_hook(gw):
        sends = slabs_mix(gw)
        parts_ffn[0] = _exchange_wait(state["ffn"], True, "scatter_wait_ffn_0", after=sends[0])
        state["mix"] = _exchange_start(sends, True, "scatter_start_mix_0", after=parts_ffn[0][0])
        return state["mix"][4]

    for l in reversed(range(DEPTH)):
        dcur, gw, gss[l], dcur_b = _layer_bwd(dcur, saved[l], wls[l], sls[l], tabs, l,
                                              after=None if state["mix"] is None else state["mix"][4],
                                              ffn_hook=make_ffn_hook(l),
                                              mix_hook=last_mix_hook if l == 0 else None, dx2_b=dcur_b)
        if l > 0:
            parts_ffn[l] = _exchange_wait(state["ffn"], True, f"scatter_wait_ffn_{l}", after=dcur)
            state["mix"] = _exchange_start(slabs_mix(gw), True, f"scatter_start_mix_{l}", after=parts_ffn[l][0])
        else:
            parts_mix[0] = _exchange_wait(state["mix"], True, "scatter_wait_mix_0", after=dcur)
    grad_x = dcur.reshape(x.shape)
    parts = [[parts_mix[l][0] for l in range(DEPTH)], [parts_mix[l][1] for l in range(DEPTH)],
             [parts_mix[l][2] for l in range(DEPTH)], [parts_ffn[l][0] for l in range(DEPTH)],
             [parts_ffn[l][1] for l in range(DEPTH)]]

    def update(p, w, m, v, name, tr):
        shp = w.shape
        r = shp[0] * shp[1]
        flat = lambda a: a.reshape(r, shp[2])
        outs = _adamw(p, flat(w), flat(m), flat(v), name, tr)
        return [o.reshape(shp) for o in outs]

    u_in = update(parts[0], w_in, m_w_in, v_w_in, "adamw_w_in", 256)
    u_co = update(parts[1], w_conv_out, m_w_conv_out, v_w_conv_out, "adamw_w_conv_out", 512)
    u_out = update(parts[2], w_out, m_w_out, v_w_out, "adamw_w_out", 128)
    u_gu = update(parts[3], w_gate_up, m_w_gate_up, v_w_gate_up, "adamw_w_gate_up", 256)
    u_dn = update(parts[4], w_down, m_w_down, v_w_down, "adamw_w_down", 176)

    def fold_rows(a):
        return jnp.sum(a, axis=0)

    def fold_heads(a):
        return jnp.sum(a, axis=0).reshape(2, HEAD_DIM).sum(axis=0)

    small_g = {
        "norm_mix": jnp.stack([fold_rows(gss[l]["norm_mix"]) for l in range(DEPTH)]),
        "q_norm": jnp.stack([fold_heads(gss[l]["gq"]) for l in range(DEPTH)]),
        "k_norm": jnp.stack([fold_heads(gss[l]["gk"]) for l in range(DEPTH)]),
        "sinks": jnp.stack([gss[l]["sinks"][0, :N_HEADS] for l in range(DEPTH)]),
        "conv_w": jnp.stack([gss[l]["conv_w"].reshape(CONV_WIDTH, SUBLANES, CONV_CH).sum(axis=1)
                             for l in range(DEPTH)]),
        "conv_b": jnp.stack([fold_rows(gss[l]["conv_b"]) for l in range(DEPTH)]),
        "conv_ln_g": jnp.stack([fold_rows(gss[l]["ln_g"]) for l in range(DEPTH)]),
        "conv_ln_b": jnp.stack([fold_rows(gss[l]["ln_b"]) for l in range(DEPTH)]),
        "norm_ffn": jnp.stack([fold_rows(gss[l]["norm_ffn"]) for l in range(DEPTH)]),
    }
    (small_parts,) = _exchange([_pack_small(small_g)], scatter=False, name="gather_small_grads")
    shapes = {"norm_mix": norm_mix.shape, "q_norm": q_norm.shape, "k_norm": k_norm.shape, "sinks": sinks.shape,
              "conv_w": (DEPTH, CONV_WIDTH, CONV_CH), "conv_b": conv_b.shape, "conv_ln_g": conv_ln_g.shape,
              "conv_ln_b": conv_ln_b.shape, "norm_ffn": norm_ffn.shape}

    def widen(a):
        z = jnp.zeros((DEPTH, CONV_WIDTH, N_DEV, CONV_CH // N_DEV), F32)
        z = lax.dynamic_update_slice(z, a[:, :, None, :], (0, 0, me, 0))
        return z.reshape(DEPTH, CONV_WIDTH, CONV_CH)

    sw = _pack_small(dict(norm_mix=norm_mix, q_norm=q_norm, k_norm=k_norm, sinks=sinks, conv_w=widen(conv_w),
                          conv_b=conv_b, conv_ln_g=conv_ln_g, conv_ln_b=conv_ln_b, norm_ffn=norm_ffn))
    sm = _pack_small(dict(norm_mix=m_norm_mix, q_norm=m_q_norm, k_norm=m_k_norm, sinks=m_sinks,
                          conv_w=widen(m_conv_w), conv_b=m_conv_b, conv_ln_g=m_conv_ln_g, conv_ln_b=m_conv_ln_b,
                          norm_ffn=m_norm_ffn))
    sv_ = _pack_small(dict(norm_mix=v_norm_mix, q_norm=v_q_norm, k_norm=v_k_norm, sinks=v_sinks,
                           conv_w=widen(v_conv_w), conv_b=v_conv_b,
                           conv_ln_g=v_conv_ln_g, conv_ln_b=v_conv_ln_b, norm_ffn=v_norm_ffn))
    s_outs = [_unpack_small(o, shapes) for o in _adamw([small_parts], sw, sm, sv_, "adamw_small", SMALL_ROWS)]

    def narrow(a):
        a4 = a.reshape(DEPTH, CONV_WIDTH, N_DEV, CONV_CH // N_DEV)
        return lax.dynamic_slice(a4, (0, 0, me, 0), (DEPTH, CONV_WIDTH, 1, CONV_CH // N_DEV)).reshape(
            DEPTH, CONV_WIDTH, CONV_CH // N_DEV)

    big = {"w_in": u_in, "w_conv_out": u_co, "w_out": u_out, "w_gate_up": u_gu, "w_down": u_dn}
    order = ["norm_mix", "w_in", "q_norm", "k_norm", "sinks", "conv_w", "conv_b", "conv_ln_g", "conv_ln_b",
             "w_conv_out", "w_out", "norm_ffn", "w_gate_up", "w_down"]
    outs = [loss, grad_x]
    for kind in range(4):
        for name in order:
            if name in big:
                outs.append(big[name][kind])
            elif name == "conv_w":
                outs.append(narrow(s_outs[kind][name]))
            else:
                outs.append(s_outs[kind][name])
    return tuple(outs)
```
